```python
import jax
import jax.numpy as jnp
from jax import lax
import numpy as np

D_MODEL = 2048
BATCH = 2
SEQ = 8192
DEPTH = 4

GRID_W = 64
CTX_LEN = 256
HEAD_DIM = 64
D_RWKV = D_MODEL // 2
N_RWKV_HEADS = D_RWKV // HEAD_DIM
D_CONV = D_MODEL // 4
D_FOURIER = D_MODEL // 4
N_FOURIER_GROUPS = D_FOURIER // HEAD_DIM
D_MIX = D_RWKV + D_CONV + D_FOURIER
DECAY_LORA = 96
ICLR_LORA = 96
GATE_LORA = 256
D_FF = ((8 * D_MODEL + 3 * 256 - 1) // (3 * 256)) * 256
N_MOD = 6
RMS_EPS = 1e-6
GN_EPS = 64e-5
KK_EPS = 1e-12

R0 = 0
K0 = R0 + D_RWKV
V0 = K0 + D_RWKV
WD0 = V0 + D_RWKV
AD0 = WD0 + DECAY_LORA
GD0 = AD0 + ICLR_LORA
RW_COLS = GD0 + GATE_LORA
CG0 = RW_COLS
CX0 = CG0 + D_CONV
CB0 = CX0 + D_CONV
FT0 = CB0 + D_CONV
D_IN = FT0 + D_FOURIER

kernel_name = "hybrid_rwkv7_shortconv_fnet_dit"


def rms_norm(x, g):
    xf = x.astype(jnp.float32)
    y = xf * lax.rsqrt(jnp.mean(xf * xf, axis=-1, keepdims=True) + RMS_EPS)
    return (y * g.astype(jnp.float32)).astype(x.dtype)


def adaln(cvec, w, b):
    m = jax.nn.silu(cvec) @ w + b
    return jnp.split(m[:, None, :], N_MOD, axis=-1)


def modulate(h, shift, scale):
    return h * (1.0 + scale) + shift


def conv3(u, w):
    pad = [(0, 0)] * (u.ndim - 2) + [(1, 1), (0, 0)]
    up = jnp.pad(u, pad)
    return up[..., :-2, :] * w[0] + up[..., 1:-1, :] * w[1] + up[..., 2:, :] * w[2]


def grid_conv3(u, w):
    b, l, ch = u.shape
    rows = l // GRID_W
    return conv3(u.reshape(b, rows, GRID_W, ch), w).reshape(b, l, ch)


def rwkv_streams(rw, dec_w0, dec_up, iclr_a0, iclr_up, k_k, k_a):
    rw = rw.astype(jnp.float32)
    b, l = rw.shape[:2]

    def heads(t):
        return t.reshape(b, l, N_RWKV_HEADS, HEAD_DIM)

    r = rw[..., R0:K0]
    k = rw[..., K0:V0]
    v = rw[..., V0:WD0]
    wd = jnp.tanh(rw[..., WD0:AD0])
    ad = rw[..., AD0:GD0]
    gd = rw[..., GD0:RW_COLS]
    kk = heads(k * k_k)
    kk = kk / jnp.maximum(jnp.sqrt(jnp.sum(kk * kk, axis=-1, keepdims=True)), KK_EPS)
    per_dir = []
    for d in range(2):
        w_log = -jax.nn.softplus(-(dec_w0[d] + wd @ dec_up[d])) - 0.5
        a = jax.nn.sigmoid(iclr_a0[d] + ad @ iclr_up[d])
        kd = k * (1.0 + (a - 1.0) * k_a)
        per_dir.append((heads(jnp.exp(-jnp.exp(w_log))), heads(kd), heads(a)))
    return heads(r), heads(v), kk, per_dir, gd


def wkv_scan(r, w, k, v, kk, a, s0, reverse):
    def step(s, inp):
        r_t, w_t, k_t, v_t, kk_t, a_t = inp
        sa = jnp.einsum('bhij,bhj->bhi', s, -kk_t)
        s = (s * w_t[:, :, None, :] + sa[..., None] * (kk_t * a_t)[:, :, None, :]
             + v_t[..., None] * k_t[:, :, None, :])
        return s, jnp.einsum('bhij,bhj->bhi', s, r_t)

    xs = tuple(jnp.moveaxis(t, 1, 0) for t in (r, w, k, v, kk, a))
    s_fin, ys = lax.scan(step, s0, xs, reverse=reverse)
    return jnp.moveaxis(ys, 0, 1), s_fin


def rwkv_mix(streams, s0, ln_w, ln_b, r_k, g_up):
    r, v, kk, per_dir, gd = streams
    (wf, kf, af), (wb, kb, ab) = per_dir
    yf, sf = wkv_scan(r, wf, kf, v, kk, af, s0[0], False)
    yb, sb = wkv_scan(r, wb, kb, v, kk, ab, s0[1], True)
    y = yf + yb
    mu = jnp.mean(y, axis=-1, keepdims=True)
    var = jnp.mean(jnp.square(y - mu), axis=-1, keepdims=True)
    y = ((y - mu) * lax.rsqrt(var + GN_EPS) * ln_w.reshape(N_RWKV_HEADS, HEAD_DIM)
         + ln_b.reshape(N_RWKV_HEADS, HEAD_DIM))
    y = y + jnp.sum(r * (kf + kb) * r_k, axis=-1, keepdims=True) * v
    b, l = y.shape[:2]
    g = jax.nn.sigmoid(gd) @ g_up
    return y.reshape(b, l, D_RWKV) * g, (sf, sb)


def short_conv_mix(p, conv_w, conv_fn):
    return p[..., CB0:FT0] * conv_fn(p[..., CG0:CX0] * p[..., CX0:CB0], conv_w)


def fourier_mix(p):
    u = p[..., FT0:D_IN].astype(jnp.float32)
    b, l = u.shape[:2]
    u = u.reshape(b, l, N_FOURIER_GROUPS, HEAD_DIM)
    y = jnp.fft.fftn(u, axes=(1, 3), norm='ortho').real
    return y.reshape(b, l, D_FOURIER).astype(p.dtype)


def ffn(h, wg, wu, wd):
    return (jax.nn.silu(h @ wg) * (h @ wu)) @ wd


def setup_inputs(seed: int = 0) -> dict:
    key = jax.random.key(seed)
    ks = jax.random.split(key, 32)
    f32 = jnp.float32

    def nrm(k, shape, s):
        return jax.random.normal(k, shape, f32) * s

    L = DEPTH
    left = jax.random.uniform(ks[8], (L, 1, RW_COLS), f32, 0.0, 0.4)
    right = jax.random.uniform(ks[9], (L, 1, RW_COLS), f32, 0.0, 0.4)
    rw_shift = jnp.concatenate([left, 1.0 - 0.5 * (left + right), right], axis=1)
    return {
        'x': nrm(ks[0], (BATCH, SEQ, D_MODEL), 1.0),
        'c': nrm(ks[1], (BATCH, D_MODEL), 1.0),
        'ctx': nrm(ks[2], (BATCH, CTX_LEN, D_MODEL), 1.0),
        'c_ctx': nrm(ks[3], (D_MODEL,), 1.0),
        'w_mod': nrm(ks[4], (L, D_MODEL, N_MOD * D_MODEL), 0.5 * D_MODEL ** -0.5),
        'b_mod': nrm(ks[5], (L, N_MOD * D_MODEL), 0.02),
        'norm_mix': 1.0 + nrm(ks[6], (L, D_MODEL), 0.05),
        'w_in': nrm(ks[7], (L, D_MODEL, D_IN), D_MODEL ** -0.5),
        'rw_shift': rw_shift,
        'dec_w0': jax.random.uniform(ks[10], (L, 2, D_RWKV), f32, -6.0, 1.0),
        'dec_up': nrm(ks[11], (L, 2, DECAY_LORA, D_RWKV), 0.5 * DECAY_LORA ** -0.5),
        'iclr_a0': nrm(ks[12], (L, 2, D_RWKV), 0.1),
        'iclr_up': nrm(ks[13], (L, 2, ICLR_LORA, D_RWKV), 0.5 * ICLR_LORA ** -0.5),
        'k_k': 0.85 + nrm(ks[14], (L, D_RWKV), 0.05),
        'k_a': 1.0 + nrm(ks[15], (L, D_RWKV), 0.05),
        'r_k': nrm(ks[16], (L, N_RWKV_HEADS, HEAD_DIM), 0.1),
        'ln_w': 1.0 + nrm(ks[17], (L, D_RWKV), 0.05),
        'ln_b': nrm(ks[18], (L, D_RWKV), 0.02),
        'g_up': nrm(ks[19], (L, GATE_LORA, D_RWKV), GATE_LORA ** -0.5),
        'conv_w': nrm(ks[20], (L, 3, D_CONV), 0.5),
        'w_out': nrm(ks[21], (L, D_MIX, D_MODEL), D_MIX ** -0.5),
        'norm_ffn': 1.0 + nrm(ks[22], (L, D_MODEL), 0.05),
        'w_gate': nrm(ks[23], (L, D_MODEL, D_FF), D_MODEL ** -0.5),
        'w_up': nrm(ks[24], (L, D_MODEL, D_FF), D_MODEL ** -0.5),
        'w_down': nrm(ks[25], (L, D_FF, D_MODEL), D_FF ** -0.5),
        'norm_final': 1.0 + nrm(ks[26], (D_MODEL,), 0.05),
    }


def reference(x, c, ctx, c_ctx, w_mod, b_mod, norm_mix, w_in, rw_shift, dec_w0, dec_up,
              iclr_a0, iclr_up, k_k, k_a, r_k, ln_w, ln_b, g_up, conv_w, w_out,
              norm_ffn, w_gate, w_up, w_down, norm_final):
    xc = ctx
    s_zero = jnp.zeros((ctx.shape[0], N_RWKV_HEADS, HEAD_DIM, HEAD_DIM), jnp.float32)
    for i in range(DEPTH):
        sh1, sc1, ga1, sh2, sc2, ga2 = adaln(c, w_mod[i], b_mod[i])
        csh1, csc1, cga1, csh2, csc2, cga2 = adaln(c_ctx[None, :], w_mod[i], b_mod[i])
        px = modulate(rms_norm(x, norm_mix[i]), sh1, sc1) @ w_in[i]
        pc = modulate(rms_norm(xc, norm_mix[i]), csh1, csc1) @ w_in[i]
        rw_args = (dec_w0[i], dec_up[i], iclr_a0[i], iclr_up[i], k_k[i], k_a[i])
        out_args = (ln_w[i], ln_b[i], r_k[i], g_up[i])
        yc_rw, s_ctx = rwkv_mix(rwkv_streams(conv3(pc[..., :RW_COLS], rw_shift[i]), *rw_args),
                                (s_zero, s_zero), *out_args)
        yx_rw, _ = rwkv_mix(rwkv_streams(conv3(px[..., :RW_COLS], rw_shift[i]), *rw_args),
                            s_ctx, *out_args)
        yx = jnp.concatenate([yx_rw.astype(px.dtype),
                              short_conv_mix(px, conv_w[i], grid_conv3),
                              fourier_mix(px)], axis=-1) @ w_out[i]
        x = x + ga1 * yx
        x = x + ga2 * ffn(modulate(rms_norm(x, norm_ffn[i]), sh2, sc2),
                          w_gate[i], w_up[i], w_down[i])
        if i < DEPTH - 1:
            yc = jnp.concatenate([yc_rw.astype(pc.dtype),
                                  short_conv_mix(pc, conv_w[i], conv3),
                                  fourier_mix(pc)], axis=-1) @ w_out[i]
            xc = xc + cga1 * yc
            xc = xc + cga2 * ffn(modulate(rms_norm(xc, norm_ffn[i]), csh2, csc2),
                                 w_gate[i], w_up[i], w_down[i])
    return rms_norm(x, norm_final)
```

```python
import functools
import math

import jax
import jax.numpy as jnp
from jax import lax
from jax.experimental import pallas as pl
from jax.experimental.pallas import tpu as pltpu

F32 = jnp.float32
BF16 = jnp.bfloat16

HEAD = 64
GROUP_HEADS = 4
GW = GROUP_HEADS * HEAD
CHUNK = 64
HALO = 8
LORA_PAD = 128
GRID_W = 64
N_MOD = 6
RMS_EPS = 1e-6
GN_EPS = 64e-5
KK_EPS = 1e-12
VMEM_LIMIT = 56 * 1024 * 1024


def _cparams(sem):
    return pltpu.CompilerParams(dimension_semantics=sem, vmem_limit_bytes=VMEM_LIMIT)


def _dot(a, b, nt=False):
    dn = (((1,), (1,)), ((), ())) if nt else (((1,), (0,)), ((), ()))
    return lax.dot_general(a.astype(BF16), b.astype(BF16), dn, preferred_element_type=F32)


def _split3(a):
    hi = a.astype(BF16)
    r1 = a - hi.astype(F32)
    mid = r1.astype(BF16)
    lo = (r1 - mid.astype(F32)).astype(BF16)
    return hi, mid, lo


def _dot_lhs_f32(a, b_exact):
    hi, mid, lo = _split3(a)
    return _dot(hi, b_exact) + (_dot(mid, b_exact) + _dot(lo, b_exact))


def _dot_rhs_f32(a_exact, b):
    hi, mid, lo = _split3(b)
    return _dot(a_exact, hi) + (_dot(a_exact, mid) + _dot(a_exact, lo))


def _rms_mod(x, g, shift, scale):
    ms = jnp.mean(x * x, axis=-1, keepdims=True)
    return (x * lax.rsqrt(ms + RMS_EPS) * g) * (1.0 + scale) + shift


def _adaln_kernel(c_ref, w_ref, b_ref, o_ref):
    c = c_ref[...]
    s = c * jax.nn.sigmoid(c)
    o_ref[...] = _dot(s, w_ref[...]) + b_ref[...]


def _adaln(cvec, w_mod, b_mod):
    nl, d, n = w_mod.shape
    tn = 1024
    return pl.pallas_call(
        _adaln_kernel,
        grid=(nl, n // tn),
        in_specs=[pl.BlockSpec((8, d), lambda l, j: (0, 0)),
                  pl.BlockSpec((None, d, tn), lambda l, j: (l, 0, j)),
                  pl.BlockSpec((None, 1, tn), lambda l, j: (l, 0, j))],
        out_specs=pl.BlockSpec((None, 8, tn), lambda l, j: (l, 0, j)),
        out_shape=jax.ShapeDtypeStruct((nl, 8, n), F32),
        compiler_params=_cparams(("parallel", "parallel")),
        name="adaln",
    )(cvec, w_mod, b_mod)


def _in_proj_kernel(x_ref, g_ref, sh_ref, sc_ref, w_ref, o_ref, xn_ref):
    @pl.when(pl.program_id(2) == 0)
    def _():
        xn_ref[...] = _rms_mod(x_ref[...], g_ref[...], sh_ref[...], sc_ref[...]).astype(BF16)

    o_ref[...] = jnp.dot(xn_ref[...], w_ref[...], preferred_element_type=F32)


def _in_proj(x, g, shift, scale, w):
    b, l, d = x.shape
    n = w.shape[1]
    tm = min(512, l)
    tn = 512
    vec = pl.BlockSpec((None, 1, d), lambda bi, i, j: (bi, 0, 0))
    return pl.pallas_call(
        _in_proj_kernel,
        grid=(b, l // tm, n // tn),
        in_specs=[pl.BlockSpec((None, tm, d), lambda bi, i, j: (bi, i, 0)),
                  pl.BlockSpec((1, d), lambda bi, i, j: (0, 0)),
                  vec, vec,
                  pl.BlockSpec((d, tn), lambda bi, i, j: (0, j))],
        out_specs=pl.BlockSpec((None, tm, tn), lambda bi, i, j: (bi, i, j)),
        out_shape=jax.ShapeDtypeStruct((b, l, n), F32),
        scratch_shapes=[pltpu.VMEM((tm, d), BF16)],
        compiler_params=_cparams(("parallel", "parallel", "arbitrary")),
        name="in_proj",
    )(x, g, shift, scale, w)


def _token_shift(main, prev_blk, next_blk, sh, has_prev, has_next):
    t = main.shape[0]
    rowid = lax.broadcasted_iota(jnp.int32, main.shape, 0)
    prev_row = jnp.where(has_prev, prev_blk[HALO - 1:HALO, :], 0.0)
    next_row = jnp.where(has_next, next_blk[0:1, :], 0.0)
    up = jnp.where(rowid == 0, prev_row, pltpu.roll(main, 1, 0))
    dn = jnp.where(rowid == t - 1, next_row, pltpu.roll(main, t - 1, 0))
    return up * sh[0:1, :] + main * sh[1:2, :] + dn * sh[2:3, :]


def _group_masks():
    r = lax.broadcasted_iota(jnp.int32, (GW, GW), 0)
    c = lax.broadcasted_iota(jnp.int32, (GW, GW), 1)
    return (r // HEAD) == (c // HEAD), r % HEAD, c % HEAD, r == c


def _rs(x, bd):
    return jnp.where(bd, jnp.concatenate([x] * GROUP_HEADS, axis=0), 0.0)


def _ls(x_rs):
    out = x_rs[0:CHUNK]
    for h in range(1, GROUP_HEADS):
        out = out + x_rs[h * CHUNK:(h + 1) * CHUNK]
    return out


def _chunk_affine(r, kd, v, kkn, aic, logw, reverse, masks):
    bd, tt, ss, eye = masks
    row = lax.broadcasted_iota(jnp.int32, (CHUNK, CHUNK), 0)
    col = lax.broadcasted_iota(jnp.int32, (CHUNK, CHUNK), 1)
    tri = jnp.where((row <= col) if reverse else (row >= col), 1.0, 0.0).astype(BF16)
    cum = _dot_rhs_f32(tri, logw)
    total = cum[0:1] if reverse else cum[CHUNK - 1:CHUNK]
    p_inv = jnp.exp(-cum)
    p_end = jnp.exp(total - cum)
    bvec = kkn * aic
    a_rs = _rs(-kkn * jnp.exp(cum - logw), bd).astype(BF16)
    r_t = r * jnp.exp(cum)
    r_rs = _rs(r_t, bd).astype(BF16)
    b_rs = _rs(bvec * p_inv, bd).astype(BF16)
    k_rs = _rs(kd * p_inv, bd).astype(BF16)
    v_rs = _rs(v, bd).astype(BF16)
    strict = (ss > tt) if reverse else (ss < tt)
    incl = (ss >= tt) if reverse else (ss <= tt)
    a_ab = jnp.where(strict, _dot(a_rs, b_rs, nt=True), 0.0)
    a_ak = jnp.where(strict, _dot(a_rs, k_rs, nt=True), 0.0)
    a_rb = jnp.where(incl, _dot(r_rs, b_rs, nt=True), 0.0).astype(BF16)
    a_rk = jnp.where(incl, _dot(r_rs, k_rs, nt=True), 0.0)
    t_inv = jnp.where(eye, 1.0, a_ab)
    pw = a_ab
    for _ in range(int(math.log2(CHUNK)) - 1):
        pw = _dot(pw, pw)
        t_inv = t_inv + _dot(t_inv, pw)
    t_inv = t_inv.astype(BF16)
    ah = _dot(t_inv, a_rs).astype(BF16)
    u0 = _dot(t_inv, _dot(a_ak, v_rs)).astype(BF16)
    rh = r_t + _ls(_dot(a_rb, ah))
    y0 = _ls(_dot(a_rb, u0) + _dot(a_rk, v_rs))
    be_t = _rs(bvec * p_end, bd).T
    ke_t = _rs(kd * p_end, bd).T
    m = jnp.where(eye, jnp.exp(total), 0.0) + _dot(be_t, ah)
    n = _dot(be_t, u0) + _dot(ke_t, v_rs)
    return y0, rh, m, n


def _head_sum(x, ones_bd):
    return _dot_lhs_f32(x, ones_bd)


def _wkv_kernel(nsub,
                rkv_f, lo_f, rkv_fp, rkv_fn, lo_fp, lo_fn,
                rkv_b, lo_b, rkv_bp, rkv_bn, lo_bp, lo_bn,
                sh_rkv, sh_lo, pvec, dec_up, iclr_up, g_up, s0,
                yf_ref, yb_ref, z_ref, gate_ref, sfin_ref,
                zst):
    c = pl.program_id(2)
    nsteps = pl.num_programs(2)

    @pl.when(c == 0)
    def _():
        zst[...] = s0[...]

    masks = _group_masks()
    ones_bd = jnp.where(masks[0], 1.0, 0.0).astype(BF16)
    pv = pvec[...]
    k_k, k_a, r_k = pv[0:1], pv[1:2], pv[2:3]
    w0 = (pv[3:4], pv[4:5])
    a0 = (pv[5:6], pv[6:7])
    srkv = sh_rkv[...]
    slo = sh_lo[...]
    sig_scale = math.exp(-0.5)

    def streams(rkv_ref, lo_ref, rkv_p, rkv_n, lo_p, lo_n, has_prev, has_next, d):
        rkv = _token_shift(rkv_ref[...], rkv_p[...], rkv_n[...], srkv, has_prev, has_next)
        lo = _token_shift(lo_ref[...], lo_p[...], lo_n[...], slo, has_prev, has_next)
        r, k, v = rkv[:, 0:GW], rkv[:, GW:2 * GW], rkv[:, 2 * GW:3 * GW]
        wd = jnp.tanh(lo[:, 0:LORA_PAD])
        ad = lo[:, LORA_PAD:2 * LORA_PAD]
        gd = lo[:, 2 * LORA_PAD:]
        kk = k * k_k
        kkn = kk / jnp.maximum(jnp.sqrt(_head_sum(kk * kk, ones_bd)), KK_EPS)
        logw = -sig_scale * jax.nn.sigmoid(w0[d] + _dot(wd, dec_up[d]))
        aic = jax.nn.sigmoid(a0[d] + _dot(ad, iclr_up[d]))
        kd = k * (1.0 + (aic - 1.0) * k_a)
        return r, k, v, kkn, logw, aic, kd, ad, gd

    cb = nsteps - 1 - c
    r, k, v, kkn, logw, aic, kd, ad, gd = streams(
        rkv_f, lo_f, rkv_fp, rkv_fn, lo_fp, lo_fn, c > 0, c < nsteps - 1, 0)
    aic_o = jax.nn.sigmoid(a0[1] + _dot(ad, iclr_up[1]))
    kd_o = k * (1.0 + (aic_o - 1.0) * k_a)
    z_ref[...] = _head_sum(r * (kd + kd_o) * r_k, ones_bd) * v
    gate_ref[...] = _dot(jax.nn.sigmoid(gd), g_up[...])
    fwd = [_chunk_affine(*(t[j * CHUNK:(j + 1) * CHUNK] for t in (r, kd, v, kkn, aic, logw)), False, masks)
           for j in range(nsub)]
    r, k, v, kkn, logw, aic, kd, ad, gd = streams(
        rkv_b, lo_b, rkv_bp, rkv_bn, lo_bp, lo_bn, cb > 0, cb < nsteps - 1, 1)
    bwd = [_chunk_affine(*(t[j * CHUNK:(j + 1) * CHUNK] for t in (r, kd, v, kkn, aic, logw)), True, masks)
           for j in range(nsub)]

    z = zst[0]
    for j in range(nsub):
        y0, rh, m, n = fwd[j]
        yf_ref[j * CHUNK:(j + 1) * CHUNK, :] = y0 + _dot(rh, z)
        z = _dot(m, z) + n
    zst[0] = z
    zf = z
    z = zst[1]
    for j in reversed(range(nsub)):
        y0, rh, m, n = bwd[j]
        yb_ref[j * CHUNK:(j + 1) * CHUNK, :] = y0 + _dot(rh, z)
        z = _dot(m, z) + n
    zst[1] = z

    @pl.when(c == nsteps - 1)
    def _():
        sfin_ref[0] = zf
        sfin_ref[1] = z


def _wkv(px, lw, s0, nsub):
    b, l, _ = px.shape
    ng = lw["ng"]
    d_rwkv = ng * GW
    t = nsub * CHUNK
    nsteps = l // t
    hb = t // HALO
    nhb = l // HALO
    lo_blk = 3 * d_rwkv // (4 * LORA_PAD)
    rkv_w = 3 * GW
    lo_w = 4 * LORA_PAD

    def main_specs(blk):
        return [pl.BlockSpec((None, t, rkv_w), lambda bi, g, c: (bi, blk(c), g)),
                pl.BlockSpec((None, t, lo_w), lambda bi, g, c: (bi, blk(c), lo_blk))]

    def halo_specs(blk):
        prev = lambda c: jnp.maximum(blk(c) * hb - 1, 0)
        nxt = lambda c: jnp.minimum((blk(c) + 1) * hb, nhb - 1)
        return [pl.BlockSpec((None, HALO, rkv_w), lambda bi, g, c: (bi, prev(c), g)),
                pl.BlockSpec((None, HALO, rkv_w), lambda bi, g, c: (bi, nxt(c), g)),
                pl.BlockSpec((None, HALO, lo_w), lambda bi, g, c: (bi, prev(c), lo_blk)),
                pl.BlockSpec((None, HALO, lo_w), lambda bi, g, c: (bi, nxt(c), lo_blk))]

    fblk = lambda c: c
    bblk = lambda c: nsteps - 1 - c
    in_specs = (main_specs(fblk) + halo_specs(fblk) + main_specs(bblk) + halo_specs(bblk) + [
        pl.BlockSpec((3, rkv_w), lambda bi, g, c: (0, g)),
        pl.BlockSpec((3, lo_w), lambda bi, g, c: (0, 0)),
        pl.BlockSpec((8, GW), lambda bi, g, c: (0, g)),
        pl.BlockSpec((2, LORA_PAD, GW), lambda bi, g, c: (0, 0, g)),
        pl.BlockSpec((2, LORA_PAD, GW), lambda bi, g, c: (0, 0, g)),
        pl.BlockSpec((2 * LORA_PAD, GW), lambda bi, g, c: (0, g)),
        pl.BlockSpec((None, 2, None, GW, GW), lambda bi, g, c: (bi, 0, g, 0, 0)),
    ])
    yspec = lambda blk: pl.BlockSpec((None, t, GW), lambda bi, g, c: (bi, blk(c), g))
    out_specs = [yspec(fblk), yspec(bblk), yspec(fblk), yspec(fblk),
                 pl.BlockSpec((None, 2, None, GW, GW), lambda bi, g, c: (bi, 0, g, 0, 0))]
    ysh = jax.ShapeDtypeStruct((b, l, d_rwkv), F32)
    return pl.pallas_call(
        functools.partial(_wkv_kernel, nsub),
        grid=(b, ng, nsteps),
        in_specs=in_specs,
        out_specs=out_specs,
        out_shape=[ysh, ysh, ysh, ysh, jax.ShapeDtypeStruct(s0.shape, F32)],
        scratch_shapes=[pltpu.VMEM((2, GW, GW), F32)],
        compiler_params=_cparams(("parallel", "parallel", "arbitrary")),
        name="wkv",
    )(px, px, px, px, px, px, px, px, px, px, px, px,
      lw["sh_rkv"], lw["sh_lo"], lw["pvec"], lw["dec_up"], lw["iclr_up"], lw["g_up"], s0)


def _dft_ch_kernel(u_ref, t_ref, o_ref):
    u = u_ref[...]
    hi = u.astype(BF16)
    lo = (u - hi.astype(F32)).astype(BF16)
    tab = t_ref[...]
    o_ref[...] = (_dot(hi, tab) + _dot(lo, tab)).astype(BF16)


def _dft_channels(px, tabs, ft_blk, d_f):
    b, l, _ = px.shape
    tm = min(512, l)
    nt = l // tm
    return pl.pallas_call(
        _dft_ch_kernel,
        grid=(b, 2, nt),
        in_specs=[pl.BlockSpec((None, tm, d_f), lambda bi, s, i: (bi, i, ft_blk)),
                  pl.BlockSpec((None, d_f, d_f), lambda bi, s, i: (s, 0, 0))],
        out_specs=pl.BlockSpec((tm, d_f), lambda bi, s, i: (s * nt + i, bi)),
        out_shape=jax.ShapeDtypeStruct((2 * l, b * d_f), BF16),
        compiler_params=_cparams(("parallel", "parallel", "parallel")),
        name="dft_channels",
    )(px, tabs)


def _dft_seq_kernel(w_ref, u_ref, o_ref, acc_ref):
    kk = pl.program_id(1)

    @pl.when(kk == 0)
    def _():
        acc_ref[...] = jnp.zeros_like(acc_ref)

    acc_ref[...] += jnp.dot(w_ref[...], u_ref[...], preferred_element_type=F32)

    @pl.when(kk == pl.num_programs(1) - 1)
    def _():
        o_ref[...] = acc_ref[...]


def _dft_seq(wtab, uu):
    l, k2 = wtab.shape
    n = uu.shape[1]
    tm = min(1024, l)
    tk = min(2048, k2)
    return pl.pallas_call(
        _dft_seq_kernel,
        grid=(l // tm, k2 // tk),
        in_specs=[pl.BlockSpec((tm, tk), lambda i, kk: (i, kk)),
                  pl.BlockSpec((tk, n), lambda i, kk: (kk, 0))],
        out_specs=pl.BlockSpec((tm, n), lambda i, kk: (i, 0)),
        out_shape=jax.ShapeDtypeStruct((l, n), F32),
        scratch_shapes=[pltpu.VMEM((tm, n), F32)],
        compiler_params=_cparams(("parallel", "arbitrary")),
        name="dft_seq",
    )(wtab, uu)


def _dft_tables(l, d_f):
    p = lax.broadcasted_iota(jnp.int32, (l, l), 0)
    q = lax.broadcasted_iota(jnp.int32, (l, l), 1)
    ang = ((p * q) % l).astype(F32) * (2.0 * math.pi / l)
    wtab = jnp.concatenate([jnp.cos(ang), -jnp.sin(ang)], axis=1).astype(BF16)
    cc = lax.broadcasted_iota(jnp.int32, (d_f, d_f), 0)
    qq = lax.broadcasted_iota(jnp.int32, (d_f, d_f), 1)
    a2 = (((cc % HEAD) * (qq % HEAD)) % HEAD).astype(F32) * (2.0 * math.pi / HEAD)
    same = (cc // HEAD) == (qq // HEAD)
    scale = 1.0 / math.sqrt(l * HEAD)
    ctab = jnp.stack([jnp.where(same, jnp.cos(a2), 0.0), jnp.where(same, jnp.sin(a2), 0.0)]) * scale
    return wtab, ctab.astype(BF16)


def _mix_out_kernel(period, yf_ref, yb_ref, z_ref, gate_ref, cg_ref, cx_ref, cb_ref, ft_ref, x_ref,
                    ga_ref, lnw_ref, lnb_ref, cw_ref, w_ref, o_ref, mix_ref):
    d_rwkv = yf_ref.shape[1]
    d_conv = cg_ref.shape[1]
    tm = yf_ref.shape[0]
    r = lax.broadcasted_iota(jnp.int32, (GW, GW), 0)
    c = lax.broadcasted_iota(jnp.int32, (GW, GW), 1)
    ones_bd = jnp.where((r // HEAD) == (c // HEAD), 1.0, 0.0).astype(BF16)
    inv_n = 1.0 / HEAD
    for s in range(d_rwkv // GW):
        sl = slice(s * GW, (s + 1) * GW)
        y = yf_ref[:, sl] + yb_ref[:, sl]
        mu = _dot_lhs_f32(y, ones_bd) * inv_n
        dlt = y - mu
        var = _dot_lhs_f32(dlt * dlt, ones_bd) * inv_n
        yn = dlt * lax.rsqrt(var + GN_EPS) * lnw_ref[:, sl] + lnb_ref[:, sl] + z_ref[:, sl]
        mix_ref[:, sl] = (yn * gate_ref[:, sl]).astype(BF16)
    u = cg_ref[...] * cx_ref[...]
    rowid = lax.broadcasted_iota(jnp.int32, u.shape, 0) % period
    up = jnp.where(rowid == 0, 0.0, pltpu.roll(u, 1, 0))
    dn = jnp.where(rowid == period - 1, 0.0, pltpu.roll(u, tm - 1, 0))
    cw = cw_ref[...]
    conv = cb_ref[...] * (up * cw[0:1] + u * cw[1:2] + dn * cw[2:3])
    mix_ref[:, d_rwkv:d_rwkv + d_conv] = conv.astype(BF16)
    mix_ref[:, d_rwkv + d_conv:] = ft_ref[...].astype(BF16)
    o_ref[...] = x_ref[...] + ga_ref[...] * jnp.dot(mix_ref[...], w_ref[...], preferred_element_type=F32)


def _mix_out(yf, yb, z, gate, px, ft, x, ga, lw, period, conv_blk):
    b, l, d = x.shape
    d_rwkv = yf.shape[2]
    d_conv = lw["conv_w"].shape[1]
    d_f = ft.shape[1] // b
    tm = min(256, l)
    assert tm % period == 0
    yspec = pl.BlockSpec((None, tm, d_rwkv), lambda bi, i: (bi, i, 0))
    cspec = lambda off: pl.BlockSpec((None, tm, d_conv), lambda bi, i: (bi, i, conv_blk + off))
    return pl.pallas_call(
        functools.partial(_mix_out_kernel, period),
        grid=(b, l // tm),
        in_specs=[yspec, yspec, yspec, yspec, cspec(0), cspec(1), cspec(2),
                  pl.BlockSpec((tm, d_f), lambda bi, i: (i, bi)),
                  pl.BlockSpec((None, tm, d), lambda bi, i: (bi, i, 0)),
                  pl.BlockSpec((None, 1, d), lambda bi, i: (bi, 0, 0)),
                  pl.BlockSpec((1, d_rwkv), lambda bi, i: (0, 0)),
                  pl.BlockSpec((1, d_rwkv), lambda bi, i: (0, 0)),
                  pl.BlockSpec((3, d_conv), lambda bi, i: (0, 0)),
                  pl.BlockSpec(lw["w_out"].shape, lambda bi, i: (0, 0))],
        out_specs=pl.BlockSpec((None, tm, d), lambda bi, i: (bi, i, 0)),
        out_shape=jax.ShapeDtypeStruct((b, l, d), F32),
        scratch_shapes=[pltpu.VMEM((tm, lw["w_out"].shape[0]), BF16)],
        compiler_params=_cparams(("parallel", "parallel")),
        name="mix_out",
    )(yf, yb, z, gate, px, px, px, ft, x, ga, lw["ln_w"], lw["ln_b"], lw["conv_w"], lw["w_out"])


def _ffn_kernel(final, x_ref, g_ref, sh_ref, sc_ref, ga_ref, wg_ref, wu_ref, wd_ref, gf_ref, o_ref,
                h_ref, acc_ref):
    j = pl.program_id(2)

    @pl.when(j == 0)
    def _():
        h_ref[...] = _rms_mod(x_ref[...], g_ref[...], sh_ref[...], sc_ref[...]).astype(BF16)
        acc_ref[...] = jnp.zeros_like(acc_ref)

    h = h_ref[...]
    a = jnp.dot(h, wg_ref[...], preferred_element_type=F32)
    u = jnp.dot(h, wu_ref[...], preferred_element_type=F32)
    t = (a * jax.nn.sigmoid(a)) * u
    acc_ref[...] += jnp.dot(t.astype(BF16), wd_ref[...], preferred_element_type=F32)

    @pl.when(j == pl.num_programs(2) - 1)
    def _():
        xn = x_ref[...] + ga_ref[...] * acc_ref[...]
        if final:
            ms = jnp.mean(xn * xn, axis=-1, keepdims=True)
            xn = xn * lax.rsqrt(ms + RMS_EPS) * gf_ref[...]
        o_ref[...] = xn


def _ffn(x, g, shift, scale, ga, wg, wu, wd, g_final, final):
    b, l, d = x.shape
    ff = wg.shape[1]
    tm = min(512, l)
    tf = 512
    vec = pl.BlockSpec((None, 1, d), lambda bi, i, j: (bi, 0, 0))
    gspec = pl.BlockSpec((1, d), lambda bi, i, j: (0, 0))
    return pl.pallas_call(
        functools.partial(_ffn_kernel, final),
        grid=(b, l // tm, ff // tf),
        in_specs=[pl.BlockSpec((None, tm, d), lambda bi, i, j: (bi, i, 0)),
                  gspec, vec, vec, vec,
                  pl.BlockSpec((d, tf), lambda bi, i, j: (0, j)),
                  pl.BlockSpec((d, tf), lambda bi, i, j: (0, j)),
                  pl.BlockSpec((tf, d), lambda bi, i, j: (j, 0)),
                  gspec],
        out_specs=pl.BlockSpec((None, tm, d), lambda bi, i, j: (bi, i, 0)),
        out_shape=jax.ShapeDtypeStruct((b, l, d), F32),
        scratch_shapes=[pltpu.VMEM((tm, d), BF16), pltpu.VMEM((tm, d), F32)],
        compiler_params=_cparams(("parallel", "parallel", "arbitrary")),
        name="ffn",
    )(x, g, shift, scale, ga, wg, wu, wd, g_final)


def _layer_weights(i, w_in, rw_shift, dec_w0, dec_up, iclr_a0, iclr_up, k_k, k_a, r_k, ln_w, ln_b, g_up,
                   conv_w, w_out, w_gate, w_up, w_down, dims):
    d_rwkv, d_conv, d_f, dl, il, gl = dims
    ng = d_rwkv // GW
    wi = w_in[i]
    sh = rw_shift[i]
    r0, k0, v0 = 0, d_rwkv, 2 * d_rwkv
    wd0 = 3 * d_rwkv
    ad0 = wd0 + dl
    gd0 = ad0 + il
    rest0 = gd0 + gl

    def regroup(m):
        cols = []
        for g in range(ng):
            for base in (r0, k0, v0):
                cols.append(m[:, base + g * GW: base + (g + 1) * GW])
        padl = jnp.zeros((m.shape[0], LORA_PAD - dl), m.dtype)
        padi = jnp.zeros((m.shape[0], LORA_PAD - il), m.dtype)
        cols += [m[:, wd0:ad0], padl, m[:, ad0:gd0], padi, m[:, gd0:rest0]]
        return jnp.concatenate(cols, axis=1)

    w_rw = regroup(wi)
    sh_all = regroup(sh)
    pad_rows = lambda m, n: jnp.concatenate([m, jnp.zeros(m.shape[:-2] + (n - m.shape[-2], m.shape[-1]), m.dtype)], -2)
    zero = jnp.zeros((d_rwkv,), F32)
    pvec = jnp.stack([k_k[i], k_a[i], r_k[i].reshape(-1), dec_w0[i, 0], dec_w0[i, 1],
                      iclr_a0[i, 0], iclr_a0[i, 1], zero])
    return dict(
        ng=ng,
        w_in=jnp.concatenate([w_rw, wi[:, rest0:]], axis=1).astype(BF16),
        sh_rkv=sh_all[:, :3 * d_rwkv], sh_lo=sh_all[:, 3 * d_rwkv:],
        pvec=pvec,
        dec_up=pad_rows(dec_up[i], LORA_PAD).astype(BF16),
        iclr_up=pad_rows(iclr_up[i], LORA_PAD).astype(BF16),
        g_up=g_up[i].astype(BF16),
        ln_w=ln_w[i][None, :], ln_b=ln_b[i][None, :],
        conv_w=conv_w[i],
        w_out=w_out[i].astype(BF16),
        w_gate=w_gate[i].astype(BF16), w_up=w_up[i].astype(BF16), w_down=w_down[i].astype(BF16),
    )


def kernel(x, c, ctx, c_ctx, w_mod, b_mod, norm_mix, w_in, rw_shift, dec_w0, dec_up, iclr_a0, iclr_up,
           k_k, k_a, r_k, ln_w, ln_b, g_up, conv_w, w_out, norm_ffn, w_gate, w_up, w_down, norm_final):
    b, l, d = x.shape
    lc = ctx.shape[1]
    depth = w_mod.shape[0]
    d_rwkv = k_k.shape[1]
    d_conv = conv_w.shape[2]
    dl, il, gl = dec_up.shape[2], iclr_up.shape[2], g_up.shape[1]
    d_f = w_out.shape[1] - d_rwkv - d_conv
    assert dl <= LORA_PAD and il <= LORA_PAD and gl == 2 * LORA_PAD and d_rwkv % GW == 0
    assert l % GRID_W == 0 and l % CHUNK == 0 and lc % CHUNK == 0
    ng = d_rwkv // GW
    n_rw = 3 * d_rwkv + 4 * LORA_PAD
    conv_blk = n_rw // d_conv
    ft_blk = (n_rw + 3 * d_conv) // d_f
    assert conv_blk * d_conv == n_rw and ft_blk * d_f == n_rw + 3 * d_conv

    cvec = jnp.concatenate([c, c_ctx[None, :], jnp.zeros((8 - b - 1, d), F32)], axis=0)
    mods = _adaln(cvec, w_mod, b_mod[:, None, :])

    def mod(i, j, ctx_rows):
        m = mods[i, :, j * d:(j + 1) * d]
        if ctx_rows:
            return jnp.broadcast_to(m[b:b + 1], (b, d))[:, None, :]
        return m[:b][:, None, :]

    wtab_x, ctab_x = _dft_tables(l, d_f)
    wtab_c, ctab_c = _dft_tables(lc, d_f)
    dims = (d_rwkv, d_conv, d_f, dl, il, gl)
    s_zero = jnp.zeros((b, 2, ng, GW, GW), F32)
    gfin = norm_final[None, :]
    xc = ctx
    for i in range(depth):
        lw = _layer_weights(i, w_in, rw_shift, dec_w0, dec_up, iclr_a0, iclr_up, k_k, k_a, r_k, ln_w, ln_b,
                            g_up, conv_w, w_out, w_gate, w_up, w_down, dims)
        gmix = norm_mix[i][None, :]
        gffn = norm_ffn[i][None, :]
        last = i == depth - 1
        pc = _in_proj(xc, gmix, mod(i, 0, True), mod(i, 1, True), lw["w_in"])
        px = _in_proj(x, gmix, mod(i, 0, False), mod(i, 1, False), lw["w_in"])
        yfc, ybc, zc, gc, s_ctx = _wkv(pc, lw, s_zero, 1)
        yfx, ybx, zx, gx, _ = _wkv(px, lw, s_ctx, 1)
        ftx = _dft_seq(wtab_x, _dft_channels(px, ctab_x, ft_blk, d_f))
        x = _mix_out(yfx, ybx, zx, gx, px, ftx, x, mod(i, 2, False), lw, GRID_W, conv_blk)
        x = _ffn(x, gffn, mod(i, 3, False), mod(i, 4, False), mod(i, 5, False),
                 lw["w_gate"], lw["w_up"], lw["w_down"], gfin, last)
        if not last:
            ftc = _dft_seq(wtab_c, _dft_channels(pc, ctab_c, ft_blk, d_f))
            xc = _mix_out(yfc, ybc, zc, gc, pc, ftc, xc, mod(i, 2, True), lw, lc, conv_blk)
            xc = _ffn(xc, gffn, mod(i, 3, True), mod(i, 4, True), mod(i, 5, True),
                      lw["w_gate"], lw["w_up"], lw["w_down"], gfin, False)
    return x
```

```python
import functools
import math

import jax
import jax.numpy as jnp
from jax import lax
from jax.experimental import pallas as pl
from jax.experimental.pallas import tpu as pltpu

F32 = jnp.float32
BF16 = jnp.bfloat16

HEAD = 64
GROUP_HEADS = 4
GW = GROUP_HEADS * HEAD
CHUNK = 64
WKV_SUB = 4
HALO = 8
LORA_PAD = 128
GRID_W = 64
N_MOD = 6
RMS_EPS = 1e-6
GN_EPS = 64e-5
KK_EPS = 1e-12
VMEM_LIMIT = 56 * 1024 * 1024


def _cparams(sem):
    return pltpu.CompilerParams(dimension_semantics=sem, vmem_limit_bytes=VMEM_LIMIT)


def _dot(a, b, nt=False):
    dn = (((1,), (1,)), ((), ())) if nt else (((1,), (0,)), ((), ()))
    return lax.dot_general(a.astype(BF16), b.astype(BF16), dn, preferred_element_type=F32)


def _split3(a):
    hi = a.astype(BF16)
    r1 = a - hi.astype(F32)
    mid = r1.astype(BF16)
    lo = (r1 - mid.astype(F32)).astype(BF16)
    return hi, mid, lo


def _dot_lhs_f32(a, b_exact):
    hi, mid, lo = _split3(a)
    return _dot(hi, b_exact) + (_dot(mid, b_exact) + _dot(lo, b_exact))


def _dot_rhs_f32(a_exact, b):
    hi, mid, lo = _split3(b)
    return _dot(a_exact, hi) + (_dot(a_exact, mid) + _dot(a_exact, lo))


def _rms_mod(x, g, shift, scale):
    ms = jnp.mean(x * x, axis=-1, keepdims=True)
    return (x * lax.rsqrt(ms + RMS_EPS) * g) * (1.0 + scale) + shift


def _adaln_kernel(c_ref, w_ref, b_ref, o_ref):
    c = c_ref[...]
    s = c * jax.nn.sigmoid(c)
    o_ref[...] = _dot(s, w_ref[...]) + b_ref[...]


def _adaln(cvec, w_mod, b_mod):
    nl, d, n = w_mod.shape
    tn = 1024
    return pl.pallas_call(
        _adaln_kernel,
        grid=(nl, n // tn),
        in_specs=[pl.BlockSpec((8, d), lambda l, j: (0, 0)),
                  pl.BlockSpec((None, d, tn), lambda l, j: (l, 0, j)),
                  pl.BlockSpec((None, 1, tn), lambda l, j: (l, 0, j))],
        out_specs=pl.BlockSpec((None, 8, tn), lambda l, j: (l, 0, j)),
        out_shape=jax.ShapeDtypeStruct((nl, 8, n), F32),
        compiler_params=_cparams(("parallel", "parallel")),
        name="adaln",
    )(cvec, w_mod, b_mod)


def _in_proj_kernel(x_ref, g_ref, sh_ref, sc_ref, w_ref, o_ref, xn_ref):
    @pl.when(pl.program_id(2) == 0)
    def _():
        xn_ref[...] = _rms_mod(x_ref[...], g_ref[...], sh_ref[...], sc_ref[...]).astype(BF16)

    o_ref[...] = jnp.dot(xn_ref[...], w_ref[...], preferred_element_type=F32)


def _in_proj(x, g, shift, scale, w):
    b, l, d = x.shape
    n = w.shape[1]
    tm = min(1024, l)
    tn = 512
    vec = pl.BlockSpec((None, 1, d), lambda bi, i, j: (bi, 0, 0))
    return pl.pallas_call(
        _in_proj_kernel,
        grid=(b, l // tm, n // tn),
        in_specs=[pl.BlockSpec((None, tm, d), lambda bi, i, j: (bi, i, 0)),
                  pl.BlockSpec((1, d), lambda bi, i, j: (0, 0)),
                  vec, vec,
                  pl.BlockSpec((d, tn), lambda bi, i, j: (0, j))],
        out_specs=pl.BlockSpec((None, tm, tn), lambda bi, i, j: (bi, i, j)),
        out_shape=jax.ShapeDtypeStruct((b, l, n), F32),
        scratch_shapes=[pltpu.VMEM((tm, d), BF16)],
        compiler_params=_cparams(("parallel", "parallel", "arbitrary")),
        name="in_proj",
    )(x, g, shift, scale, w)


def _token_shift(main, prev_blk, next_blk, sh, has_prev, has_next):
    t = main.shape[0]
    rowid = lax.broadcasted_iota(jnp.int32, main.shape, 0)
    prev_row = jnp.where(has_prev, prev_blk[HALO - 1:HALO, :], 0.0)
    next_row = jnp.where(has_next, next_blk[0:1, :], 0.0)
    up = jnp.where(rowid == 0, prev_row, pltpu.roll(main, 1, 0))
    dn = jnp.where(rowid == t - 1, next_row, pltpu.roll(main, t - 1, 0))
    return up * sh[0:1, :] + main * sh[1:2, :] + dn * sh[2:3, :]


def _group_masks():
    r = lax.broadcasted_iota(jnp.int32, (GW, GW), 0)
    c = lax.broadcasted_iota(jnp.int32, (GW, GW), 1)
    return (r // HEAD) == (c // HEAD), r % HEAD, c % HEAD, r == c


def _rs(x, bd):
    return jnp.where(bd, jnp.concatenate([x] * GROUP_HEADS, axis=0), 0.0)


def _ls(x_rs):
    out = x_rs[0:CHUNK]
    for h in range(1, GROUP_HEADS):
        out = out + x_rs[h * CHUNK:(h + 1) * CHUNK]
    return out


def _chunk_affine(chains, masks):
    bd, tt, ss, eye = masks
    row = lax.broadcasted_iota(jnp.int32, (CHUNK, CHUNK), 0)
    col = lax.broadcasted_iota(jnp.int32, (CHUNK, CHUNK), 1)
    tri = {rev: jnp.where((row <= col) if rev else (row >= col), 1.0, 0.0).astype(BF16) for rev in (False, True)}
    strict = {False: ss < tt, True: ss > tt}
    incl = {False: ss <= tt, True: ss >= tt}
    rev = [ch[6] for ch in chains]
    n = len(chains)

    def each(f, *lists):
        return [f(*a) for a in zip(*lists)]

    logw = [ch[5] for ch in chains]
    cum = each(lambda rv, lw: _dot_rhs_f32(tri[rv], lw), rev, logw)
    total = each(lambda rv, cm: cm[0:1] if rv else cm[CHUNK - 1:CHUNK], rev, cum)
    p_inv = each(lambda cm: jnp.exp(-cm), cum)
    p_end = each(lambda tot, cm: jnp.exp(tot - cm), total, cum)
    bvec = [ch[3] * ch[4] for ch in chains]
    a_rs = each(lambda ch, cm, lw: _rs(-ch[3] * jnp.exp(cm - lw), bd).astype(BF16), chains, cum, logw)
    r_t = each(lambda ch, cm: ch[0] * jnp.exp(cm), chains, cum)
    r_rs = each(lambda x: _rs(x, bd).astype(BF16), r_t)
    b_rs = each(lambda b, p: _rs(b * p, bd).astype(BF16), bvec, p_inv)
    k_rs = each(lambda ch, p: _rs(ch[1] * p, bd).astype(BF16), chains, p_inv)
    v_rs = each(lambda ch: _rs(ch[2], bd).astype(BF16), chains)
    a_ab = each(lambda rv, a, b: jnp.where(strict[rv], _dot(a, b, nt=True), 0.0), rev, a_rs, b_rs)
    a_ak = each(lambda rv, a, k: jnp.where(strict[rv], _dot(a, k, nt=True), 0.0), rev, a_rs, k_rs)
    a_rb = each(lambda rv, r, b: jnp.where(incl[rv], _dot(r, b, nt=True), 0.0).astype(BF16), rev, r_rs, b_rs)
    a_rk = each(lambda rv, r, k: jnp.where(incl[rv], _dot(r, k, nt=True), 0.0), rev, r_rs, k_rs)
    t_inv = each(lambda a: jnp.where(eye, 1.0, a), a_ab)
    pw = a_ab
    for _ in range(int(math.log2(CHUNK)) - 1):
        pw = each(lambda p: _dot(p, p), pw)
        t_inv = each(lambda t, p: t + _dot(t, p), t_inv, pw)
    t_inv = each(lambda t: t.astype(BF16), t_inv)
    ah = each(lambda t, a: _dot(t, a).astype(BF16), t_inv, a_rs)
    akv = each(_dot, a_ak, v_rs)
    u0 = each(lambda t, x: _dot(t, x).astype(BF16), t_inv, akv)
    rh = each(lambda r, arb, a: r + _ls(_dot(arb, a)), r_t, a_rb, ah)
    y0 = each(lambda arb, u, ark, v: _ls(_dot(arb, u) + _dot(ark, v)), a_rb, u0, a_rk, v_rs)
    be_t = each(lambda b, p: _rs(b * p, bd).T, bvec, p_end)
    ke_t = each(lambda ch, p: _rs(ch[1] * p, bd).T, chains, p_end)
    m = each(lambda tot, b, a: jnp.where(eye, jnp.exp(tot), 0.0) + _dot(b, a), total, be_t, ah)
    nn = each(lambda b, u, k, v: _dot(b, u) + _dot(k, v), be_t, u0, ke_t, v_rs)
    return [(y0[i], rh[i], m[i], nn[i]) for i in range(n)]


def _head_sum(x, ones_bd):
    return _dot_lhs_f32(x, ones_bd)


def _wkv_kernel(nsub,
                rkv_f, lo_f, rkv_fp, rkv_fn, lo_fp, lo_fn,
                rkv_b, lo_b, rkv_bp, rkv_bn, lo_bp, lo_bn,
                sh_rkv, sh_lo, pvec, dec_up, iclr_up, g_up, s0,
                yf_ref, yb_ref, z_ref, gate_ref, sfin_ref,
                zst):
    c = pl.program_id(2)
    nsteps = pl.num_programs(2)

    @pl.when(c == 0)
    def _():
        zst[...] = s0[...]

    masks = _group_masks()
    ones_bd = jnp.where(masks[0], 1.0, 0.0).astype(BF16)
    pv = pvec[...]
    k_k, k_a, r_k = pv[0:1], pv[1:2], pv[2:3]
    w0 = (pv[3:4], pv[4:5])
    a0 = (pv[5:6], pv[6:7])
    srkv = sh_rkv[...]
    slo = sh_lo[...]
    sig_scale = math.exp(-0.5)

    def streams(rkv_ref, lo_ref, rkv_p, rkv_n, lo_p, lo_n, has_prev, has_next, d):
        rkv = _token_shift(rkv_ref[...], rkv_p[...], rkv_n[...], srkv, has_prev, has_next)
        lo = _token_shift(lo_ref[...], lo_p[...], lo_n[...], slo, has_prev, has_next)
        r, k, v = rkv[:, 0:GW], rkv[:, GW:2 * GW], rkv[:, 2 * GW:3 * GW]
        wd = jnp.tanh(lo[:, 0:LORA_PAD])
        ad = lo[:, LORA_PAD:2 * LORA_PAD]
        gd = lo[:, 2 * LORA_PAD:]
        kk = k * k_k
        kkn = kk / jnp.maximum(jnp.sqrt(_head_sum(kk * kk, ones_bd)), KK_EPS)
        logw = -sig_scale * jax.nn.sigmoid(w0[d] + _dot(wd, dec_up[d]))
        aic = jax.nn.sigmoid(a0[d] + _dot(ad, iclr_up[d]))
        kd = k * (1.0 + (aic - 1.0) * k_a)
        return r, k, v, kkn, logw, aic, kd, ad, gd

    cb = nsteps - 1 - c
    r, k, v, kkn, logw, aic, kd, ad, gd = streams(
        rkv_f, lo_f, rkv_fp, rkv_fn, lo_fp, lo_fn, c > 0, c < nsteps - 1, 0)
    aic_o = jax.nn.sigmoid(a0[1] + _dot(ad, iclr_up[1]))
    kd_o = k * (1.0 + (aic_o - 1.0) * k_a)
    z_ref[...] = _head_sum(r * (kd + kd_o) * r_k, ones_bd) * v
    gate_ref[...] = _dot(jax.nn.sigmoid(gd), g_up[...])
    chains = [tuple(t[j * CHUNK:(j + 1) * CHUNK] for t in (r, kd, v, kkn, aic, logw)) + (False,)
              for j in range(nsub)]
    r, k, v, kkn, logw, aic, kd, ad, gd = streams(
        rkv_b, lo_b, rkv_bp, rkv_bn, lo_bp, lo_bn, cb > 0, cb < nsteps - 1, 1)
    chains += [tuple(t[j * CHUNK:(j + 1) * CHUNK] for t in (r, kd, v, kkn, aic, logw)) + (True,)
               for j in range(nsub)]
    affine = _chunk_affine(chains, masks)
    fwd, bwd = affine[:nsub], affine[nsub:]

    z = zst[0]
    for j in range(nsub):
        y0, rh, m, n = fwd[j]
        yf_ref[j * CHUNK:(j + 1) * CHUNK, :] = y0 + _dot(rh, z)
        z = _dot(m, z) + n
    zst[0] = z
    zf = z
    z = zst[1]
    for j in reversed(range(nsub)):
        y0, rh, m, n = bwd[j]
        yb_ref[j * CHUNK:(j + 1) * CHUNK, :] = y0 + _dot(rh, z)
        z = _dot(m, z) + n
    zst[1] = z

    @pl.when(c == nsteps - 1)
    def _():
        sfin_ref[0] = zf
        sfin_ref[1] = z


def _wkv(px, lw, s0, nsub):
    b, l, _ = px.shape
    ng = lw["ng"]
    d_rwkv = ng * GW
    t = nsub * CHUNK
    nsteps = l // t
    hb = t // HALO
    nhb = l // HALO
    lo_blk = 3 * d_rwkv // (4 * LORA_PAD)
    rkv_w = 3 * GW
    lo_w = 4 * LORA_PAD

    def main_specs(blk):
        return [pl.BlockSpec((None, t, rkv_w), lambda bi, g, c: (bi, blk(c), g)),
                pl.BlockSpec((None, t, lo_w), lambda bi, g, c: (bi, blk(c), lo_blk))]

    def halo_specs(blk):
        prev = lambda c: jnp.maximum(blk(c) * hb - 1, 0)
        nxt = lambda c: jnp.minimum((blk(c) + 1) * hb, nhb - 1)
        return [pl.BlockSpec((None, HALO, rkv_w), lambda bi, g, c: (bi, prev(c), g)),
                pl.BlockSpec((None, HALO, rkv_w), lambda bi, g, c: (bi, nxt(c), g)),
                pl.BlockSpec((None, HALO, lo_w), lambda bi, g, c: (bi, prev(c), lo_blk)),
                pl.BlockSpec((None, HALO, lo_w), lambda bi, g, c: (bi, nxt(c), lo_blk))]

    fblk = lambda c: c
    bblk = lambda c: nsteps - 1 - c
    in_specs = (main_specs(fblk) + halo_specs(fblk) + main_specs(bblk) + halo_specs(bblk) + [
        pl.BlockSpec((3, rkv_w), lambda bi, g, c: (0, g)),
        pl.BlockSpec((3, lo_w), lambda bi, g, c: (0, 0)),
        pl.BlockSpec((8, GW), lambda bi, g, c: (0, g)),
        pl.BlockSpec((2, LORA_PAD, GW), lambda bi, g, c: (0, 0, g)),
        pl.BlockSpec((2, LORA_PAD, GW), lambda bi, g, c: (0, 0, g)),
        pl.BlockSpec((2 * LORA_PAD, GW), lambda bi, g, c: (0, g)),
        pl.BlockSpec((None, 2, None, GW, GW), lambda bi, g, c: (bi, 0, g, 0, 0)),
    ])
    yspec = lambda blk: pl.BlockSpec((None, t, GW), lambda bi, g, c: (bi, blk(c), g))
    out_specs = [yspec(fblk), yspec(bblk), yspec(fblk), yspec(fblk),
                 pl.BlockSpec((None, 2, None, GW, GW), lambda bi, g, c: (bi, 0, g, 0, 0))]
    ysh = jax.ShapeDtypeStruct((b, l, d_rwkv), F32)
    return pl.pallas_call(
        functools.partial(_wkv_kernel, nsub),
        grid=(b, ng, nsteps),
        in_specs=in_specs,
        out_specs=out_specs,
        out_shape=[ysh, ysh, ysh, ysh, jax.ShapeDtypeStruct(s0.shape, F32)],
        scratch_shapes=[pltpu.VMEM((2, GW, GW), F32)],
        compiler_params=_cparams(("parallel", "parallel", "arbitrary")),
        name="wkv",
    )(px, px, px, px, px, px, px, px, px, px, px, px,
      lw["sh_rkv"], lw["sh_lo"], lw["pvec"], lw["dec_up"], lw["iclr_up"], lw["g_up"], s0)


def _dft_ch_kernel(u_ref, t_ref, o_ref):
    u = u_ref[...]
    hi = u.astype(BF16)
    lo = (u - hi.astype(F32)).astype(BF16)
    tab = t_ref[...]
    o_ref[...] = (_dot(hi, tab) + _dot(lo, tab)).astype(BF16)


def _dft_channels(px, tabs, ft_blk, d_f):
    b, l, _ = px.shape
    tm = min(512, l)
    nt = l // tm
    return pl.pallas_call(
        _dft_ch_kernel,
        grid=(b, 2, nt),
        in_specs=[pl.BlockSpec((None, tm, d_f), lambda bi, s, i: (bi, i, ft_blk)),
                  pl.BlockSpec((None, d_f, d_f), lambda bi, s, i: (s, 0, 0))],
        out_specs=pl.BlockSpec((tm, d_f), lambda bi, s, i: (s * nt + i, bi)),
        out_shape=jax.ShapeDtypeStruct((2 * l, b * d_f), BF16),
        compiler_params=_cparams(("parallel", "parallel", "parallel")),
        name="dft_channels",
    )(px, tabs)


def _dft_seq_kernel(w_ref, u_ref, o_ref, acc_ref):
    kk = pl.program_id(1)

    @pl.when(kk == 0)
    def _():
        acc_ref[...] = jnp.zeros_like(acc_ref)

    acc_ref[...] += jnp.dot(w_ref[...], u_ref[...], preferred_element_type=F32)

    @pl.when(kk == pl.num_programs(1) - 1)
    def _():
        o_ref[...] = acc_ref[...]


def _dft_seq(wtab, uu):
    l, k2 = wtab.shape
    n = uu.shape[1]
    tm = min(1024, l)
    tk = min(2048, k2)
    return pl.pallas_call(
        _dft_seq_kernel,
        grid=(l // tm, k2 // tk),
        in_specs=[pl.BlockSpec((tm, tk), lambda i, kk: (i, kk)),
                  pl.BlockSpec((tk, n), lambda i, kk: (kk, 0))],
        out_specs=pl.BlockSpec((tm, n), lambda i, kk: (i, 0)),
        out_shape=jax.ShapeDtypeStruct((l, n), F32),
        scratch_shapes=[pltpu.VMEM((tm, n), F32)],
        compiler_params=_cparams(("parallel", "arbitrary")),
        name="dft_seq",
    )(wtab, uu)


def _dft_tables(l, d_f):
    f = math.gcd(l, 128)

    def thin(n, stride):
        p = lax.broadcasted_iota(jnp.int32, (l, n), 0)
        q = lax.broadcasted_iota(jnp.int32, (l, n), 1) * stride
        ang = ((p * q) % l).astype(F32) * (2.0 * math.pi / l)
        return jnp.cos(ang), jnp.sin(ang)

    c1, s1 = (t[:, :, None] for t in thin(l // f, f))
    c2, s2 = (t[:, None, :] for t in thin(f, 1))
    wtab = jnp.concatenate([(c1 * c2 - s1 * s2).reshape(l, l), (-(s1 * c2 + c1 * s2)).reshape(l, l)],
                           axis=1).astype(BF16)
    cc = lax.broadcasted_iota(jnp.int32, (d_f, d_f), 0)
    qq = lax.broadcasted_iota(jnp.int32, (d_f, d_f), 1)
    a2 = (((cc % HEAD) * (qq % HEAD)) % HEAD).astype(F32) * (2.0 * math.pi / HEAD)
    same = (cc // HEAD) == (qq // HEAD)
    scale = 1.0 / math.sqrt(l * HEAD)
    ctab = jnp.stack([jnp.where(same, jnp.cos(a2), 0.0), jnp.where(same, jnp.sin(a2), 0.0)]) * scale
    return wtab, ctab.astype(BF16)


def _mix_out_kernel(period, yf_ref, yb_ref, z_ref, gate_ref, cg_ref, cx_ref, cb_ref, ft_ref, x_ref,
                    ga_ref, lnw_ref, lnb_ref, cw_ref, w_ref, o_ref, mix_ref):
    d_rwkv = yf_ref.shape[1]
    d_conv = cg_ref.shape[1]
    tm = yf_ref.shape[0]
    r = lax.broadcasted_iota(jnp.int32, (GW, GW), 0)
    c = lax.broadcasted_iota(jnp.int32, (GW, GW), 1)
    ones_bd = jnp.where((r // HEAD) == (c // HEAD), 1.0, 0.0).astype(BF16)
    inv_n = 1.0 / HEAD
    for s in range(d_rwkv // GW):
        sl = slice(s * GW, (s + 1) * GW)
        y = yf_ref[:, sl] + yb_ref[:, sl]
        mu = _dot_lhs_f32(y, ones_bd) * inv_n
        dlt = y - mu
        var = _dot_lhs_f32(dlt * dlt, ones_bd) * inv_n
        yn = dlt * lax.rsqrt(var + GN_EPS) * lnw_ref[:, sl] + lnb_ref[:, sl] + z_ref[:, sl]
        mix_ref[:, sl] = (yn * gate_ref[:, sl]).astype(BF16)
    u = cg_ref[...] * cx_ref[...]
    rowid = lax.broadcasted_iota(jnp.int32, u.shape, 0) % period
    up = jnp.where(rowid == 0, 0.0, pltpu.roll(u, 1, 0))
    dn = jnp.where(rowid == period - 1, 0.0, pltpu.roll(u, tm - 1, 0))
    cw = cw_ref[...]
    conv = cb_ref[...] * (up * cw[0:1] + u * cw[1:2] + dn * cw[2:3])
    mix_ref[:, d_rwkv:d_rwkv + d_conv] = conv.astype(BF16)
    mix_ref[:, d_rwkv + d_conv:] = ft_ref[...].astype(BF16)
    o_ref[...] = x_ref[...] + ga_ref[...] * jnp.dot(mix_ref[...], w_ref[...], preferred_element_type=F32)


def _mix_out(yf, yb, z, gate, px, ft, x, ga, lw, period, conv_blk):
    b, l, d = x.shape
    d_rwkv = yf.shape[2]
    d_conv = lw["conv_w"].shape[1]
    d_f = ft.shape[1] // b
    tm = min(256, l)
    assert tm % period == 0
    yspec = pl.BlockSpec((None, tm, d_rwkv), lambda bi, i: (bi, i, 0))
    cspec = lambda off: pl.BlockSpec((None, tm, d_conv), lambda bi, i: (bi, i, conv_blk + off))
    return pl.pallas_call(
        functools.partial(_mix_out_kernel, period),
        grid=(b, l // tm),
        in_specs=[yspec, yspec, yspec, yspec, cspec(0), cspec(1), cspec(2),
                  pl.BlockSpec((tm, d_f), lambda bi, i: (i, bi)),
                  pl.BlockSpec((None, tm, d), lambda bi, i: (bi, i, 0)),
                  pl.BlockSpec((None, 1, d), lambda bi, i: (bi, 0, 0)),
                  pl.BlockSpec((1, d_rwkv), lambda bi, i: (0, 0)),
                  pl.BlockSpec((1, d_rwkv), lambda bi, i: (0, 0)),
                  pl.BlockSpec((3, d_conv), lambda bi, i: (0, 0)),
                  pl.BlockSpec(lw["w_out"].shape, lambda bi, i: (0, 0))],
        out_specs=pl.BlockSpec((None, tm, d), lambda bi, i: (bi, i, 0)),
        out_shape=jax.ShapeDtypeStruct((b, l, d), F32),
        scratch_shapes=[pltpu.VMEM((tm, lw["w_out"].shape[0]), BF16)],
        compiler_params=_cparams(("parallel", "parallel")),
        name="mix_out",
    )(yf, yb, z, gate, px, px, px, ft, x, ga, lw["ln_w"], lw["ln_b"], lw["conv_w"], lw["w_out"])


def _ffn_kernel(final, x_ref, g_ref, sh_ref, sc_ref, ga_ref, wg_ref, wu_ref, wd_ref, gf_ref, o_ref,
                h_ref, acc_ref):
    j = pl.program_id(2)

    @pl.when(j == 0)
    def _():
        h_ref[...] = _rms_mod(x_ref[...], g_ref[...], sh_ref[...], sc_ref[...]).astype(BF16)
        acc_ref[...] = jnp.zeros_like(acc_ref)

    h = h_ref[...]
    a = jnp.dot(h, wg_ref[...], preferred_element_type=F32)
    u = jnp.dot(h, wu_ref[...], preferred_element_type=F32)
    t = (a * jax.nn.sigmoid(a)) * u
    acc_ref[...] += jnp.dot(t.astype(BF16), wd_ref[...], preferred_element_type=F32)

    @pl.when(j == pl.num_programs(2) - 1)
    def _():
        xn = x_ref[...] + ga_ref[...] * acc_ref[...]
        if final:
            ms = jnp.mean(xn * xn, axis=-1, keepdims=True)
            xn = xn * lax.rsqrt(ms + RMS_EPS) * gf_ref[...]
        o_ref[...] = xn


def _ffn(x, g, shift, scale, ga, wg, wu, wd, g_final, final):
    b, l, d = x.shape
    ff = wg.shape[1]
    tm = min(512, l)
    tf = 512
    vec = pl.BlockSpec((None, 1, d), lambda bi, i, j: (bi, 0, 0))
    gspec = pl.BlockSpec((1, d), lambda bi, i, j: (0, 0))
    return pl.pallas_call(
        functools.partial(_ffn_kernel, final),
        grid=(b, l // tm, ff // tf),
        in_specs=[pl.BlockSpec((None, tm, d), lambda bi, i, j: (bi, i, 0)),
                  gspec, vec, vec, vec,
                  pl.BlockSpec((d, tf), lambda bi, i, j: (0, j)),
                  pl.BlockSpec((d, tf), lambda bi, i, j: (0, j)),
                  pl.BlockSpec((tf, d), lambda bi, i, j: (j, 0)),
                  gspec],
        out_specs=pl.BlockSpec((None, tm, d), lambda bi, i, j: (bi, i, 0)),
        out_shape=jax.ShapeDtypeStruct((b, l, d), F32),
        scratch_shapes=[pltpu.VMEM((tm, d), BF16), pltpu.VMEM((tm, d), F32)],
        compiler_params=_cparams(("parallel", "parallel", "arbitrary")),
        name="ffn",
    )(x, g, shift, scale, ga, wg, wu, wd, g_final)


def _layer_weights(i, w_in, rw_shift, dec_w0, dec_up, iclr_a0, iclr_up, k_k, k_a, r_k, ln_w, ln_b, g_up,
                   conv_w, w_out, w_gate, w_up, w_down, dims):
    d_rwkv, d_conv, d_f, dl, il, gl = dims
    ng = d_rwkv // GW
    wi = w_in[i]
    sh = rw_shift[i]
    r0, k0, v0 = 0, d_rwkv, 2 * d_rwkv
    wd0 = 3 * d_rwkv
    ad0 = wd0 + dl
    gd0 = ad0 + il
    rest0 = gd0 + gl

    def regroup(m):
        cols = []
        for g in range(ng):
            for base in (r0, k0, v0):
                cols.append(m[:, base + g * GW: base + (g + 1) * GW])
        padl = jnp.zeros((m.shape[0], LORA_PAD - dl), m.dtype)
        padi = jnp.zeros((m.shape[0], LORA_PAD - il), m.dtype)
        cols += [m[:, wd0:ad0], padl, m[:, ad0:gd0], padi, m[:, gd0:rest0]]
        return jnp.concatenate(cols, axis=1)

    w_rw = regroup(wi)
    sh_all = regroup(sh)
    pad_rows = lambda m, n: jnp.concatenate([m, jnp.zeros(m.shape[:-2] + (n - m.shape[-2], m.shape[-1]), m.dtype)], -2)
    zero = jnp.zeros((d_rwkv,), F32)
    pvec = jnp.stack([k_k[i], k_a[i], r_k[i].reshape(-1), dec_w0[i, 0], dec_w0[i, 1],
                      iclr_a0[i, 0], iclr_a0[i, 1], zero])
    return dict(
        ng=ng,
        w_in=jnp.concatenate([w_rw, wi[:, rest0:]], axis=1).astype(BF16),
        sh_rkv=sh_all[:, :3 * d_rwkv], sh_lo=sh_all[:, 3 * d_rwkv:],
        pvec=pvec,
        dec_up=pad_rows(dec_up[i], LORA_PAD).astype(BF16),
        iclr_up=pad_rows(iclr_up[i], LORA_PAD).astype(BF16),
        g_up=g_up[i].astype(BF16),
        ln_w=ln_w[i][None, :], ln_b=ln_b[i][None, :],
        conv_w=conv_w[i],
        w_out=w_out[i].astype(BF16),
        w_gate=w_gate[i].astype(BF16), w_up=w_up[i].astype(BF16), w_down=w_down[i].astype(BF16),
    )


def kernel(x, c, ctx, c_ctx, w_mod, b_mod, norm_mix, w_in, rw_shift, dec_w0, dec_up, iclr_a0, iclr_up,
           k_k, k_a, r_k, ln_w, ln_b, g_up, conv_w, w_out, norm_ffn, w_gate, w_up, w_down, norm_final):
    b, l, d = x.shape
    lc = ctx.shape[1]
    depth = w_mod.shape[0]
    d_rwkv = k_k.shape[1]
    d_conv = conv_w.shape[2]
    dl, il, gl = dec_up.shape[2], iclr_up.shape[2], g_up.shape[1]
    d_f = w_out.shape[1] - d_rwkv - d_conv
    assert dl <= LORA_PAD and il <= LORA_PAD and gl == 2 * LORA_PAD and d_rwkv % GW == 0
    assert l % GRID_W == 0 and l % (WKV_SUB * CHUNK) == 0 and lc % (WKV_SUB * CHUNK) == 0
    ng = d_rwkv // GW
    n_rw = 3 * d_rwkv + 4 * LORA_PAD
    conv_blk = n_rw // d_conv
    ft_blk = (n_rw + 3 * d_conv) // d_f
    assert conv_blk * d_conv == n_rw and ft_blk * d_f == n_rw + 3 * d_conv

    cvec = jnp.concatenate([c, c_ctx[None, :], jnp.zeros((8 - b - 1, d), F32)], axis=0)
    mods = _adaln(cvec, w_mod, b_mod[:, None, :])

    def mod(i, j, ctx_rows):
        m = mods[i, :, j * d:(j + 1) * d]
        if ctx_rows:
            return jnp.broadcast_to(m[b:b + 1], (b, d))[:, None, :]
        return m[:b][:, None, :]

    wtab_x, ctab_x = _dft_tables(l, d_f)
    wtab_c, ctab_c = _dft_tables(lc, d_f)
    dims = (d_rwkv, d_conv, d_f, dl, il, gl)
    s_zero = jnp.zeros((b, 2, ng, GW, GW), F32)
    gfin = norm_final[None, :]
    xc = ctx
    for i in range(depth):
        lw = _layer_weights(i, w_in, rw_shift, dec_w0, dec_up, iclr_a0, iclr_up, k_k, k_a, r_k, ln_w, ln_b,
                            g_up, conv_w, w_out, w_gate, w_up, w_down, dims)
        gmix = norm_mix[i][None, :]
        gffn = norm_ffn[i][None, :]
        last = i == depth - 1
        pc = _in_proj(xc, gmix, mod(i, 0, True), mod(i, 1, True), lw["w_in"])
        px = _in_proj(x, gmix, mod(i, 0, False), mod(i, 1, False), lw["w_in"])
        yfc, ybc, zc, gc, s_ctx = _wkv(pc, lw, s_zero, WKV_SUB)
        yfx, ybx, zx, gx, _ = _wkv(px, lw, s_ctx, WKV_SUB)
        ftx = _dft_seq(wtab_x, _dft_channels(px, ctab_x, ft_blk, d_f))
        x = _mix_out(yfx, ybx, zx, gx, px, ftx, x, mod(i, 2, False), lw, GRID_W, conv_blk)
        x = _ffn(x, gffn, mod(i, 3, False), mod(i, 4, False), mod(i, 5, False),
                 lw["w_gate"], lw["w_up"], lw["w_down"], gfin, last)
        if not last:
            ftc = _dft_seq(wtab_c, _dft_channels(pc, ctab_c, ft_blk, d_f))
            xc = _mix_out(yfc, ybc, zc, gc, pc, ftc, xc, mod(i, 2, True), lw, lc, conv_blk)
            xc = _ffn(xc, gffn, mod(i, 3, True), mod(i, 4, True), mod(i, 5, True),
                      lw["w_gate"], lw["w_up"], lw["w_down"], gfin, False)
    return x
```

```python
import functools
import math

import jax
import jax.numpy as jnp
from jax import lax
from jax.experimental import pallas as pl
from jax.experimental.pallas import tpu as pltpu

F32 = jnp.float32
BF16 = jnp.bfloat16

HEAD = 64
GROUP_HEADS = 4
GW = GROUP_HEADS * HEAD
PAIR_HEADS = 2
PW = PAIR_HEADS * HEAD
N_PAIR = GW // PW
CHUNK = 64
WKV_SUB = 4
HALO = 8
LORA_PAD = 128
GRID_W = 64
N_MOD = 6
RMS_EPS = 1e-6
GN_EPS = 64e-5
KK_EPS = 1e-12
VMEM_LIMIT = 56 * 1024 * 1024


def _cparams(sem):
    return pltpu.CompilerParams(dimension_semantics=sem, vmem_limit_bytes=VMEM_LIMIT)


def _dot(a, b, nt=False):
    dn = (((1,), (1,)), ((), ())) if nt else (((1,), (0,)), ((), ()))
    return lax.dot_general(a.astype(BF16), b.astype(BF16), dn, preferred_element_type=F32)


def _split3(a):
    hi = a.astype(BF16)
    r1 = a - hi.astype(F32)
    mid = r1.astype(BF16)
    lo = (r1 - mid.astype(F32)).astype(BF16)
    return hi, mid, lo


def _dot_lhs_f32(a, b_exact):
    hi, mid, lo = _split3(a)
    return _dot(hi, b_exact) + (_dot(mid, b_exact) + _dot(lo, b_exact))


def _dot_rhs_f32(a_exact, b):
    hi, mid, lo = _split3(b)
    return _dot(a_exact, hi) + (_dot(a_exact, mid) + _dot(a_exact, lo))


def _rms_mod(x, g, shift, scale):
    ms = jnp.mean(x * x, axis=-1, keepdims=True)
    return (x * lax.rsqrt(ms + RMS_EPS) * g) * (1.0 + scale) + shift


def _adaln_kernel(c_ref, w_ref, b_ref, o_ref):
    c = c_ref[...]
    s = c * jax.nn.sigmoid(c)
    o_ref[...] = _dot(s, w_ref[...]) + b_ref[...]


def _adaln(cvec, w_mod, b_mod):
    nl, d, n = w_mod.shape
    tn = 1024
    return pl.pallas_call(
        _adaln_kernel,
        grid=(nl, n // tn),
        in_specs=[pl.BlockSpec((8, d), lambda l, j: (0, 0)),
                  pl.BlockSpec((None, d, tn), lambda l, j: (l, 0, j)),
                  pl.BlockSpec((None, 1, tn), lambda l, j: (l, 0, j))],
        out_specs=pl.BlockSpec((None, 8, tn), lambda l, j: (l, 0, j)),
        out_shape=jax.ShapeDtypeStruct((nl, 8, n), F32),
        compiler_params=_cparams(("parallel", "parallel")),
        name="adaln",
    )(cvec, w_mod, b_mod)


def _in_proj_kernel(x_ref, g_ref, sh_ref, sc_ref, w_ref, o_ref, xn_ref):
    @pl.when(pl.program_id(2) == 0)
    def _():
        xn_ref[...] = _rms_mod(x_ref[...], g_ref[...], sh_ref[...], sc_ref[...]).astype(BF16)

    o_ref[...] = jnp.dot(xn_ref[...], w_ref[...], preferred_element_type=F32)


def _in_proj(x, g, shift, scale, w):
    b, l, d = x.shape
    n = w.shape[1]
    tm = min(1024, l)
    tn = 512
    vec = pl.BlockSpec((None, 1, d), lambda bi, i, j: (bi, 0, 0))
    return pl.pallas_call(
        _in_proj_kernel,
        grid=(b, l // tm, n // tn),
        in_specs=[pl.BlockSpec((None, tm, d), lambda bi, i, j: (bi, i, 0)),
                  pl.BlockSpec((1, d), lambda bi, i, j: (0, 0)),
                  vec, vec,
                  pl.BlockSpec((d, tn), lambda bi, i, j: (0, j))],
        out_specs=pl.BlockSpec((None, tm, tn), lambda bi, i, j: (bi, i, j)),
        out_shape=jax.ShapeDtypeStruct((b, l, n), F32),
        scratch_shapes=[pltpu.VMEM((tm, d), BF16)],
        compiler_params=_cparams(("parallel", "parallel", "arbitrary")),
        name="in_proj",
    )(x, g, shift, scale, w)


def _token_shift(main, prev_blk, next_blk, sh, has_prev, has_next):
    t = main.shape[0]
    rowid = lax.broadcasted_iota(jnp.int32, main.shape, 0)
    prev_row = jnp.where(has_prev, prev_blk[HALO - 1:HALO, :], 0.0)
    next_row = jnp.where(has_next, next_blk[0:1, :], 0.0)
    up = jnp.where(rowid == 0, prev_row, pltpu.roll(main, 1, 0))
    dn = jnp.where(rowid == t - 1, next_row, pltpu.roll(main, t - 1, 0))
    return up * sh[0:1, :] + main * sh[1:2, :] + dn * sh[2:3, :]


def _block_masks(width):
    r = lax.broadcasted_iota(jnp.int32, (width, width), 0)
    c = lax.broadcasted_iota(jnp.int32, (width, width), 1)
    return (r // HEAD) == (c // HEAD), r % HEAD, c % HEAD, r == c


def _rs(x, bd):
    return jnp.where(bd, jnp.concatenate([x] * PAIR_HEADS, axis=0), 0.0)


def _ls(x_rs):
    out = x_rs[0:CHUNK]
    for h in range(1, PAIR_HEADS):
        out = out + x_rs[h * CHUNK:(h + 1) * CHUNK]
    return out


def _each(f, *lists):
    return [f(*a) for a in zip(*lists)]


def _chunk_affine(chunks, masks):
    bd, tt, ss, eye = masks
    row = lax.broadcasted_iota(jnp.int32, (CHUNK, CHUNK), 0)
    col = lax.broadcasted_iota(jnp.int32, (CHUNK, CHUNK), 1)
    tri = {rev: jnp.where((row <= col) if rev else (row >= col), 1.0, 0.0).astype(BF16) for rev in (False, True)}
    strict = {False: ss < tt, True: ss > tt}
    t2 = lax.broadcasted_iota(jnp.int32, (PW, 2 * PW), 0) % HEAD
    s2 = lax.broadcasted_iota(jnp.int32, (PW, 2 * PW), 1) % HEAD
    incl2 = {False: s2 <= t2, True: s2 >= t2}

    crev = [ch[6] for ch in chunks]
    logw = [ch[5] for ch in chunks]
    cum = _each(lambda rv, lw: _dot_rhs_f32(tri[rv], lw), crev, logw)
    total = _each(lambda rv, cm: cm[0:1] if rv else cm[CHUNK - 1:CHUNK], crev, cum)
    p_inv = _each(lambda cm: jnp.exp(-cm), cum)
    p_end = _each(lambda tot, cm: jnp.exp(tot - cm), total, cum)
    bvec = [ch[3] * ch[4] for ch in chunks]
    wide = dict(
        a=_each(lambda ch, cm, lw: -ch[3] * jnp.exp(cm - lw), chunks, cum, logw),
        r=_each(lambda ch, cm: ch[0] * jnp.exp(cm), chunks, cum),
        b=_each(lambda b, p: b * p, bvec, p_inv),
        k=_each(lambda ch, p: ch[1] * p, chunks, p_inv),
        v=[ch[2] for ch in chunks],
        be=_each(lambda b, p: b * p, bvec, p_end),
        ke=_each(lambda ch, p: ch[1] * p, chunks, p_end),
        pc=_each(jnp.exp, total),
    )

    def pairs(name):
        return [x[:, p * PW:(p + 1) * PW] for x in wide[name] for p in range(N_PAIR)]

    rev = [rv for rv in crev for _ in range(N_PAIR)]
    r_t = pairs("r")
    a_rs = _each(lambda x: _rs(x, bd).astype(BF16), pairs("a"))
    r_rs = _each(lambda x: _rs(x, bd).astype(BF16), r_t)
    b_rs = _each(lambda x: _rs(x, bd).astype(BF16), pairs("b"))
    k_rs = _each(lambda x: _rs(x, bd).astype(BF16), pairs("k"))
    v_rs = _each(lambda x: _rs(x, bd).astype(BF16), pairs("v"))
    aa = _each(lambda a, r, b, k: _dot(jnp.concatenate([a, r], axis=0), jnp.concatenate([b, k], axis=0), nt=True),
               a_rs, r_rs, b_rs, k_rs)
    a_ab = _each(lambda rv, x: jnp.where(strict[rv], x[0:PW, 0:PW], 0.0), rev, aa)
    a_ak = _each(lambda rv, x: jnp.where(strict[rv], x[0:PW, PW:], 0.0), rev, aa)
    a_r = _each(lambda rv, x: jnp.where(incl2[rv], x[PW:, :], 0.0).astype(BF16), rev, aa)
    t_inv = _each(lambda a: jnp.where(eye, 1.0, a), a_ab)
    pw = _each(lambda a: _dot(a, a), a_ab)
    for _ in range(int(math.log2(CHUNK)) - 2):
        res = _each(lambda p, t: _dot(p, jnp.concatenate([p, t], axis=1)), pw, t_inv)
        pw = [x[:, 0:PW] for x in res]
        t_inv = _each(lambda t, x: t + x[:, PW:], t_inv, res)
    t_inv = _each(lambda p, t: (t + _dot(p, t)).astype(BF16), pw, t_inv)
    akv = _each(_dot, a_ak, v_rs)
    tw = _each(lambda t, a, x: _dot(t, jnp.concatenate([a, x.astype(BF16)], axis=1)).astype(BF16),
               t_inv, a_rs, akv)
    bk_t = _each(lambda b, k: jnp.concatenate([_rs(b, bd), _rs(k, bd)], axis=0).T.astype(BF16),
                 pairs("be"), pairs("ke"))
    fin = _each(lambda ar, bk, w, v: _dot(jnp.concatenate([ar, bk], axis=0),
                                          jnp.concatenate([w, jnp.concatenate([jnp.zeros_like(v), v], axis=1)],
                                                          axis=0)),
                a_r, bk_t, tw, v_rs)
    rh = _each(lambda r, f: r + _ls(f[0:PW, 0:PW]), r_t, fin)
    y0 = _each(lambda f: _ls(f[0:PW, PW:]), fin)
    m = _each(lambda pc, f: jnp.where(eye, pc, 0.0) + f[PW:, 0:PW], pairs("pc"), fin)
    nn = [f[PW:, PW:] for f in fin]
    out = list(zip(y0, rh, m, nn))
    return [out[i * N_PAIR:(i + 1) * N_PAIR] for i in range(len(chunks))]


def _head_sum(x, ones_bd):
    return _dot_lhs_f32(x, ones_bd)


def _wkv_kernel(nsub,
                rkv_f, lo_f, rkv_fp, rkv_fn, lo_fp, lo_fn,
                rkv_b, lo_b, rkv_bp, rkv_bn, lo_bp, lo_bn,
                sh_rkv, sh_lo, pvec, dec_up, iclr_up, g_up, s0,
                yf_ref, yb_ref, z_ref, gate_ref, sfin_ref,
                zst):
    c = pl.program_id(2)
    nsteps = pl.num_programs(2)

    @pl.when(c == 0)
    def _():
        zst[...] = s0[...]

    masks = _block_masks(PW)
    ones_bd = jnp.where(_block_masks(GW)[0], 1.0, 0.0).astype(BF16)
    pv = pvec[...]
    k_k, k_a, r_k = pv[0:1], pv[1:2], pv[2:3]
    w0 = (pv[3:4], pv[4:5])
    a0 = (pv[5:6], pv[6:7])
    srkv = sh_rkv[...]
    slo = sh_lo[...]
    sig_scale = math.exp(-0.5)

    def streams(rkv_ref, lo_ref, rkv_p, rkv_n, lo_p, lo_n, has_prev, has_next, d):
        rkv = _token_shift(rkv_ref[...], rkv_p[...], rkv_n[...], srkv, has_prev, has_next)
        lo = _token_shift(lo_ref[...], lo_p[...], lo_n[...], slo, has_prev, has_next)
        r, k, v = rkv[:, 0:GW], rkv[:, GW:2 * GW], rkv[:, 2 * GW:3 * GW]
        wd = jnp.tanh(lo[:, 0:LORA_PAD])
        ad = lo[:, LORA_PAD:2 * LORA_PAD]
        gd = lo[:, 2 * LORA_PAD:]
        kk = k * k_k
        kkn = kk / jnp.maximum(jnp.sqrt(_head_sum(kk * kk, ones_bd)), KK_EPS)
        logw = -sig_scale * jax.nn.sigmoid(w0[d] + _dot(wd, dec_up[d]))
        aic = jax.nn.sigmoid(a0[d] + _dot(ad, iclr_up[d]))
        kd = k * (1.0 + (aic - 1.0) * k_a)
        return r, k, v, kkn, logw, aic, kd, ad, gd

    cb = nsteps - 1 - c
    r, k, v, kkn, logw, aic, kd, ad, gd = streams(
        rkv_f, lo_f, rkv_fp, rkv_fn, lo_fp, lo_fn, c > 0, c < nsteps - 1, 0)
    aic_o = jax.nn.sigmoid(a0[1] + _dot(ad, iclr_up[1]))
    kd_o = k * (1.0 + (aic_o - 1.0) * k_a)
    z_ref[...] = _head_sum(r * (kd + kd_o) * r_k, ones_bd) * v
    gate_ref[...] = _dot(jax.nn.sigmoid(gd), g_up[...])
    chains = [tuple(t[j * CHUNK:(j + 1) * CHUNK] for t in (r, kd, v, kkn, aic, logw)) + (False,)
              for j in range(nsub)]
    r, k, v, kkn, logw, aic, kd, ad, gd = streams(
        rkv_b, lo_b, rkv_bp, rkv_bn, lo_bp, lo_bn, cb > 0, cb < nsteps - 1, 1)
    chains += [tuple(t[j * CHUNK:(j + 1) * CHUNK] for t in (r, kd, v, kkn, aic, logw)) + (True,)
               for j in range(nsub)]
    affine = _chunk_affine(chains, masks)
    fwd, bwd = affine[:nsub], affine[nsub:]

    for d, (steps, y_ref) in enumerate(((list(range(nsub)), yf_ref), (list(reversed(range(nsub))), yb_ref))):
        affine_d = fwd if d == 0 else bwd
        z = [zst[d, p] for p in range(N_PAIR)]
        for j in steps:
            mz = [_dot(jnp.concatenate([affine_d[j][p][2], affine_d[j][p][1]], axis=0), z[p])
                  for p in range(N_PAIR)]
            y_ref[j * CHUNK:(j + 1) * CHUNK, :] = jnp.concatenate(
                [affine_d[j][p][0] + mz[p][PW:] for p in range(N_PAIR)], axis=1)
            z = [mz[p][0:PW] + affine_d[j][p][3] for p in range(N_PAIR)]
        for p in range(N_PAIR):
            zst[d, p] = z[p]

    @pl.when(c == nsteps - 1)
    def _():
        sfin_ref[...] = zst[...]


def _wkv(px, lw, s0, nsub):
    b, l, _ = px.shape
    ng = lw["ng"]
    d_rwkv = ng * GW
    t = nsub * CHUNK
    nsteps = l // t
    hb = t // HALO
    nhb = l // HALO
    lo_blk = 3 * d_rwkv // (4 * LORA_PAD)
    rkv_w = 3 * GW
    lo_w = 4 * LORA_PAD

    def main_specs(blk):
        return [pl.BlockSpec((None, t, rkv_w), lambda bi, g, c: (bi, blk(c), g)),
                pl.BlockSpec((None, t, lo_w), lambda bi, g, c: (bi, blk(c), lo_blk))]

    def halo_specs(blk):
        prev = lambda c: jnp.maximum(blk(c) * hb - 1, 0)
        nxt = lambda c: jnp.minimum((blk(c) + 1) * hb, nhb - 1)
        return [pl.BlockSpec((None, HALO, rkv_w), lambda bi, g, c: (bi, prev(c), g)),
                pl.BlockSpec((None, HALO, rkv_w), lambda bi, g, c: (bi, nxt(c), g)),
                pl.BlockSpec((None, HALO, lo_w), lambda bi, g, c: (bi, prev(c), lo_blk)),
                pl.BlockSpec((None, HALO, lo_w), lambda bi, g, c: (bi, nxt(c), lo_blk))]

    fblk = lambda c: c
    bblk = lambda c: nsteps - 1 - c
    in_specs = (main_specs(fblk) + halo_specs(fblk) + main_specs(bblk) + halo_specs(bblk) + [
        pl.BlockSpec((3, rkv_w), lambda bi, g, c: (0, g)),
        pl.BlockSpec((3, lo_w), lambda bi, g, c: (0, 0)),
        pl.BlockSpec((8, GW), lambda bi, g, c: (0, g)),
        pl.BlockSpec((2, LORA_PAD, GW), lambda bi, g, c: (0, 0, g)),
        pl.BlockSpec((2, LORA_PAD, GW), lambda bi, g, c: (0, 0, g)),
        pl.BlockSpec((2 * LORA_PAD, GW), lambda bi, g, c: (0, g)),
        pl.BlockSpec((None, 2, N_PAIR, PW, PW), lambda bi, g, c: (bi, 0, g, 0, 0)),
    ])
    yspec = lambda blk: pl.BlockSpec((None, t, GW), lambda bi, g, c: (bi, blk(c), g))
    out_specs = [yspec(fblk), yspec(bblk), yspec(fblk), yspec(fblk),
                 pl.BlockSpec((None, 2, N_PAIR, PW, PW), lambda bi, g, c: (bi, 0, g, 0, 0))]
    ysh = jax.ShapeDtypeStruct((b, l, d_rwkv), F32)
    return pl.pallas_call(
        functools.partial(_wkv_kernel, nsub),
        grid=(b, ng, nsteps),
        in_specs=in_specs,
        out_specs=out_specs,
        out_shape=[ysh, ysh, ysh, ysh, jax.ShapeDtypeStruct(s0.shape, F32)],
        scratch_shapes=[pltpu.VMEM((2, N_PAIR, PW, PW), F32)],
        compiler_params=_cparams(("parallel", "parallel", "arbitrary")),
        name="wkv",
    )(px, px, px, px, px, px, px, px, px, px, px, px,
      lw["sh_rkv"], lw["sh_lo"], lw["pvec"], lw["dec_up"], lw["iclr_up"], lw["g_up"], s0)


def _dft_ch_kernel(u_ref, t_ref, o_ref):
    u = u_ref[...]
    hi = u.astype(BF16)
    lo = (u - hi.astype(F32)).astype(BF16)
    tab = t_ref[...]
    o_ref[...] = (_dot(hi, tab) + _dot(lo, tab)).astype(BF16)


def _dft_channels(px, tabs, ft_blk, d_f):
    b, l, _ = px.shape
    tm = min(512, l)
    nt = l // tm
    return pl.pallas_call(
        _dft_ch_kernel,
        grid=(b, 2, nt),
        in_specs=[pl.BlockSpec((None, tm, d_f), lambda bi, s, i: (bi, i, ft_blk)),
                  pl.BlockSpec((None, d_f, d_f), lambda bi, s, i: (s, 0, 0))],
        out_specs=pl.BlockSpec((tm, d_f), lambda bi, s, i: (s * nt + i, bi)),
        out_shape=jax.ShapeDtypeStruct((2 * l, b * d_f), BF16),
        compiler_params=_cparams(("parallel", "parallel", "parallel")),
        name="dft_channels",
    )(px, tabs)


def _dft_seq_kernel(w_ref, u_ref, o_ref, acc_ref):
    kk = pl.program_id(1)

    @pl.when(kk == 0)
    def _():
        acc_ref[...] = jnp.zeros_like(acc_ref)

    acc_ref[...] += jnp.dot(w_ref[...], u_ref[...], preferred_element_type=F32)

    @pl.when(kk == pl.num_programs(1) - 1)
    def _():
        o_ref[...] = acc_ref[...]


def _dft_seq(wtab, uu):
    l, k2 = wtab.shape
    n = uu.shape[1]
    tm = min(1024, l)
    tk = min(2048, k2)
    return pl.pallas_call(
        _dft_seq_kernel,
        grid=(l // tm, k2 // tk),
        in_specs=[pl.BlockSpec((tm, tk), lambda i, kk: (i, kk)),
                  pl.BlockSpec((tk, n), lambda i, kk: (kk, 0))],
        out_specs=pl.BlockSpec((tm, n), lambda i, kk: (i, 0)),
        out_shape=jax.ShapeDtypeStruct((l, n), F32),
        scratch_shapes=[pltpu.VMEM((tm, n), F32)],
        compiler_params=_cparams(("parallel", "arbitrary")),
        name="dft_seq",
    )(wtab, uu)


def _dft_tables(l, d_f):
    f = math.gcd(l, 128)

    def thin(n, stride):
        p = lax.broadcasted_iota(jnp.int32, (l, n), 0)
        q = lax.broadcasted_iota(jnp.int32, (l, n), 1) * stride
        ang = ((p * q) % l).astype(F32) * (2.0 * math.pi / l)
        return jnp.cos(ang), jnp.sin(ang)

    c1, s1 = (t[:, :, None] for t in thin(l // f, f))
    c2, s2 = (t[:, None, :] for t in thin(f, 1))
    wtab = jnp.concatenate([(c1 * c2 - s1 * s2).reshape(l, l), (-(s1 * c2 + c1 * s2)).reshape(l, l)],
                           axis=1).astype(BF16)
    cc = lax.broadcasted_iota(jnp.int32, (d_f, d_f), 0)
    qq = lax.broadcasted_iota(jnp.int32, (d_f, d_f), 1)
    a2 = (((cc % HEAD) * (qq % HEAD)) % HEAD).astype(F32) * (2.0 * math.pi / HEAD)
    same = (cc // HEAD) == (qq // HEAD)
    scale = 1.0 / math.sqrt(l * HEAD)
    ctab = jnp.stack([jnp.where(same, jnp.cos(a2), 0.0), jnp.where(same, jnp.sin(a2), 0.0)]) * scale
    return wtab, ctab.astype(BF16)


def _mix_out_kernel(period, yf_ref, yb_ref, z_ref, gate_ref, cg_ref, cx_ref, cb_ref, ft_ref, x_ref,
                    ga_ref, lnw_ref, lnb_ref, cw_ref, w_ref, o_ref, mix_ref):
    d_rwkv = yf_ref.shape[1]
    d_conv = cg_ref.shape[1]
    tm = yf_ref.shape[0]
    r = lax.broadcasted_iota(jnp.int32, (GW, GW), 0)
    c = lax.broadcasted_iota(jnp.int32, (GW, GW), 1)
    ones_bd = jnp.where((r // HEAD) == (c // HEAD), 1.0, 0.0).astype(BF16)
    inv_n = 1.0 / HEAD
    for s in range(d_rwkv // GW):
        sl = slice(s * GW, (s + 1) * GW)
        y = yf_ref[:, sl] + yb_ref[:, sl]
        mu = _dot_lhs_f32(y, ones_bd) * inv_n
        dlt = y - mu
        var = _dot_lhs_f32(dlt * dlt, ones_bd) * inv_n
        yn = dlt * lax.rsqrt(var + GN_EPS) * lnw_ref[:, sl] + lnb_ref[:, sl] + z_ref[:, sl]
        mix_ref[:, sl] = (yn * gate_ref[:, sl]).astype(BF16)
    u = cg_ref[...] * cx_ref[...]
    rowid = lax.broadcasted_iota(jnp.int32, u.shape, 0) % period
    up = jnp.where(rowid == 0, 0.0, pltpu.roll(u, 1, 0))
    dn = jnp.where(rowid == period - 1, 0.0, pltpu.roll(u, tm - 1, 0))
    cw = cw_ref[...]
    conv = cb_ref[...] * (up * cw[0:1] + u * cw[1:2] + dn * cw[2:3])
    mix_ref[:, d_rwkv:d_rwkv + d_conv] = conv.astype(BF16)
    mix_ref[:, d_rwkv + d_conv:] = ft_ref[...].astype(BF16)
    o_ref[...] = x_ref[...] + ga_ref[...] * jnp.dot(mix_ref[...], w_ref[...], preferred_element_type=F32)


def _mix_out(yf, yb, z, gate, px, ft, x, ga, lw, period, conv_blk):
    b, l, d = x.shape
    d_rwkv = yf.shape[2]
    d_conv = lw["conv_w"].shape[1]
    d_f = ft.shape[1] // b
    tm = min(256, l)
    assert tm % period == 0
    yspec = pl.BlockSpec((None, tm, d_rwkv), lambda bi, i: (bi, i, 0))
    cspec = lambda off: pl.BlockSpec((None, tm, d_conv), lambda bi, i: (bi, i, conv_blk + off))
    return pl.pallas_call(
        functools.partial(_mix_out_kernel, period),
        grid=(b, l // tm),
        in_specs=[yspec, yspec, yspec, yspec, cspec(0), cspec(1), cspec(2),
                  pl.BlockSpec((tm, d_f), lambda bi, i: (i, bi)),
                  pl.BlockSpec((None, tm, d), lambda bi, i: (bi, i, 0)),
                  pl.BlockSpec((None, 1, d), lambda bi, i: (bi, 0, 0)),
                  pl.BlockSpec((1, d_rwkv), lambda bi, i: (0, 0)),
                  pl.BlockSpec((1, d_rwkv), lambda bi, i: (0, 0)),
                  pl.BlockSpec((3, d_conv), lambda bi, i: (0, 0)),
                  pl.BlockSpec(lw["w_out"].shape, lambda bi, i: (0, 0))],
        out_specs=pl.BlockSpec((None, tm, d), lambda bi, i: (bi, i, 0)),
        out_shape=jax.ShapeDtypeStruct((b, l, d), F32),
        scratch_shapes=[pltpu.VMEM((tm, lw["w_out"].shape[0]), BF16)],
        compiler_params=_cparams(("parallel", "parallel")),
        name="mix_out",
    )(yf, yb, z, gate, px, px, px, ft, x, ga, lw["ln_w"], lw["ln_b"], lw["conv_w"], lw["w_out"])


def _ffn_kernel(final, x_ref, g_ref, sh_ref, sc_ref, ga_ref, wg_ref, wu_ref, wd_ref, gf_ref, o_ref,
                h_ref, acc_ref):
    j = pl.program_id(2)

    @pl.when(j == 0)
    def _():
        h_ref[...] = _rms_mod(x_ref[...], g_ref[...], sh_ref[...], sc_ref[...]).astype(BF16)
        acc_ref[...] = jnp.zeros_like(acc_ref)

    h = h_ref[...]
    a = jnp.dot(h, wg_ref[...], preferred_element_type=F32)
    u = jnp.dot(h, wu_ref[...], preferred_element_type=F32)
    t = (a * jax.nn.sigmoid(a)) * u
    acc_ref[...] += jnp.dot(t.astype(BF16), wd_ref[...], preferred_element_type=F32)

    @pl.when(j == pl.num_programs(2) - 1)
    def _():
        xn = x_ref[...] + ga_ref[...] * acc_ref[...]
        if final:
            ms = jnp.mean(xn * xn, axis=-1, keepdims=True)
            xn = xn * lax.rsqrt(ms + RMS_EPS) * gf_ref[...]
        o_ref[...] = xn


def _ffn(x, g, shift, scale, ga, wg, wu, wd, g_final, final):
    b, l, d = x.shape
    ff = wg.shape[1]
    tm = min(512, l)
    tf = 512
    vec = pl.BlockSpec((None, 1, d), lambda bi, i, j: (bi, 0, 0))
    gspec = pl.BlockSpec((1, d), lambda bi, i, j: (0, 0))
    return pl.pallas_call(
        functools.partial(_ffn_kernel, final),
        grid=(b, l // tm, ff // tf),
        in_specs=[pl.BlockSpec((None, tm, d), lambda bi, i, j: (bi, i, 0)),
                  gspec, vec, vec, vec,
                  pl.BlockSpec((d, tf), lambda bi, i, j: (0, j)),
                  pl.BlockSpec((d, tf), lambda bi, i, j: (0, j)),
                  pl.BlockSpec((tf, d), lambda bi, i, j: (j, 0)),
                  gspec],
        out_specs=pl.BlockSpec((None, tm, d), lambda bi, i, j: (bi, i, 0)),
        out_shape=jax.ShapeDtypeStruct((b, l, d), F32),
        scratch_shapes=[pltpu.VMEM((tm, d), BF16), pltpu.VMEM((tm, d), F32)],
        compiler_params=_cparams(("parallel", "parallel", "arbitrary")),
        name="ffn",
    )(x, g, shift, scale, ga, wg, wu, wd, g_final)


def _layer_weights(i, w_in, rw_shift, dec_w0, dec_up, iclr_a0, iclr_up, k_k, k_a, r_k, ln_w, ln_b, g_up,
                   conv_w, w_out, w_gate, w_up, w_down, dims):
    d_rwkv, d_conv, d_f, dl, il, gl = dims
    ng = d_rwkv // GW
    wi = w_in[i]
    sh = rw_shift[i]
    r0, k0, v0 = 0, d_rwkv, 2 * d_rwkv
    wd0 = 3 * d_rwkv
    ad0 = wd0 + dl
    gd0 = ad0 + il
    rest0 = gd0 + gl

    def regroup(m):
        cols = []
        for g in range(ng):
            for base in (r0, k0, v0):
                cols.append(m[:, base + g * GW: base + (g + 1) * GW])
        padl = jnp.zeros((m.shape[0], LORA_PAD - dl), m.dtype)
        padi = jnp.zeros((m.shape[0], LORA_PAD - il), m.dtype)
        cols += [m[:, wd0:ad0], padl, m[:, ad0:gd0], padi, m[:, gd0:rest0]]
        return jnp.concatenate(cols, axis=1)

    w_rw = regroup(wi)
    sh_all = regroup(sh)
    pad_rows = lambda m, n: jnp.concatenate([m, jnp.zeros(m.shape[:-2] + (n - m.shape[-2], m.shape[-1]), m.dtype)], -2)
    zero = jnp.zeros((d_rwkv,), F32)
    pvec = jnp.stack([k_k[i], k_a[i], r_k[i].reshape(-1), dec_w0[i, 0], dec_w0[i, 1],
                      iclr_a0[i, 0], iclr_a0[i, 1], zero])
    return dict(
        ng=ng,
        w_in=jnp.concatenate([w_rw, wi[:, rest0:]], axis=1).astype(BF16),
        sh_rkv=sh_all[:, :3 * d_rwkv], sh_lo=sh_all[:, 3 * d_rwkv:],
        pvec=pvec,
        dec_up=pad_rows(dec_up[i], LORA_PAD).astype(BF16),
        iclr_up=pad_rows(iclr_up[i], LORA_PAD).astype(BF16),
        g_up=g_up[i].astype(BF16),
        ln_w=ln_w[i][None, :], ln_b=ln_b[i][None, :],
        conv_w=conv_w[i],
        w_out=w_out[i].astype(BF16),
        w_gate=w_gate[i].astype(BF16), w_up=w_up[i].astype(BF16), w_down=w_down[i].astype(BF16),
    )


def kernel(x, c, ctx, c_ctx, w_mod, b_mod, norm_mix, w_in, rw_shift, dec_w0, dec_up, iclr_a0, iclr_up,
           k_k, k_a, r_k, ln_w, ln_b, g_up, conv_w, w_out, norm_ffn, w_gate, w_up, w_down, norm_final):
    b, l, d = x.shape
    lc = ctx.shape[1]
    depth = w_mod.shape[0]
    d_rwkv = k_k.shape[1]
    d_conv = conv_w.shape[2]
    dl, il, gl = dec_up.shape[2], iclr_up.shape[2], g_up.shape[1]
    d_f = w_out.shape[1] - d_rwkv - d_conv
    assert dl <= LORA_PAD and il <= LORA_PAD and gl == 2 * LORA_PAD and d_rwkv % GW == 0
    assert l % GRID_W == 0 and l % (WKV_SUB * CHUNK) == 0 and lc % (WKV_SUB * CHUNK) == 0
    ng = d_rwkv // GW
    n_rw = 3 * d_rwkv + 4 * LORA_PAD
    conv_blk = n_rw // d_conv
    ft_blk = (n_rw + 3 * d_conv) // d_f
    assert conv_blk * d_conv == n_rw and ft_blk * d_f == n_rw + 3 * d_conv

    cvec = jnp.concatenate([c, c_ctx[None, :], jnp.zeros((8 - b - 1, d), F32)], axis=0)
    mods = _adaln(cvec, w_mod, b_mod[:, None, :])

    def mod(i, j, ctx_rows):
        m = mods[i, :, j * d:(j + 1) * d]
        if ctx_rows:
            return jnp.broadcast_to(m[b:b + 1], (b, d))[:, None, :]
        return m[:b][:, None, :]

    wtab_x, ctab_x = _dft_tables(l, d_f)
    wtab_c, ctab_c = _dft_tables(lc, d_f)
    dims = (d_rwkv, d_conv, d_f, dl, il, gl)
    s_zero = jnp.zeros((b, 2, ng * N_PAIR, PW, PW), F32)
    gfin = norm_final[None, :]
    xc = ctx
    for i in range(depth):
        lw = _layer_weights(i, w_in, rw_shift, dec_w0, dec_up, iclr_a0, iclr_up, k_k, k_a, r_k, ln_w, ln_b,
                            g_up, conv_w, w_out, w_gate, w_up, w_down, dims)
        gmix = norm_mix[i][None, :]
        gffn = norm_ffn[i][None, :]
        last = i == depth - 1
        pc = _in_proj(xc, gmix, mod(i, 0, True), mod(i, 1, True), lw["w_in"])
        px = _in_proj(x, gmix, mod(i, 0, False), mod(i, 1, False), lw["w_in"])
        yfc, ybc, zc, gc, s_ctx = _wkv(pc, lw, s_zero, WKV_SUB)
        yfx, ybx, zx, gx, _ = _wkv(px, lw, s_ctx, WKV_SUB)
        ftx = _dft_seq(wtab_x, _dft_channels(px, ctab_x, ft_blk, d_f))
        x = _mix_out(yfx, ybx, zx, gx, px, ftx, x, mod(i, 2, False), lw, GRID_W, conv_blk)
        x = _ffn(x, gffn, mod(i, 3, False), mod(i, 4, False), mod(i, 5, False),
                 lw["w_gate"], lw["w_up"], lw["w_down"], gfin, last)
        if not last:
            ftc = _dft_seq(wtab_c, _dft_channels(pc, ctab_c, ft_blk, d_f))
            xc = _mix_out(yfc, ybc, zc, gc, pc, ftc, xc, mod(i, 2, True), lw, lc, conv_blk)
            xc = _ffn(xc, gffn, mod(i, 3, True), mod(i, 4, True), mod(i, 5, True),
                      lw["w_gate"], lw["w_up"], lw["w_down"], gfin, False)
    return x
```

```python
import functools
import math

import jax
import jax.numpy as jnp
from jax import lax
from jax.experimental import pallas as pl
from jax.experimental.pallas import tpu as pltpu

F32 = jnp.float32
BF16 = jnp.bfloat16

HEAD = 64
GROUP_HEADS = 4
GW = GROUP_HEADS * HEAD
PAIR_HEADS = 2
PW = PAIR_HEADS * HEAD
N_PAIR = GW // PW
CHUNK = 64
WKV_SUB = 4
HALO = 16
LORA_PAD = 128
GRID_W = 64
N_MOD = 6
RMS_EPS = 1e-6
GN_EPS = 64e-5
KK_EPS = 1e-12
VMEM_LIMIT = 56 * 1024 * 1024


def _cparams(sem):
    return pltpu.CompilerParams(dimension_semantics=sem, vmem_limit_bytes=VMEM_LIMIT)


def _dot(a, b, nt=False):
    dn = (((1,), (1,)), ((), ())) if nt else (((1,), (0,)), ((), ()))
    return lax.dot_general(a.astype(BF16), b.astype(BF16), dn, preferred_element_type=F32)


def _split3(a):
    hi = a.astype(BF16)
    r1 = a - hi.astype(F32)
    mid = r1.astype(BF16)
    lo = (r1 - mid.astype(F32)).astype(BF16)
    return hi, mid, lo


def _dot_lhs_f32(a, b_exact):
    hi, mid, lo = _split3(a)
    return _dot(hi, b_exact) + (_dot(mid, b_exact) + _dot(lo, b_exact))


def _dot_rhs_f32(a_exact, b):
    hi, mid, lo = _split3(b)
    return _dot(a_exact, hi) + (_dot(a_exact, mid) + _dot(a_exact, lo))


def _rms_mod(x, g, shift, scale):
    ms = jnp.mean(x * x, axis=-1, keepdims=True)
    return (x * lax.rsqrt(ms + RMS_EPS) * g) * (1.0 + scale) + shift


def _adaln_kernel(c_ref, w_ref, b_ref, o_ref):
    c = c_ref[...]
    s = c * jax.nn.sigmoid(c)
    o_ref[...] = _dot(s, w_ref[...]) + b_ref[...]


def _adaln(cvec, w_mod, b_mod):
    nl, d, n = w_mod.shape
    tn = 1024
    return pl.pallas_call(
        _adaln_kernel,
        grid=(nl, n // tn),
        in_specs=[pl.BlockSpec((8, d), lambda l, j: (0, 0)),
                  pl.BlockSpec((None, d, tn), lambda l, j: (l, 0, j)),
                  pl.BlockSpec((None, 1, tn), lambda l, j: (l, 0, j))],
        out_specs=pl.BlockSpec((None, 8, tn), lambda l, j: (l, 0, j)),
        out_shape=jax.ShapeDtypeStruct((nl, 8, n), F32),
        compiler_params=_cparams(("parallel", "parallel")),
        name="adaln",
    )(cvec, w_mod, b_mod)


def _in_proj_kernel(x_ref, xp_ref, xn_ref, g_ref, sh_ref, sc_ref, w_ref, mix_ref, o_ref, h_ref):
    i = pl.program_id(1)
    tm = x_ref.shape[0]

    @pl.when(pl.program_id(2) == 0)
    def _():
        g, sh, sc = g_ref[...], sh_ref[...], sc_ref[...]
        prev = jnp.where(i > 0, _rms_mod(xp_ref[...], g, sh, sc), 0.0)
        nxt = jnp.where(i < pl.num_programs(1) - 1, _rms_mod(xn_ref[...], g, sh, sc), 0.0)
        h_ref[0:HALO, :] = prev.astype(BF16)
        h_ref[HALO:HALO + tm, :] = _rms_mod(x_ref[...], g, sh, sc).astype(BF16)
        h_ref[HALO + tm:, :] = nxt.astype(BF16)

    p = jnp.dot(h_ref[...], w_ref[...], preferred_element_type=F32)
    mix = mix_ref[...]
    te = tm + 2 * HALO
    up = pltpu.roll(p, 1, 0)[HALO:HALO + tm]
    dn = pltpu.roll(p, te - 1, 0)[HALO:HALO + tm]
    o_ref[...] = up * mix[0:1, :] + p[HALO:HALO + tm] * mix[1:2, :] + dn * mix[2:3, :]


def _in_proj(x, g, shift, scale, w, mix):
    b, l, d = x.shape
    n = w.shape[1]
    tm = min(1024, l)
    tn = 512
    hb = tm // HALO
    nhb = l // HALO
    vec = pl.BlockSpec((None, 1, d), lambda bi, i, j: (bi, 0, 0))
    return pl.pallas_call(
        _in_proj_kernel,
        grid=(b, l // tm, n // tn),
        in_specs=[pl.BlockSpec((None, tm, d), lambda bi, i, j: (bi, i, 0)),
                  pl.BlockSpec((None, HALO, d), lambda bi, i, j: (bi, jnp.maximum(i * hb - 1, 0), 0)),
                  pl.BlockSpec((None, HALO, d), lambda bi, i, j: (bi, jnp.minimum((i + 1) * hb, nhb - 1), 0)),
                  pl.BlockSpec((1, d), lambda bi, i, j: (0, 0)),
                  vec, vec,
                  pl.BlockSpec((d, tn), lambda bi, i, j: (0, j)),
                  pl.BlockSpec((3, tn), lambda bi, i, j: (0, j))],
        out_specs=pl.BlockSpec((None, tm, tn), lambda bi, i, j: (bi, i, j)),
        out_shape=jax.ShapeDtypeStruct((b, l, n), F32),
        scratch_shapes=[pltpu.VMEM((tm + 2 * HALO, d), BF16)],
        compiler_params=_cparams(("parallel", "parallel", "arbitrary")),
        name="in_proj",
    )(x, x, x, g, shift, scale, w, mix)


def _block_masks(width):
    r = lax.broadcasted_iota(jnp.int32, (width, width), 0)
    c = lax.broadcasted_iota(jnp.int32, (width, width), 1)
    return (r // HEAD) == (c // HEAD), r % HEAD, c % HEAD, r == c


def _rs(x, bd):
    return jnp.where(bd, jnp.concatenate([x] * PAIR_HEADS, axis=0), 0.0)


def _ls(x_rs):
    out = x_rs[0:CHUNK]
    for h in range(1, PAIR_HEADS):
        out = out + x_rs[h * CHUNK:(h + 1) * CHUNK]
    return out


def _each(f, *lists):
    return [f(*a) for a in zip(*lists)]


def _chunk_affine(chunks, masks):
    bd, tt, ss, eye = masks
    row = lax.broadcasted_iota(jnp.int32, (CHUNK, CHUNK), 0)
    col = lax.broadcasted_iota(jnp.int32, (CHUNK, CHUNK), 1)
    tri = {rev: jnp.where((row <= col) if rev else (row >= col), 1.0, 0.0).astype(BF16) for rev in (False, True)}
    strict = {False: ss < tt, True: ss > tt}
    t2 = lax.broadcasted_iota(jnp.int32, (PW, 2 * PW), 0) % HEAD
    s2 = lax.broadcasted_iota(jnp.int32, (PW, 2 * PW), 1) % HEAD
    incl2 = {False: s2 <= t2, True: s2 >= t2}

    crev = [ch[6] for ch in chunks]
    logw = [ch[5] for ch in chunks]
    cum = _each(lambda rv, lw: _dot_rhs_f32(tri[rv], lw), crev, logw)
    total = _each(lambda rv, cm: cm[0:1] if rv else cm[CHUNK - 1:CHUNK], crev, cum)
    p_inv = _each(lambda cm: jnp.exp(-cm), cum)
    p_end = _each(lambda tot, cm: jnp.exp(tot - cm), total, cum)
    bvec = [ch[3] * ch[4] for ch in chunks]
    wide = dict(
        a=_each(lambda ch, cm, lw: -ch[3] * jnp.exp(cm - lw), chunks, cum, logw),
        r=_each(lambda ch, cm: ch[0] * jnp.exp(cm), chunks, cum),
        b=_each(lambda b, p: b * p, bvec, p_inv),
        k=_each(lambda ch, p: ch[1] * p, chunks, p_inv),
        v=[ch[2] for ch in chunks],
        be=_each(lambda b, p: b * p, bvec, p_end),
        ke=_each(lambda ch, p: ch[1] * p, chunks, p_end),
        pc=_each(jnp.exp, total),
    )

    def pairs(name):
        return [x[:, p * PW:(p + 1) * PW] for x in wide[name] for p in range(N_PAIR)]

    rev = [rv for rv in crev for _ in range(N_PAIR)]
    r_t = pairs("r")
    a_rs = _each(lambda x: _rs(x, bd).astype(BF16), pairs("a"))
    r_rs = _each(lambda x: _rs(x, bd).astype(BF16), r_t)
    b_rs = _each(lambda x: _rs(x, bd).astype(BF16), pairs("b"))
    k_rs = _each(lambda x: _rs(x, bd).astype(BF16), pairs("k"))
    v_rs = _each(lambda x: _rs(x, bd).astype(BF16), pairs("v"))
    aa = _each(lambda a, r, b, k: _dot(jnp.concatenate([a, r], axis=0), jnp.concatenate([b, k], axis=0), nt=True),
               a_rs, r_rs, b_rs, k_rs)
    a_ab = _each(lambda rv, x: jnp.where(strict[rv], x[0:PW, 0:PW], 0.0), rev, aa)
    a_ak = _each(lambda rv, x: jnp.where(strict[rv], x[0:PW, PW:], 0.0), rev, aa)
    a_r = _each(lambda rv, x: jnp.where(incl2[rv], x[PW:, :], 0.0).astype(BF16), rev, aa)
    t_inv = _each(lambda a: jnp.where(eye, 1.0, a), a_ab)
    pw = _each(lambda a: _dot(a, a), a_ab)
    for _ in range(int(math.log2(CHUNK)) - 2):
        res = _each(lambda p, t: _dot(p, jnp.concatenate([p, t], axis=1)), pw, t_inv)
        pw = [x[:, 0:PW] for x in res]
        t_inv = _each(lambda t, x: t + x[:, PW:], t_inv, res)
    t_inv = _each(lambda p, t: (t + _dot(p, t)).astype(BF16), pw, t_inv)
    akv = _each(_dot, a_ak, v_rs)
    tw = _each(lambda t, a, x: _dot(t, jnp.concatenate([a, x.astype(BF16)], axis=1)).astype(BF16),
               t_inv, a_rs, akv)
    bk_t = _each(lambda b, k: jnp.concatenate([_rs(b, bd), _rs(k, bd)], axis=0).T.astype(BF16),
                 pairs("be"), pairs("ke"))
    fin = _each(lambda ar, bk, w, v: _dot(jnp.concatenate([ar, bk], axis=0),
                                          jnp.concatenate([w, jnp.concatenate([jnp.zeros_like(v), v], axis=1)],
                                                          axis=0)),
                a_r, bk_t, tw, v_rs)
    rh = _each(lambda r, f: r + _ls(f[0:PW, 0:PW]), r_t, fin)
    y0 = _each(lambda f: _ls(f[0:PW, PW:]), fin)
    m = _each(lambda pc, f: jnp.where(eye, pc, 0.0) + f[PW:, 0:PW], pairs("pc"), fin)
    nn = [f[PW:, PW:] for f in fin]
    out = list(zip(y0, rh, m, nn))
    return [out[i * N_PAIR:(i + 1) * N_PAIR] for i in range(len(chunks))]


def _head_sum(x, ones_bd):
    return _dot_lhs_f32(x, ones_bd)


def _wkv_kernel(nsub, rkv_f, lo_f, rkv_b, lo_b, pvec, dec_up, iclr_up, g_up, s0,
                yf_ref, yb_ref, z_ref, gate_ref, sfin_ref,
                zst):
    c = pl.program_id(2)
    nsteps = pl.num_programs(2)

    @pl.when(c == 0)
    def _():
        zst[...] = s0[...]

    masks = _block_masks(PW)
    ones_bd = jnp.where(_block_masks(GW)[0], 1.0, 0.0).astype(BF16)
    pv = pvec[...]
    k_k, k_a, r_k = pv[0:1], pv[1:2], pv[2:3]
    w0 = (pv[3:4], pv[4:5])
    a0 = (pv[5:6], pv[6:7])
    sig_scale = math.exp(-0.5)

    def streams(rkv_ref, lo_ref, d):
        rkv = rkv_ref[...]
        lo = lo_ref[...]
        r, k, v = rkv[:, 0:GW], rkv[:, GW:2 * GW], rkv[:, 2 * GW:3 * GW]
        wd = jnp.tanh(lo[:, 0:LORA_PAD])
        ad = lo[:, LORA_PAD:2 * LORA_PAD]
        gd = lo[:, 2 * LORA_PAD:]
        kk = k * k_k
        kkn = kk / jnp.maximum(jnp.sqrt(_head_sum(kk * kk, ones_bd)), KK_EPS)
        logw = -sig_scale * jax.nn.sigmoid(w0[d] + _dot(wd, dec_up[d]))
        aic = jax.nn.sigmoid(a0[d] + _dot(ad, iclr_up[d]))
        kd = k * (1.0 + (aic - 1.0) * k_a)
        return r, k, v, kkn, logw, aic, kd, ad, gd

    r, k, v, kkn, logw, aic, kd, ad, gd = streams(rkv_f, lo_f, 0)
    aic_o = jax.nn.sigmoid(a0[1] + _dot(ad, iclr_up[1]))
    kd_o = k * (1.0 + (aic_o - 1.0) * k_a)
    z_ref[...] = _head_sum(r * (kd + kd_o) * r_k, ones_bd) * v
    gate_ref[...] = _dot(jax.nn.sigmoid(gd), g_up[...])
    chains = [tuple(t[j * CHUNK:(j + 1) * CHUNK] for t in (r, kd, v, kkn, aic, logw)) + (False,)
              for j in range(nsub)]
    r, k, v, kkn, logw, aic, kd, ad, gd = streams(rkv_b, lo_b, 1)
    chains += [tuple(t[j * CHUNK:(j + 1) * CHUNK] for t in (r, kd, v, kkn, aic, logw)) + (True,)
               for j in range(nsub)]
    affine = _chunk_affine(chains, masks)
    fwd, bwd = affine[:nsub], affine[nsub:]

    for d, (steps, y_ref) in enumerate(((list(range(nsub)), yf_ref), (list(reversed(range(nsub))), yb_ref))):
        affine_d = fwd if d == 0 else bwd
        z = [zst[d, p] for p in range(N_PAIR)]
        for j in steps:
            mz = [_dot(jnp.concatenate([affine_d[j][p][2], affine_d[j][p][1]], axis=0), z[p])
                  for p in range(N_PAIR)]
            y_ref[j * CHUNK:(j + 1) * CHUNK, :] = jnp.concatenate(
                [affine_d[j][p][0] + mz[p][PW:] for p in range(N_PAIR)], axis=1)
            z = [mz[p][0:PW] + affine_d[j][p][3] for p in range(N_PAIR)]
        for p in range(N_PAIR):
            zst[d, p] = z[p]

    @pl.when(c == nsteps - 1)
    def _():
        sfin_ref[...] = zst[...]


def _wkv(px, lw, s0, nsub):
    b, l, _ = px.shape
    ng = lw["ng"]
    d_rwkv = ng * GW
    t = nsub * CHUNK
    nsteps = l // t
    lo_blk = 3 * d_rwkv // (4 * LORA_PAD)
    rkv_w = 3 * GW
    lo_w = 4 * LORA_PAD

    def main_specs(blk):
        return [pl.BlockSpec((None, t, rkv_w), lambda bi, g, c: (bi, blk(c), g)),
                pl.BlockSpec((None, t, lo_w), lambda bi, g, c: (bi, blk(c), lo_blk))]

    fblk = lambda c: c
    bblk = lambda c: nsteps - 1 - c
    in_specs = (main_specs(fblk) + main_specs(bblk) + [
        pl.BlockSpec((8, GW), lambda bi, g, c: (0, g)),
        pl.BlockSpec((2, LORA_PAD, GW), lambda bi, g, c: (0, 0, g)),
        pl.BlockSpec((2, LORA_PAD, GW), lambda bi, g, c: (0, 0, g)),
        pl.BlockSpec((2 * LORA_PAD, GW), lambda bi, g, c: (0, g)),
        pl.BlockSpec((None, 2, N_PAIR, PW, PW), lambda bi, g, c: (bi, 0, g, 0, 0)),
    ])
    yspec = lambda blk: pl.BlockSpec((None, t, GW), lambda bi, g, c: (bi, blk(c), g))
    out_specs = [yspec(fblk), yspec(bblk), yspec(fblk), yspec(fblk),
                 pl.BlockSpec((None, 2, N_PAIR, PW, PW), lambda bi, g, c: (bi, 0, g, 0, 0))]
    ysh = jax.ShapeDtypeStruct((b, l, d_rwkv), F32)
    return pl.pallas_call(
        functools.partial(_wkv_kernel, nsub),
        grid=(b, ng, nsteps),
        in_specs=in_specs,
        out_specs=out_specs,
        out_shape=[ysh, ysh, ysh, ysh, jax.ShapeDtypeStruct(s0.shape, F32)],
        scratch_shapes=[pltpu.VMEM((2, N_PAIR, PW, PW), F32)],
        compiler_params=_cparams(("parallel", "parallel", "arbitrary")),
        name="wkv",
    )(px, px, px, px, lw["pvec"], lw["dec_up"], lw["iclr_up"], lw["g_up"], s0)


def _dft_ch_kernel(u_ref, t_ref, o_ref):
    u = u_ref[...]
    hi = u.astype(BF16)
    lo = (u - hi.astype(F32)).astype(BF16)
    tab = t_ref[...]
    o_ref[...] = (_dot(hi, tab) + _dot(lo, tab)).astype(BF16)


def _dft_channels(px, tabs, ft_blk, d_f):
    b, l, _ = px.shape
    tm = min(512, l)
    nt = l // tm
    return pl.pallas_call(
        _dft_ch_kernel,
        grid=(b, 2, nt),
        in_specs=[pl.BlockSpec((None, tm, d_f), lambda bi, s, i: (bi, i, ft_blk)),
                  pl.BlockSpec((None, d_f, d_f), lambda bi, s, i: (s, 0, 0))],
        out_specs=pl.BlockSpec((tm, d_f), lambda bi, s, i: (s * nt + i, bi)),
        out_shape=jax.ShapeDtypeStruct((2 * l, b * d_f), BF16),
        compiler_params=_cparams(("parallel", "parallel", "parallel")),
        name="dft_channels",
    )(px, tabs)


def _dft_seq_kernel(w_ref, u_ref, o_ref, acc_ref):
    kk = pl.program_id(1)

    @pl.when(kk == 0)
    def _():
        acc_ref[...] = jnp.zeros_like(acc_ref)

    acc_ref[...] += jnp.dot(w_ref[...], u_ref[...], preferred_element_type=F32)

    @pl.when(kk == pl.num_programs(1) - 1)
    def _():
        o_ref[...] = acc_ref[...]


def _dft_seq(wtab, uu):
    l, k2 = wtab.shape
    n = uu.shape[1]
    tm = min(1024, l)
    tk = min(2048, k2)
    return pl.pallas_call(
        _dft_seq_kernel,
        grid=(l // tm, k2 // tk),
        in_specs=[pl.BlockSpec((tm, tk), lambda i, kk: (i, kk)),
                  pl.BlockSpec((tk, n), lambda i, kk: (kk, 0))],
        out_specs=pl.BlockSpec((tm, n), lambda i, kk: (i, 0)),
        out_shape=jax.ShapeDtypeStruct((l, n), F32),
        scratch_shapes=[pltpu.VMEM((tm, n), F32)],
        compiler_params=_cparams(("parallel", "arbitrary")),
        name="dft_seq",
    )(wtab, uu)


def _dft_tables(l, d_f):
    f = math.gcd(l, 128)

    def thin(n, stride):
        p = lax.broadcasted_iota(jnp.int32, (l, n), 0)
        q = lax.broadcasted_iota(jnp.int32, (l, n), 1) * stride
        ang = ((p * q) % l).astype(F32) * (2.0 * math.pi / l)
        return jnp.cos(ang), jnp.sin(ang)

    c1, s1 = (t[:, :, None] for t in thin(l // f, f))
    c2, s2 = (t[:, None, :] for t in thin(f, 1))
    wtab = jnp.concatenate([(c1 * c2 - s1 * s2).reshape(l, l), (-(s1 * c2 + c1 * s2)).reshape(l, l)],
                           axis=1).astype(BF16)
    cc = lax.broadcasted_iota(jnp.int32, (d_f, d_f), 0)
    qq = lax.broadcasted_iota(jnp.int32, (d_f, d_f), 1)
    a2 = (((cc % HEAD) * (qq % HEAD)) % HEAD).astype(F32) * (2.0 * math.pi / HEAD)
    same = (cc // HEAD) == (qq // HEAD)
    scale = 1.0 / math.sqrt(l * HEAD)
    ctab = jnp.stack([jnp.where(same, jnp.cos(a2), 0.0), jnp.where(same, jnp.sin(a2), 0.0)]) * scale
    return wtab, ctab.astype(BF16)


def _mix_out_kernel(period, yf_ref, yb_ref, z_ref, gate_ref, cg_ref, cx_ref, cb_ref, ft_ref, x_ref,
                    ga_ref, lnw_ref, lnb_ref, cw_ref, w_ref, o_ref, mix_ref):
    d_rwkv = yf_ref.shape[1]
    d_conv = cg_ref.shape[1]
    tm = yf_ref.shape[0]
    r = lax.broadcasted_iota(jnp.int32, (GW, GW), 0)
    c = lax.broadcasted_iota(jnp.int32, (GW, GW), 1)
    ones_bd = jnp.where((r // HEAD) == (c // HEAD), 1.0, 0.0).astype(BF16)
    inv_n = 1.0 / HEAD
    for s in range(d_rwkv // GW):
        sl = slice(s * GW, (s + 1) * GW)
        y = yf_ref[:, sl] + yb_ref[:, sl]
        mu = _dot_lhs_f32(y, ones_bd) * inv_n
        dlt = y - mu
        var = _dot_lhs_f32(dlt * dlt, ones_bd) * inv_n
        yn = dlt * lax.rsqrt(var + GN_EPS) * lnw_ref[:, sl] + lnb_ref[:, sl] + z_ref[:, sl]
        mix_ref[:, sl] = (yn * gate_ref[:, sl]).astype(BF16)
    u = cg_ref[...] * cx_ref[...]
    rowid = lax.broadcasted_iota(jnp.int32, u.shape, 0) % period
    up = jnp.where(rowid == 0, 0.0, pltpu.roll(u, 1, 0))
    dn = jnp.where(rowid == period - 1, 0.0, pltpu.roll(u, tm - 1, 0))
    cw = cw_ref[...]
    conv = cb_ref[...] * (up * cw[0:1] + u * cw[1:2] + dn * cw[2:3])
    mix_ref[:, d_rwkv:d_rwkv + d_conv] = conv.astype(BF16)
    mix_ref[:, d_rwkv + d_conv:] = ft_ref[...].astype(BF16)
    o_ref[...] = x_ref[...] + ga_ref[...] * jnp.dot(mix_ref[...], w_ref[...], preferred_element_type=F32)


def _mix_out(yf, yb, z, gate, px, ft, x, ga, lw, period, conv_blk):
    b, l, d = x.shape
    d_rwkv = yf.shape[2]
    d_conv = lw["conv_w"].shape[1]
    d_f = ft.shape[1] // b
    tm = min(256, l)
    assert tm % period == 0
    yspec = pl.BlockSpec((None, tm, d_rwkv), lambda bi, i: (bi, i, 0))
    cspec = lambda off: pl.BlockSpec((None, tm, d_conv), lambda bi, i: (bi, i, conv_blk + off))
    return pl.pallas_call(
        functools.partial(_mix_out_kernel, period),
        grid=(b, l // tm),
        in_specs=[yspec, yspec, yspec, yspec, cspec(0), cspec(1), cspec(2),
                  pl.BlockSpec((tm, d_f), lambda bi, i: (i, bi)),
                  pl.BlockSpec((None, tm, d), lambda bi, i: (bi, i, 0)),
                  pl.BlockSpec((None, 1, d), lambda bi, i: (bi, 0, 0)),
                  pl.BlockSpec((1, d_rwkv), lambda bi, i: (0, 0)),
                  pl.BlockSpec((1, d_rwkv), lambda bi, i: (0, 0)),
                  pl.BlockSpec((3, d_conv), lambda bi, i: (0, 0)),
                  pl.BlockSpec(lw["w_out"].shape, lambda bi, i: (0, 0))],
        out_specs=pl.BlockSpec((None, tm, d), lambda bi, i: (bi, i, 0)),
        out_shape=jax.ShapeDtypeStruct((b, l, d), F32),
        scratch_shapes=[pltpu.VMEM((tm, lw["w_out"].shape[0]), BF16)],
        compiler_params=_cparams(("parallel", "parallel")),
        name="mix_out",
    )(yf, yb, z, gate, px, px, px, ft, x, ga, lw["ln_w"], lw["ln_b"], lw["conv_w"], lw["w_out"])


def _ffn_kernel(final, x_ref, g_ref, sh_ref, sc_ref, ga_ref, wg_ref, wu_ref, wd_ref, gf_ref, o_ref,
                h_ref, acc_ref):
    j = pl.program_id(2)

    @pl.when(j == 0)
    def _():
        h_ref[...] = _rms_mod(x_ref[...], g_ref[...], sh_ref[...], sc_ref[...]).astype(BF16)
        acc_ref[...] = jnp.zeros_like(acc_ref)

    h = h_ref[...]
    a = jnp.dot(h, wg_ref[...], preferred_element_type=F32)
    u = jnp.dot(h, wu_ref[...], preferred_element_type=F32)
    t = (a * jax.nn.sigmoid(a)) * u
    acc_ref[...] += jnp.dot(t.astype(BF16), wd_ref[...], preferred_element_type=F32)

    @pl.when(j == pl.num_programs(2) - 1)
    def _():
        xn = x_ref[...] + ga_ref[...] * acc_ref[...]
        if final:
            ms = jnp.mean(xn * xn, axis=-1, keepdims=True)
            xn = xn * lax.rsqrt(ms + RMS_EPS) * gf_ref[...]
        o_ref[...] = xn


def _ffn(x, g, shift, scale, ga, wg, wu, wd, g_final, final):
    b, l, d = x.shape
    ff = wg.shape[1]
    tm = min(512, l)
    tf = 512
    vec = pl.BlockSpec((None, 1, d), lambda bi, i, j: (bi, 0, 0))
    gspec = pl.BlockSpec((1, d), lambda bi, i, j: (0, 0))
    return pl.pallas_call(
        functools.partial(_ffn_kernel, final),
        grid=(b, l // tm, ff // tf),
        in_specs=[pl.BlockSpec((None, tm, d), lambda bi, i, j: (bi, i, 0)),
                  gspec, vec, vec, vec,
                  pl.BlockSpec((d, tf), lambda bi, i, j: (0, j)),
                  pl.BlockSpec((d, tf), lambda bi, i, j: (0, j)),
                  pl.BlockSpec((tf, d), lambda bi, i, j: (j, 0)),
                  gspec],
        out_specs=pl.BlockSpec((None, tm, d), lambda bi, i, j: (bi, i, 0)),
        out_shape=jax.ShapeDtypeStruct((b, l, d), F32),
        scratch_shapes=[pltpu.VMEM((tm, d), BF16), pltpu.VMEM((tm, d), F32)],
        compiler_params=_cparams(("parallel", "parallel", "arbitrary")),
        name="ffn",
    )(x, g, shift, scale, ga, wg, wu, wd, g_final)


def _layer_weights(i, w_in, rw_shift, dec_w0, dec_up, iclr_a0, iclr_up, k_k, k_a, r_k, ln_w, ln_b, g_up,
                   conv_w, w_out, w_gate, w_up, w_down, dims):
    d_rwkv, d_conv, d_f, dl, il, gl = dims
    ng = d_rwkv // GW
    wi = w_in[i]
    sh = rw_shift[i]
    r0, k0, v0 = 0, d_rwkv, 2 * d_rwkv
    wd0 = 3 * d_rwkv
    ad0 = wd0 + dl
    gd0 = ad0 + il
    rest0 = gd0 + gl

    def regroup(m):
        cols = []
        for g in range(ng):
            for base in (r0, k0, v0):
                cols.append(m[:, base + g * GW: base + (g + 1) * GW])
        padl = jnp.zeros((m.shape[0], LORA_PAD - dl), m.dtype)
        padi = jnp.zeros((m.shape[0], LORA_PAD - il), m.dtype)
        cols += [m[:, wd0:ad0], padl, m[:, ad0:gd0], padi, m[:, gd0:rest0]]
        return jnp.concatenate(cols, axis=1)

    w_rw = regroup(wi)
    sh_all = regroup(sh)
    pad_rows = lambda m, n: jnp.concatenate([m, jnp.zeros(m.shape[:-2] + (n - m.shape[-2], m.shape[-1]), m.dtype)], -2)
    zero = jnp.zeros((d_rwkv,), F32)
    pvec = jnp.stack([k_k[i], k_a[i], r_k[i].reshape(-1), dec_w0[i, 0], dec_w0[i, 1],
                      iclr_a0[i, 0], iclr_a0[i, 1], zero])
    return dict(
        ng=ng,
        w_in=jnp.concatenate([w_rw, wi[:, rest0:]], axis=1).astype(BF16),
        mix=jnp.concatenate([sh_all, jnp.broadcast_to(jnp.array([[0.0], [1.0], [0.0]], F32),
                                                        (3, wi.shape[1] - rest0))], axis=1),
        pvec=pvec,
        dec_up=pad_rows(dec_up[i], LORA_PAD).astype(BF16),
        iclr_up=pad_rows(iclr_up[i], LORA_PAD).astype(BF16),
        g_up=g_up[i].astype(BF16),
        ln_w=ln_w[i][None, :], ln_b=ln_b[i][None, :],
        conv_w=conv_w[i],
        w_out=w_out[i].astype(BF16),
        w_gate=w_gate[i].astype(BF16), w_up=w_up[i].astype(BF16), w_down=w_down[i].astype(BF16),
    )


def kernel(x, c, ctx, c_ctx, w_mod, b_mod, norm_mix, w_in, rw_shift, dec_w0, dec_up, iclr_a0, iclr_up,
           k_k, k_a, r_k, ln_w, ln_b, g_up, conv_w, w_out, norm_ffn, w_gate, w_up, w_down, norm_final):
    b, l, d = x.shape
    lc = ctx.shape[1]
    depth = w_mod.shape[0]
    d_rwkv = k_k.shape[1]
    d_conv = conv_w.shape[2]
    dl, il, gl = dec_up.shape[2], iclr_up.shape[2], g_up.shape[1]
    d_f = w_out.shape[1] - d_rwkv - d_conv
    assert dl <= LORA_PAD and il <= LORA_PAD and gl == 2 * LORA_PAD and d_rwkv % GW == 0
    assert l % GRID_W == 0 and l % CHUNK == 0 and lc % CHUNK == 0
    nsub_x = math.gcd(l // CHUNK, WKV_SUB)
    nsub_c = math.gcd(lc // CHUNK, WKV_SUB)
    ng = d_rwkv // GW
    n_rw = 3 * d_rwkv + 4 * LORA_PAD
    conv_blk = n_rw // d_conv
    ft_blk = (n_rw + 3 * d_conv) // d_f
    assert conv_blk * d_conv == n_rw and ft_blk * d_f == n_rw + 3 * d_conv

    cvec = jnp.concatenate([c, c_ctx[None, :], jnp.zeros((8 - b - 1, d), F32)], axis=0)
    mods = _adaln(cvec, w_mod, b_mod[:, None, :])

    def mod(i, j, ctx_rows):
        m = mods[i, :, j * d:(j + 1) * d]
        if ctx_rows:
            return jnp.broadcast_to(m[b:b + 1], (b, d))[:, None, :]
        return m[:b][:, None, :]

    wtab_x, ctab_x = _dft_tables(l, d_f)
    wtab_c, ctab_c = _dft_tables(lc, d_f)
    dims = (d_rwkv, d_conv, d_f, dl, il, gl)
    s_zero = jnp.zeros((b, 2, ng * N_PAIR, PW, PW), F32)
    gfin = norm_final[None, :]
    xc = ctx
    for i in range(depth):
        lw = _layer_weights(i, w_in, rw_shift, dec_w0, dec_up, iclr_a0, iclr_up, k_k, k_a, r_k, ln_w, ln_b,
                            g_up, conv_w, w_out, w_gate, w_up, w_down, dims)
        gmix = norm_mix[i][None, :]
        gffn = norm_ffn[i][None, :]
        last = i == depth - 1
        pc = _in_proj(xc, gmix, mod(i, 0, True), mod(i, 1, True), lw["w_in"], lw["mix"])
        px = _in_proj(x, gmix, mod(i, 0, False), mod(i, 1, False), lw["w_in"], lw["mix"])
        yfc, ybc, zc, gc, s_ctx = _wkv(pc, lw, s_zero, nsub_c)
        yfx, ybx, zx, gx, _ = _wkv(px, lw, s_ctx, nsub_x)
        ftx = _dft_seq(wtab_x, _dft_channels(px, ctab_x, ft_blk, d_f))
        x = _mix_out(yfx, ybx, zx, gx, px, ftx, x, mod(i, 2, False), lw, GRID_W, conv_blk)
        x = _ffn(x, gffn, mod(i, 3, False), mod(i, 4, False), mod(i, 5, False),
                 lw["w_gate"], lw["w_up"], lw["w_down"], gfin, last)
        if not last:
            ftc = _dft_seq(wtab_c, _dft_channels(pc, ctab_c, ft_blk, d_f))
            xc = _mix_out(yfc, ybc, zc, gc, pc, ftc, xc, mod(i, 2, True), lw, lc, conv_blk)
            xc = _ffn(xc, gffn, mod(i, 3, True), mod(i, 4, True), mod(i, 5, True),
                      lw["w_gate"], lw["w_up"], lw["w_down"], gfin, False)
    return x
```

```python
import functools
import math

import jax
import jax.numpy as jnp
from jax import lax
from jax.experimental import pallas as pl
from jax.experimental.pallas import tpu as pltpu

F32 = jnp.float32
BF16 = jnp.bfloat16

HEAD = 64
GROUP_HEADS = 4
GW = GROUP_HEADS * HEAD
PAIR_HEADS = 2
PW = PAIR_HEADS * HEAD
N_PAIR = GW // PW
CHUNK = 64
WKV_SUB = 4
HALO = 16
LORA_PAD = 128
GRID_W = 64
FFT_L1 = 64
N_MOD = 6
RMS_EPS = 1e-6
GN_EPS = 64e-5
KK_EPS = 1e-12
VMEM_LIMIT = 56 * 1024 * 1024


def _cparams(sem):
    return pltpu.CompilerParams(dimension_semantics=sem, vmem_limit_bytes=VMEM_LIMIT)


def _dot(a, b, nt=False):
    dn = (((1,), (1,)), ((), ())) if nt else (((1,), (0,)), ((), ()))
    return lax.dot_general(a.astype(BF16), b.astype(BF16), dn, preferred_element_type=F32)


def _split2(a):
    hi = a.astype(BF16)
    return hi, (a - hi.astype(F32)).astype(BF16)


def _dot_lhs_f32(a, b_exact):
    hi, lo = _split2(a)
    return _dot(hi, b_exact) + _dot(lo, b_exact)


def _dot_rhs_f32(a_exact, b):
    hi, lo = _split2(b)
    return _dot(a_exact, hi) + _dot(a_exact, lo)


def _rms_mod(x, g, shift, scale):
    ms = jnp.mean(x * x, axis=-1, keepdims=True)
    return (x * lax.rsqrt(ms + RMS_EPS) * g) * (1.0 + scale) + shift


def _adaln_kernel(c_ref, w_ref, b_ref, o_ref):
    c = c_ref[...]
    s = c * jax.nn.sigmoid(c)
    o_ref[...] = _dot(s, w_ref[...]) + b_ref[...]


def _adaln(cvec, w_mod, b_mod):
    nl, d, n = w_mod.shape
    tn = 1024
    return pl.pallas_call(
        _adaln_kernel,
        grid=(nl, n // tn),
        in_specs=[pl.BlockSpec((8, d), lambda l, j: (0, 0)),
                  pl.BlockSpec((None, d, tn), lambda l, j: (l, 0, j)),
                  pl.BlockSpec((None, 1, tn), lambda l, j: (l, 0, j))],
        out_specs=pl.BlockSpec((None, 8, tn), lambda l, j: (l, 0, j)),
        out_shape=jax.ShapeDtypeStruct((nl, 8, n), F32),
        compiler_params=_cparams(("parallel", "parallel")),
        name="adaln",
    )(cvec, w_mod, b_mod)


def _in_proj_kernel(n_mix_tiles, x_ref, xp_ref, xn_ref, g_ref, sh_ref, sc_ref, w_ref, mix_ref, o_ref, h_ref):
    i = pl.program_id(1)
    tm = x_ref.shape[0]

    @pl.when(pl.program_id(2) == 0)
    def _():
        g, sh, sc = g_ref[...], sh_ref[...], sc_ref[...]
        prev = jnp.where(i > 0, _rms_mod(xp_ref[...], g, sh, sc), 0.0)
        nxt = jnp.where(i < pl.num_programs(1) - 1, _rms_mod(xn_ref[...], g, sh, sc), 0.0)
        h_ref[0:HALO, :] = prev.astype(BF16)
        h_ref[HALO:HALO + tm, :] = _rms_mod(x_ref[...], g, sh, sc).astype(BF16)
        h_ref[HALO + tm:, :] = nxt.astype(BF16)

    @pl.when(pl.program_id(2) < n_mix_tiles)
    def _():
        p = jnp.dot(h_ref[...], w_ref[...], preferred_element_type=F32)
        mix = mix_ref[...]
        up = pltpu.roll(p, 1, 0)[HALO:HALO + tm]
        dn = pltpu.roll(p, tm + 2 * HALO - 1, 0)[HALO:HALO + tm]
        o_ref[...] = up * mix[0:1, :] + p[HALO:HALO + tm] * mix[1:2, :] + dn * mix[2:3, :]

    @pl.when(pl.program_id(2) >= n_mix_tiles)
    def _():
        o_ref[...] = jnp.dot(h_ref[HALO:HALO + tm, :], w_ref[...], preferred_element_type=F32)


def _in_proj(x, g, shift, scale, w, mix):
    b, l, d = x.shape
    n = w.shape[1]
    tm = min(1024, l)
    tn = 512
    n_mix_tiles = mix.shape[1] // tn
    assert n_mix_tiles * tn == mix.shape[1]
    hb = tm // HALO
    nhb = l // HALO
    vec = pl.BlockSpec((None, 1, d), lambda bi, i, j: (bi, 0, 0))
    return pl.pallas_call(
        functools.partial(_in_proj_kernel, n_mix_tiles),
        grid=(b, l // tm, n // tn),
        in_specs=[pl.BlockSpec((None, tm, d), lambda bi, i, j: (bi, i, 0)),
                  pl.BlockSpec((None, HALO, d), lambda bi, i, j: (bi, jnp.maximum(i * hb - 1, 0), 0)),
                  pl.BlockSpec((None, HALO, d), lambda bi, i, j: (bi, jnp.minimum((i + 1) * hb, nhb - 1), 0)),
                  pl.BlockSpec((1, d), lambda bi, i, j: (0, 0)),
                  vec, vec,
                  pl.BlockSpec((d, tn), lambda bi, i, j: (0, j)),
                  pl.BlockSpec((3, tn), lambda bi, i, j: (0, jnp.minimum(j, n_mix_tiles - 1)))],
        out_specs=pl.BlockSpec((None, tm, tn), lambda bi, i, j: (bi, i, j)),
        out_shape=jax.ShapeDtypeStruct((b, l, n), F32),
        scratch_shapes=[pltpu.VMEM((tm + 2 * HALO, d), BF16)],
        compiler_params=_cparams(("parallel", "parallel", "arbitrary")),
        name="in_proj",
    )(x, x, x, g, shift, scale, w, mix)


def _block_masks(width):
    r = lax.broadcasted_iota(jnp.int32, (width, width), 0)
    c = lax.broadcasted_iota(jnp.int32, (width, width), 1)
    return (r // HEAD) == (c // HEAD), r % HEAD, c % HEAD, r == c


def _rs(x, bd):
    return jnp.where(bd, jnp.concatenate([x] * PAIR_HEADS, axis=0), 0.0)


def _ls(x_rs):
    out = x_rs[0:CHUNK]
    for h in range(1, PAIR_HEADS):
        out = out + x_rs[h * CHUNK:(h + 1) * CHUNK]
    return out


def _each(f, *lists):
    return [f(*a) for a in zip(*lists)]


def _chunk_affine(chunks, masks):
    bd, tt, ss, eye = masks
    row = lax.broadcasted_iota(jnp.int32, (CHUNK, CHUNK), 0)
    col = lax.broadcasted_iota(jnp.int32, (CHUNK, CHUNK), 1)
    tri = {rev: jnp.where((row <= col) if rev else (row >= col), 1.0, 0.0).astype(BF16) for rev in (False, True)}
    strict = {False: bd & (ss < tt), True: bd & (ss > tt)}
    r2 = lax.broadcasted_iota(jnp.int32, (PW, 2 * PW), 0)
    c2 = lax.broadcasted_iota(jnp.int32, (PW, 2 * PW), 1)
    bd2 = (r2 // HEAD) == ((c2 % PW) // HEAD)
    incl2 = {False: bd2 & (c2 % HEAD <= r2 % HEAD), True: bd2 & (c2 % HEAD >= r2 % HEAD)}

    crev = [ch[6] for ch in chunks]
    logw = [ch[5] for ch in chunks]
    cum = _each(lambda rv, lw: _dot_rhs_f32(tri[rv], lw), crev, logw)
    total = _each(lambda rv, cm: cm[0:1] if rv else cm[CHUNK - 1:CHUNK], crev, cum)
    p_inv = _each(lambda cm: jnp.exp(-cm), cum)
    p_end = _each(lambda tot, cm: jnp.exp(tot - cm), total, cum)
    bvec = [ch[3] * ch[4] for ch in chunks]
    wide = dict(
        a=_each(lambda ch, cm, lw: -ch[3] * jnp.exp(cm - lw), chunks, cum, logw),
        r=_each(lambda ch, cm: ch[0] * jnp.exp(cm), chunks, cum),
        b=_each(lambda b, p: b * p, bvec, p_inv),
        k=_each(lambda ch, p: ch[1] * p, chunks, p_inv),
        v=[ch[2] for ch in chunks],
        be=_each(lambda b, p: b * p, bvec, p_end),
        ke=_each(lambda ch, p: ch[1] * p, chunks, p_end),
        pc=_each(jnp.exp, total),
    )

    def pairs(name):
        return [x[:, p * PW:(p + 1) * PW] for x in wide[name] for p in range(N_PAIR)]

    rev = [rv for rv in crev for _ in range(N_PAIR)]
    r_t = pairs("r")
    a_rs = _each(lambda x: _rs(x, bd).astype(BF16), pairs("a"))
    r_rs = _each(lambda x: _rs(x, bd).astype(BF16), r_t)
    b_rs = _each(lambda x: jnp.concatenate([x.astype(BF16)] * PAIR_HEADS, axis=0), pairs("b"))
    k_rs = _each(lambda x: jnp.concatenate([x.astype(BF16)] * PAIR_HEADS, axis=0), pairs("k"))
    v_rs = _each(lambda x: _rs(x, bd).astype(BF16), pairs("v"))
    aa = _each(lambda a, r, b, k: _dot(jnp.concatenate([a, r], axis=0), jnp.concatenate([b, k], axis=0), nt=True),
               a_rs, r_rs, b_rs, k_rs)
    a_ab = _each(lambda rv, x: jnp.where(strict[rv], x[0:PW, 0:PW], 0.0), rev, aa)
    a_ak = _each(lambda rv, x: jnp.where(strict[rv], x[0:PW, PW:], 0.0), rev, aa)
    a_r = _each(lambda rv, x: jnp.where(incl2[rv], x[PW:, :], 0.0).astype(BF16), rev, aa)
    t_inv = _each(lambda a: jnp.where(eye, 1.0, a), a_ab)
    pw = _each(lambda a: _dot(a, a), a_ab)
    for _ in range(int(math.log2(CHUNK)) - 2):
        res = _each(lambda p, t: _dot(p, jnp.concatenate([p, t], axis=1)), pw, t_inv)
        pw = [x[:, 0:PW] for x in res]
        t_inv = _each(lambda t, x: t + x[:, PW:], t_inv, res)
    t_inv = _each(lambda p, t: (t + _dot(p, t)).astype(BF16), pw, t_inv)
    akv = _each(_dot, a_ak, v_rs)
    tw = _each(lambda t, a, x: _dot(t, jnp.concatenate([a, x.astype(BF16)], axis=1)).astype(BF16),
               t_inv, a_rs, akv)
    bk_t = _each(lambda b, k: jnp.concatenate([_rs(b, bd), _rs(k, bd)], axis=0).T.astype(BF16),
                 pairs("be"), pairs("ke"))
    fin = _each(lambda ar, bk, w, v: _dot(jnp.concatenate([ar, bk], axis=0),
                                          jnp.concatenate([w, jnp.concatenate([jnp.zeros_like(v), v], axis=1)],
                                                          axis=0)),
                a_r, bk_t, tw, v_rs)
    rh = _each(lambda r, f: r + _ls(f[0:PW, 0:PW]), r_t, fin)
    y0 = _each(lambda f: _ls(f[0:PW, PW:]), fin)
    m = _each(lambda pc, f: jnp.where(eye, pc, 0.0) + f[PW:, 0:PW], pairs("pc"), fin)
    nn = [f[PW:, PW:] for f in fin]
    out = list(zip(y0, rh, m, nn))
    return [out[i * N_PAIR:(i + 1) * N_PAIR] for i in range(len(chunks))]


def _head_sum(x, ones_bd):
    return _dot_lhs_f32(x, ones_bd)


def _wkv_kernel(nsub, rkv_f, lo_f, rkv_b, lo_b, pvec, dec_up, iclr_up, g_up, s0,
                yf_ref, yb_ref, z_ref, gate_ref, sfin_ref,
                zst):
    c = pl.program_id(2)
    nsteps = pl.num_programs(2)

    @pl.when(c == 0)
    def _():
        zst[...] = s0[...]

    masks = _block_masks(PW)
    ones_bd = jnp.where(_block_masks(GW)[0], 1.0, 0.0).astype(BF16)
    pv = pvec[...]
    k_k, k_a, r_k = pv[0:1], pv[1:2], pv[2:3]
    w0 = (pv[3:4], pv[4:5])
    a0 = (pv[5:6], pv[6:7])
    sig_scale = math.exp(-0.5)

    def streams(rkv_ref, lo_ref, d):
        rkv = rkv_ref[...]
        lo = lo_ref[...]
        r, k, v = rkv[:, 0:GW], rkv[:, GW:2 * GW], rkv[:, 2 * GW:3 * GW]
        wd = jnp.tanh(lo[:, 0:LORA_PAD])
        ad = lo[:, LORA_PAD:2 * LORA_PAD]
        gd = lo[:, 2 * LORA_PAD:]
        kk = k * k_k
        kkn = kk * lax.rsqrt(jnp.maximum(_head_sum(kk * kk, ones_bd), KK_EPS * KK_EPS))
        logw = -sig_scale * jax.nn.sigmoid(w0[d] + _dot(wd, dec_up[d]))
        aic = jax.nn.sigmoid(a0[d] + _dot(ad, iclr_up[d]))
        kd = k * (1.0 + (aic - 1.0) * k_a)
        return r, k, v, kkn, logw, aic, kd, ad, gd

    r, k, v, kkn, logw, aic, kd, ad, gd = streams(rkv_f, lo_f, 0)
    aic_o = jax.nn.sigmoid(a0[1] + _dot(ad, iclr_up[1]))
    kd_o = k * (1.0 + (aic_o - 1.0) * k_a)
    z_ref[...] = _head_sum(r * (kd + kd_o) * r_k, ones_bd) * v
    gate_ref[...] = _dot(jax.nn.sigmoid(gd), g_up[...])
    chains = [tuple(t[j * CHUNK:(j + 1) * CHUNK] for t in (r, kd, v, kkn, aic, logw)) + (False,)
              for j in range(nsub)]
    r, k, v, kkn, logw, aic, kd, ad, gd = streams(rkv_b, lo_b, 1)
    chains += [tuple(t[j * CHUNK:(j + 1) * CHUNK] for t in (r, kd, v, kkn, aic, logw)) + (True,)
               for j in range(nsub)]
    affine = _chunk_affine(chains, masks)
    fwd, bwd = affine[:nsub], affine[nsub:]

    for d, (steps, y_ref) in enumerate(((list(range(nsub)), yf_ref), (list(reversed(range(nsub))), yb_ref))):
        affine_d = fwd if d == 0 else bwd
        z = [zst[d, p] for p in range(N_PAIR)]
        for j in steps:
            mz = [_dot(jnp.concatenate([affine_d[j][p][2], affine_d[j][p][1]], axis=0), z[p])
                  for p in range(N_PAIR)]
            y_ref[j * CHUNK:(j + 1) * CHUNK, :] = jnp.concatenate(
                [affine_d[j][p][0] + mz[p][PW:] for p in range(N_PAIR)], axis=1)
            z = [mz[p][0:PW] + affine_d[j][p][3] for p in range(N_PAIR)]
        for p in range(N_PAIR):
            zst[d, p] = z[p]

    @pl.when(c == nsteps - 1)
    def _():
        sfin_ref[...] = zst[...]


def _wkv(px, lw, s0, nsub):
    b, l, _ = px.shape
    ng = lw["ng"]
    d_rwkv = ng * GW
    t = nsub * CHUNK
    nsteps = l // t
    lo_blk = 3 * d_rwkv // (4 * LORA_PAD)
    rkv_w = 3 * GW
    lo_w = 4 * LORA_PAD

    def main_specs(blk):
        return [pl.BlockSpec((None, t, rkv_w), lambda bi, g, c: (bi, blk(c), g)),
                pl.BlockSpec((None, t, lo_w), lambda bi, g, c: (bi, blk(c), lo_blk))]

    fblk = lambda c: c
    bblk = lambda c: nsteps - 1 - c
    in_specs = (main_specs(fblk) + main_specs(bblk) + [
        pl.BlockSpec((8, GW), lambda bi, g, c: (0, g)),
        pl.BlockSpec((2, LORA_PAD, GW), lambda bi, g, c: (0, 0, g)),
        pl.BlockSpec((2, LORA_PAD, GW), lambda bi, g, c: (0, 0, g)),
        pl.BlockSpec((2 * LORA_PAD, GW), lambda bi, g, c: (0, g)),
        pl.BlockSpec((None, 2, N_PAIR, PW, PW), lambda bi, g, c: (bi, 0, g, 0, 0)),
    ])
    yspec = lambda blk: pl.BlockSpec((None, t, GW), lambda bi, g, c: (bi, blk(c), g))
    out_specs = [yspec(fblk), yspec(bblk), yspec(fblk), yspec(fblk),
                 pl.BlockSpec((None, 2, N_PAIR, PW, PW), lambda bi, g, c: (bi, 0, g, 0, 0))]
    ysh = jax.ShapeDtypeStruct((b, l, d_rwkv), F32)
    return pl.pallas_call(
        functools.partial(_wkv_kernel, nsub),
        grid=(b, ng, nsteps),
        in_specs=in_specs,
        out_specs=out_specs,
        out_shape=[ysh, ysh, ysh, ysh, jax.ShapeDtypeStruct(s0.shape, F32)],
        scratch_shapes=[pltpu.VMEM((2, N_PAIR, PW, PW), F32)],
        compiler_params=_cparams(("parallel", "parallel", "arbitrary")),
        name="wkv",
    )(px, px, px, px, lw["pvec"], lw["dec_up"], lw["iclr_up"], lw["g_up"], s0)


def _dft_ch_kernel(u_ref, t_ref, o_ref):
    u = u_ref[...]
    hi = u.astype(BF16)
    lo = (u - hi.astype(F32)).astype(BF16)
    tab = t_ref[...]
    o_ref[...] = (_dot(hi, tab) + _dot(lo, tab)).astype(BF16)


def _dft_channels(px, tabs, ft_blk, d_f):
    b, l, _ = px.shape
    tm = min(512, l)
    nt = l // tm
    return pl.pallas_call(
        _dft_ch_kernel,
        grid=(b, 2, nt),
        in_specs=[pl.BlockSpec((None, tm, d_f), lambda bi, s, i: (bi, i, ft_blk)),
                  pl.BlockSpec((None, d_f, d_f), lambda bi, s, i: (s, 0, 0))],
        out_specs=pl.BlockSpec((tm, d_f), lambda bi, s, i: (s * nt + i, bi)),
        out_shape=jax.ShapeDtypeStruct((2 * l, b * d_f), BF16),
        compiler_params=_cparams(("parallel", "parallel", "parallel")),
        name="dft_channels",
    )(px, tabs)


def _dft_seq_kernel(w_ref, u_ref, o_ref, acc_ref):
    kk = pl.program_id(1)

    @pl.when(kk == 0)
    def _():
        acc_ref[...] = jnp.zeros_like(acc_ref)

    acc_ref[...] += jnp.dot(w_ref[...], u_ref[...], preferred_element_type=F32)

    @pl.when(kk == pl.num_programs(1) - 1)
    def _():
        o_ref[...] = acc_ref[...]


def _dft_seq(wtab, uu):
    l, k2 = wtab.shape
    n = uu.shape[1]
    tm = min(1024, l)
    tk = min(2048, k2)
    return pl.pallas_call(
        _dft_seq_kernel,
        grid=(l // tm, k2 // tk),
        in_specs=[pl.BlockSpec((tm, tk), lambda i, kk: (i, kk)),
                  pl.BlockSpec((tk, n), lambda i, kk: (kk, 0))],
        out_specs=pl.BlockSpec((tm, n), lambda i, kk: (i, 0)),
        out_shape=jax.ShapeDtypeStruct((l, n), F32),
        scratch_shapes=[pltpu.VMEM((tm, n), F32)],
        compiler_params=_cparams(("parallel", "arbitrary")),
        name="dft_seq",
    )(wtab, uu)


def _seq_table(l):
    f = math.gcd(l, 128)

    def thin(n, stride):
        p = lax.broadcasted_iota(jnp.int32, (l, n), 0)
        q = lax.broadcasted_iota(jnp.int32, (l, n), 1) * stride
        ang = ((p * q) % l).astype(F32) * (2.0 * math.pi / l)
        return jnp.cos(ang), jnp.sin(ang)

    c1, s1 = (t[:, :, None] for t in thin(l // f, f))
    c2, s2 = (t[:, None, :] for t in thin(f, 1))
    return jnp.concatenate([(c1 * c2 - s1 * s2).reshape(l, l), (-(s1 * c2 + c1 * s2)).reshape(l, l)],
                           axis=1).astype(BF16)


def _channel_tables(l, d_f):
    cc = lax.broadcasted_iota(jnp.int32, (d_f, d_f), 0)
    qq = lax.broadcasted_iota(jnp.int32, (d_f, d_f), 1)
    a2 = (((cc % HEAD) * (qq % HEAD)) % HEAD).astype(F32) * (2.0 * math.pi / HEAD)
    same = (cc // HEAD) == (qq // HEAD)
    scale = 1.0 / math.sqrt(l * HEAD)
    return jnp.stack([jnp.where(same, jnp.cos(a2), 0.0), jnp.where(same, jnp.sin(a2), 0.0)]) * scale


def _dot3(a, b):
    ah = a.astype(BF16)
    al = (a - ah.astype(F32)).astype(BF16)
    bh = b.astype(BF16)
    bl = (b - bh.astype(F32)).astype(BF16)
    return _dot(ah, bh) + (_dot(ah, bl) + _dot(al, bh))


def _fft_stage1_kernel(u_ref, ct_ref, t2_ref, tw_ref, o_ref):
    u = u_ref[...]
    l2 = u.shape[0]
    z = jnp.concatenate([_dot3(u, ct_ref[0]), -_dot3(u, ct_ref[1])], axis=0)
    bm = _dot3(t2_ref[...], z)
    br, bi = bm[0:l2], bm[l2:]
    tw = tw_ref[...]
    cs, sn = tw[:, 0:1], tw[:, 1:2]
    o_ref[0] = br * cs + bi * sn
    o_ref[1] = bi * cs - br * sn


def _fft_stage2_kernel(t1_ref, b_ref, o_ref):
    o_ref[...] = _dot3(t1_ref[...], b_ref[...])


def _fft_tables(l):
    l1, l2 = FFT_L1, l // FFT_L1

    def cs(n, m, period):
        p = lax.broadcasted_iota(jnp.int32, (n, m), 0)
        q = lax.broadcasted_iota(jnp.int32, (n, m), 1)
        ang = ((p * q) % period).astype(F32) * (2.0 * math.pi / period)
        return jnp.cos(ang), jnp.sin(ang)

    c2, s2 = cs(l2, l2, l2)
    t2 = jnp.concatenate([jnp.concatenate([c2, s2], axis=1), jnp.concatenate([-s2, c2], axis=1)], axis=0)
    tw = jnp.stack(cs(l1, l2, l), axis=-1)
    c1, s1 = cs(l1, l1, l1)
    return t2, tw, jnp.concatenate([c1, s1], axis=1)


def _fourier_fft(px, ctab, tabs, ft_blk, d_f):
    b, l, n = px.shape
    l1, l2 = FFT_L1, l // FFT_L1
    t2, tw, t1 = tabs
    stage1 = pl.pallas_call(
        _fft_stage1_kernel,
        grid=(b, l1),
        in_specs=[pl.BlockSpec((None, l2, d_f), lambda bi, i: (bi, 0, i * (n // d_f) + ft_blk)),
                  pl.BlockSpec(ctab.shape, lambda bi, i: (0, 0, 0)),
                  pl.BlockSpec(t2.shape, lambda bi, i: (0, 0)),
                  pl.BlockSpec((None, l2, 2), lambda bi, i: (i, 0, 0))],
        out_specs=pl.BlockSpec((None, 2, None, l2, d_f), lambda bi, i: (bi, 0, i, 0, 0)),
        out_shape=jax.ShapeDtypeStruct((b, 2, l1, l2, d_f), F32),
        compiler_params=_cparams(("parallel", "parallel")),
        name="fft_stage1",
    )(px.reshape(b, l2, l1 * n), ctab, t2, tw)
    tn = 8192
    cols = l2 * d_f
    y = pl.pallas_call(
        _fft_stage2_kernel,
        grid=(b, cols // tn),
        in_specs=[pl.BlockSpec(t1.shape, lambda bi, j: (0, 0)),
                  pl.BlockSpec((None, 2 * l1, tn), lambda bi, j: (bi, 0, j))],
        out_specs=pl.BlockSpec((None, l1, tn), lambda bi, j: (bi, 0, j)),
        out_shape=jax.ShapeDtypeStruct((b, l1, cols), F32),
        compiler_params=_cparams(("parallel", "parallel")),
        name="fft_stage2",
    )(t1, stage1.reshape(b, 2 * l1, cols))
    return y.reshape(b, l, d_f)


def _fourier_dense(px, wtab, ctab, ft_blk, d_f):
    b, l, _ = px.shape
    y = _dft_seq(wtab, _dft_channels(px, ctab, ft_blk, d_f))
    return y.reshape(l, b, d_f).transpose(1, 0, 2)


def _mix_out_kernel(period, yf_ref, yb_ref, z_ref, gate_ref, cg_ref, cx_ref, cb_ref, ft_ref, x_ref,
                    ga_ref, lnw_ref, lnb_ref, cw_ref, w_ref, o_ref, mix_ref):
    d_rwkv = yf_ref.shape[1]
    d_conv = cg_ref.shape[1]
    tm = yf_ref.shape[0]
    r = lax.broadcasted_iota(jnp.int32, (GW, GW), 0)
    c = lax.broadcasted_iota(jnp.int32, (GW, GW), 1)
    ones_bd = jnp.where((r // HEAD) == (c // HEAD), 1.0, 0.0).astype(BF16)
    inv_n = 1.0 / HEAD
    for s in range(d_rwkv // GW):
        sl = slice(s * GW, (s + 1) * GW)
        y = yf_ref[:, sl] + yb_ref[:, sl]
        mu = _dot_lhs_f32(y, ones_bd) * inv_n
        dlt = y - mu
        var = _dot_lhs_f32(dlt * dlt, ones_bd) * inv_n
        yn = dlt * lax.rsqrt(var + GN_EPS) * lnw_ref[:, sl] + lnb_ref[:, sl] + z_ref[:, sl]
        mix_ref[:, sl] = (yn * gate_ref[:, sl]).astype(BF16)
    u = cg_ref[...] * cx_ref[...]
    rowid = lax.broadcasted_iota(jnp.int32, u.shape, 0) % period
    up = jnp.where(rowid == 0, 0.0, pltpu.roll(u, 1, 0))
    dn = jnp.where(rowid == period - 1, 0.0, pltpu.roll(u, tm - 1, 0))
    cw = cw_ref[...]
    conv = cb_ref[...] * (up * cw[0:1] + u * cw[1:2] + dn * cw[2:3])
    mix_ref[:, d_rwkv:d_rwkv + d_conv] = conv.astype(BF16)
    mix_ref[:, d_rwkv + d_conv:] = ft_ref[...].astype(BF16)
    o_ref[...] = x_ref[...] + ga_ref[...] * jnp.dot(mix_ref[...], w_ref[...], preferred_element_type=F32)


def _mix_out(yf, yb, z, gate, px, ft, x, ga, lw, period, conv_blk):
    b, l, d = x.shape
    d_rwkv = yf.shape[2]
    d_conv = lw["conv_w"].shape[1]
    d_f = ft.shape[2]
    tm = min(256, l)
    assert tm % period == 0
    yspec = pl.BlockSpec((None, tm, d_rwkv), lambda bi, i: (bi, i, 0))
    cspec = lambda off: pl.BlockSpec((None, tm, d_conv), lambda bi, i: (bi, i, conv_blk + off))
    return pl.pallas_call(
        functools.partial(_mix_out_kernel, period),
        grid=(b, l // tm),
        in_specs=[yspec, yspec, yspec, yspec, cspec(0), cspec(1), cspec(2),
                  pl.BlockSpec((None, tm, d_f), lambda bi, i: (bi, i, 0)),
                  pl.BlockSpec((None, tm, d), lambda bi, i: (bi, i, 0)),
                  pl.BlockSpec((None, 1, d), lambda bi, i: (bi, 0, 0)),
                  pl.BlockSpec((1, d_rwkv), lambda bi, i: (0, 0)),
                  pl.BlockSpec((1, d_rwkv), lambda bi, i: (0, 0)),
                  pl.BlockSpec((3, d_conv), lambda bi, i: (0, 0)),
                  pl.BlockSpec(lw["w_out"].shape, lambda bi, i: (0, 0))],
        out_specs=pl.BlockSpec((None, tm, d), lambda bi, i: (bi, i, 0)),
        out_shape=jax.ShapeDtypeStruct((b, l, d), F32),
        scratch_shapes=[pltpu.VMEM((tm, lw["w_out"].shape[0]), BF16)],
        compiler_params=_cparams(("parallel", "parallel")),
        name="mix_out",
    )(yf, yb, z, gate, px, px, px, ft, x, ga, lw["ln_w"], lw["ln_b"], lw["conv_w"], lw["w_out"])


def _ffn_kernel(final, x_ref, g_ref, sh_ref, sc_ref, ga_ref, wg_ref, wu_ref, wd_ref, gf_ref, o_ref,
                h_ref, acc_ref):
    j = pl.program_id(2)

    @pl.when(j == 0)
    def _():
        h_ref[...] = _rms_mod(x_ref[...], g_ref[...], sh_ref[...], sc_ref[...]).astype(BF16)
        acc_ref[...] = jnp.zeros_like(acc_ref)

    h = h_ref[...]
    a = jnp.dot(h, wg_ref[...], preferred_element_type=F32)
    u = jnp.dot(h, wu_ref[...], preferred_element_type=F32)
    t = (a * jax.nn.sigmoid(a)) * u
    acc_ref[...] += jnp.dot(t.astype(BF16), wd_ref[...], preferred_element_type=F32)

    @pl.when(j == pl.num_programs(2) - 1)
    def _():
        xn = x_ref[...] + ga_ref[...] * acc_ref[...]
        if final:
            ms = jnp.mean(xn * xn, axis=-1, keepdims=True)
            xn = xn * lax.rsqrt(ms + RMS_EPS) * gf_ref[...]
        o_ref[...] = xn


def _ffn(x, g, shift, scale, ga, wg, wu, wd, g_final, final):
    b, l, d = x.shape
    ff = wg.shape[1]
    tm = min(512, l)
    tf = 512
    vec = pl.BlockSpec((None, 1, d), lambda bi, i, j: (bi, 0, 0))
    gspec = pl.BlockSpec((1, d), lambda bi, i, j: (0, 0))
    return pl.pallas_call(
        functools.partial(_ffn_kernel, final),
        grid=(b, l // tm, ff // tf),
        in_specs=[pl.BlockSpec((None, tm, d), lambda bi, i, j: (bi, i, 0)),
                  gspec, vec, vec, vec,
                  pl.BlockSpec((d, tf), lambda bi, i, j: (0, j)),
                  pl.BlockSpec((d, tf), lambda bi, i, j: (0, j)),
                  pl.BlockSpec((tf, d), lambda bi, i, j: (j, 0)),
                  gspec],
        out_specs=pl.BlockSpec((None, tm, d), lambda bi, i, j: (bi, i, 0)),
        out_shape=jax.ShapeDtypeStruct((b, l, d), F32),
        scratch_shapes=[pltpu.VMEM((tm, d), BF16), pltpu.VMEM((tm, d), F32)],
        compiler_params=_cparams(("parallel", "parallel", "arbitrary")),
        name="ffn",
    )(x, g, shift, scale, ga, wg, wu, wd, g_final)


def _layer_weights(i, w_in, rw_shift, dec_w0, dec_up, iclr_a0, iclr_up, k_k, k_a, r_k, ln_w, ln_b, g_up,
                   conv_w, w_out, w_gate, w_up, w_down, dims):
    d_rwkv, d_conv, d_f, dl, il, gl = dims
    ng = d_rwkv // GW
    wi = w_in[i]
    sh = rw_shift[i]
    r0, k0, v0 = 0, d_rwkv, 2 * d_rwkv
    wd0 = 3 * d_rwkv
    ad0 = wd0 + dl
    gd0 = ad0 + il
    rest0 = gd0 + gl

    def regroup(m):
        cols = []
        for g in range(ng):
            for base in (r0, k0, v0):
                cols.append(m[:, base + g * GW: base + (g + 1) * GW])
        padl = jnp.zeros((m.shape[0], LORA_PAD - dl), m.dtype)
        padi = jnp.zeros((m.shape[0], LORA_PAD - il), m.dtype)
        cols += [m[:, wd0:ad0], padl, m[:, ad0:gd0], padi, m[:, gd0:rest0]]
        return jnp.concatenate(cols, axis=1)

    w_rw = regroup(wi)
    sh_all = regroup(sh)
    pad_rows = lambda m, n: jnp.concatenate([m, jnp.zeros(m.shape[:-2] + (n - m.shape[-2], m.shape[-1]), m.dtype)], -2)
    zero = jnp.zeros((d_rwkv,), F32)
    pvec = jnp.stack([k_k[i], k_a[i], r_k[i].reshape(-1), dec_w0[i, 0], dec_w0[i, 1],
                      iclr_a0[i, 0], iclr_a0[i, 1], zero])
    return dict(
        ng=ng,
        w_in=jnp.concatenate([w_rw, wi[:, rest0:]], axis=1).astype(BF16),
        mix=sh_all,
        pvec=pvec,
        dec_up=pad_rows(dec_up[i], LORA_PAD).astype(BF16),
        iclr_up=pad_rows(iclr_up[i], LORA_PAD).astype(BF16),
        g_up=g_up[i].astype(BF16),
        ln_w=ln_w[i][None, :], ln_b=ln_b[i][None, :],
        conv_w=conv_w[i],
        w_out=w_out[i].astype(BF16),
        w_gate=w_gate[i].astype(BF16), w_up=w_up[i].astype(BF16), w_down=w_down[i].astype(BF16),
    )


def kernel(x, c, ctx, c_ctx, w_mod, b_mod, norm_mix, w_in, rw_shift, dec_w0, dec_up, iclr_a0, iclr_up,
           k_k, k_a, r_k, ln_w, ln_b, g_up, conv_w, w_out, norm_ffn, w_gate, w_up, w_down, norm_final):
    b, l, d = x.shape
    lc = ctx.shape[1]
    depth = w_mod.shape[0]
    d_rwkv = k_k.shape[1]
    d_conv = conv_w.shape[2]
    dl, il, gl = dec_up.shape[2], iclr_up.shape[2], g_up.shape[1]
    d_f = w_out.shape[1] - d_rwkv - d_conv
    assert dl <= LORA_PAD and il <= LORA_PAD and gl == 2 * LORA_PAD and d_rwkv % GW == 0
    assert l % GRID_W == 0 and l % CHUNK == 0 and lc % CHUNK == 0
    nsub_x = math.gcd(l // CHUNK, WKV_SUB)
    nsub_c = math.gcd(lc // CHUNK, WKV_SUB)
    ng = d_rwkv // GW
    n_rw = 3 * d_rwkv + 4 * LORA_PAD
    conv_blk = n_rw // d_conv
    ft_blk = (n_rw + 3 * d_conv) // d_f
    assert conv_blk * d_conv == n_rw and ft_blk * d_f == n_rw + 3 * d_conv

    cvec = jnp.concatenate([c, c_ctx[None, :], jnp.zeros((8 - b - 1, d), F32)], axis=0)
    mods = _adaln(cvec, w_mod, b_mod[:, None, :])

    def mod(i, j, ctx_rows):
        m = mods[i, :, j * d:(j + 1) * d]
        if ctx_rows:
            return jnp.broadcast_to(m[b:b + 1], (b, d))[:, None, :]
        return m[:b][:, None, :]

    fft_x = l % (FFT_L1 * 128) == 0
    wtab_c, ctab_c = _seq_table(lc), _channel_tables(lc, d_f)
    ctab_x = _channel_tables(l, d_f)
    if fft_x:
        ftab_x = _fft_tables(l)
    else:
        wtab_x = _seq_table(l)
    dims = (d_rwkv, d_conv, d_f, dl, il, gl)
    s_zero = jnp.zeros((b, 2, ng * N_PAIR, PW, PW), F32)
    gfin = norm_final[None, :]
    xc = ctx
    for i in range(depth):
        lw = _layer_weights(i, w_in, rw_shift, dec_w0, dec_up, iclr_a0, iclr_up, k_k, k_a, r_k, ln_w, ln_b,
                            g_up, conv_w, w_out, w_gate, w_up, w_down, dims)
        gmix = norm_mix[i][None, :]
        gffn = norm_ffn[i][None, :]
        last = i == depth - 1
        pc = _in_proj(xc, gmix, mod(i, 0, True), mod(i, 1, True), lw["w_in"], lw["mix"])
        px = _in_proj(x, gmix, mod(i, 0, False), mod(i, 1, False), lw["w_in"], lw["mix"])
        yfc, ybc, zc, gc, s_ctx = _wkv(pc, lw, s_zero, nsub_c)
        yfx, ybx, zx, gx, _ = _wkv(px, lw, s_ctx, nsub_x)
        if fft_x:
            ftx = _fourier_fft(px, ctab_x, ftab_x, ft_blk, d_f)
        else:
            ftx = _fourier_dense(px, wtab_x, ctab_x, ft_blk, d_f)
        x = _mix_out(yfx, ybx, zx, gx, px, ftx, x, mod(i, 2, False), lw, GRID_W, conv_blk)
        x = _ffn(x, gffn, mod(i, 3, False), mod(i, 4, False), mod(i, 5, False),
                 lw["w_gate"], lw["w_up"], lw["w_down"], gfin, last)
        if not last:
            ftc = _fourier_dense(pc, wtab_c, ctab_c, ft_blk, d_f)
            xc = _mix_out(yfc, ybc, zc, gc, pc, ftc, xc, mod(i, 2, True), lw, lc, conv_blk)
            xc = _ffn(xc, gffn, mod(i, 3, True), mod(i, 4, True), mod(i, 5, True),
                      lw["w_gate"], lw["w_up"], lw["w_down"], gfin, False)
    return x
```

```python
import functools
import math

import jax
import jax.numpy as jnp
from jax import lax
from jax.experimental import pallas as pl
from jax.experimental.pallas import tpu as pltpu

F32 = jnp.float32
BF16 = jnp.bfloat16

HEAD = 64
GROUP_HEADS = 4
GW = GROUP_HEADS * HEAD
PAIR_HEADS = 2
PW = PAIR_HEADS * HEAD
N_PAIR = GW // PW
CHUNK = 64
WKV_SUB = 4
HALO = 16
LANES = 128
LORA_PAD = 128
GRID_W = 64
FFT_L1 = 64
N_MOD = 6
RMS_EPS = 1e-6
GN_EPS = 64e-5
KK_EPS = 1e-12
VMEM_LIMIT = 56 * 1024 * 1024


def _cparams(sem):
    return pltpu.CompilerParams(dimension_semantics=sem, vmem_limit_bytes=VMEM_LIMIT)


def _dot(a, b, nt=False):
    dn = (((1,), (1,)), ((), ())) if nt else (((1,), (0,)), ((), ()))
    return lax.dot_general(a.astype(BF16), b.astype(BF16), dn, preferred_element_type=F32)


def _split2(a):
    hi = a.astype(BF16)
    return hi, (a - hi.astype(F32)).astype(BF16)


def _dot_lhs_f32(a, b_exact):
    hi, lo = _split2(a)
    return _dot(hi, b_exact) + _dot(lo, b_exact)


def _dot_rhs_f32(a_exact, b):
    hi, lo = _split2(b)
    return _dot(a_exact, hi) + _dot(a_exact, lo)


def _rms_mod(x, g, shift, scale):
    ms = jnp.mean(x * x, axis=-1, keepdims=True)
    return (x * lax.rsqrt(ms + RMS_EPS) * g) * (1.0 + scale) + shift


def _adaln_kernel(c_ref, w_ref, b_ref, o_ref):
    c = c_ref[...]
    s = c * jax.nn.sigmoid(c)
    o_ref[...] = _dot(s, w_ref[...]) + b_ref[...]


def _adaln(cvec, w_mod, b_mod):
    nl, d, n = w_mod.shape
    tn = 1024
    return pl.pallas_call(
        _adaln_kernel,
        grid=(nl, n // tn),
        in_specs=[pl.BlockSpec((8, d), lambda l, j: (0, 0)),
                  pl.BlockSpec((None, d, tn), lambda l, j: (l, 0, j)),
                  pl.BlockSpec((None, 1, tn), lambda l, j: (l, 0, j))],
        out_specs=pl.BlockSpec((None, 8, tn), lambda l, j: (l, 0, j)),
        out_shape=jax.ShapeDtypeStruct((nl, 8, n), F32),
        compiler_params=_cparams(("parallel", "parallel")),
        name="adaln",
    )(cvec, w_mod, b_mod)


def _in_proj_kernel(n_mix_tiles, x_ref, xp_ref, xn_ref, g_ref, sh_ref, sc_ref, w_ref, mix_ref, o_ref, h_ref):
    i = pl.program_id(1)
    tm = x_ref.shape[0]

    @pl.when(pl.program_id(2) == 0)
    def _():
        g, sh, sc = g_ref[...], sh_ref[...], sc_ref[...]
        prev = jnp.where(i > 0, _rms_mod(xp_ref[...], g, sh, sc), 0.0)
        nxt = jnp.where(i < pl.num_programs(1) - 1, _rms_mod(xn_ref[...], g, sh, sc), 0.0)
        h_ref[0:HALO, :] = prev.astype(BF16)
        h_ref[HALO:HALO + tm, :] = _rms_mod(x_ref[...], g, sh, sc).astype(BF16)
        h_ref[HALO + tm:, :] = nxt.astype(BF16)

    @pl.when(pl.program_id(2) < n_mix_tiles)
    def _():
        p = jnp.dot(h_ref[...], w_ref[...], preferred_element_type=F32)
        mix = mix_ref[...]
        up = pltpu.roll(p, 1, 0)[HALO:HALO + tm]
        dn = pltpu.roll(p, tm + 2 * HALO - 1, 0)[HALO:HALO + tm]
        o_ref[...] = up * mix[0:1, :] + p[HALO:HALO + tm] * mix[1:2, :] + dn * mix[2:3, :]

    @pl.when(pl.program_id(2) >= n_mix_tiles)
    def _():
        o_ref[...] = jnp.dot(h_ref[HALO:HALO + tm, :], w_ref[...], preferred_element_type=F32)


def _in_proj(x, g, shift, scale, w, mix):
    b, l, d = x.shape
    n = w.shape[1]
    tm = min(1024, l)
    tn = 512
    n_mix_tiles = mix.shape[1] // tn
    assert n_mix_tiles * tn == mix.shape[1]
    hb = tm // HALO
    nhb = l // HALO
    vec = pl.BlockSpec((None, 1, d), lambda bi, i, j: (bi, 0, 0))
    return pl.pallas_call(
        functools.partial(_in_proj_kernel, n_mix_tiles),
        grid=(b, l // tm, n // tn),
        in_specs=[pl.BlockSpec((None, tm, d), lambda bi, i, j: (bi, i, 0)),
                  pl.BlockSpec((None, HALO, d), lambda bi, i, j: (bi, jnp.maximum(i * hb - 1, 0), 0)),
                  pl.BlockSpec((None, HALO, d), lambda bi, i, j: (bi, jnp.minimum((i + 1) * hb, nhb - 1), 0)),
                  pl.BlockSpec((1, d), lambda bi, i, j: (0, 0)),
                  vec, vec,
                  pl.BlockSpec((d, tn), lambda bi, i, j: (0, j)),
                  pl.BlockSpec((3, tn), lambda bi, i, j: (0, jnp.minimum(j, n_mix_tiles - 1)))],
        out_specs=pl.BlockSpec((None, tm, tn), lambda bi, i, j: (bi, i, j)),
        out_shape=jax.ShapeDtypeStruct((b, l, n), F32),
        scratch_shapes=[pltpu.VMEM((tm + 2 * HALO, d), BF16)],
        compiler_params=_cparams(("parallel", "parallel", "arbitrary")),
        name="in_proj",
    )(x, x, x, g, shift, scale, w, mix)


def _block_masks(width):
    r = lax.broadcasted_iota(jnp.int32, (width, width), 0)
    c = lax.broadcasted_iota(jnp.int32, (width, width), 1)
    return (r // HEAD) == (c // HEAD), r % HEAD, c % HEAD, r == c


def _rs(x, bd):
    return jnp.where(bd, jnp.concatenate([x] * PAIR_HEADS, axis=0), 0.0)


def _ls(x_rs):
    out = x_rs[0:CHUNK]
    for h in range(1, PAIR_HEADS):
        out = out + x_rs[h * CHUNK:(h + 1) * CHUNK]
    return out


def _each(f, *lists):
    return [f(*a) for a in zip(*lists)]


def _chunk_affine(chunks, masks):
    bd, tt, ss, eye = masks
    row = lax.broadcasted_iota(jnp.int32, (CHUNK, CHUNK), 0)
    col = lax.broadcasted_iota(jnp.int32, (CHUNK, CHUNK), 1)
    tri = {rev: jnp.where((row <= col) if rev else (row >= col), 1.0, 0.0).astype(BF16) for rev in (False, True)}
    strict = {False: bd & (ss < tt), True: bd & (ss > tt)}
    r2 = lax.broadcasted_iota(jnp.int32, (PW, 2 * PW), 0)
    c2 = lax.broadcasted_iota(jnp.int32, (PW, 2 * PW), 1)
    bd2 = (r2 // HEAD) == ((c2 % PW) // HEAD)
    incl2 = {False: bd2 & (c2 % HEAD <= r2 % HEAD), True: bd2 & (c2 % HEAD >= r2 % HEAD)}

    crev = [ch[6] for ch in chunks]
    logw = [ch[5] for ch in chunks]
    cum = _each(lambda rv, lw: _dot_rhs_f32(tri[rv], lw), crev, logw)
    total = _each(lambda rv, cm: cm[0:1] if rv else cm[CHUNK - 1:CHUNK], crev, cum)
    p_inv = _each(lambda cm: jnp.exp(-cm), cum)
    p_end = _each(lambda tot, cm: jnp.exp(tot - cm), total, cum)
    bvec = [ch[3] * ch[4] for ch in chunks]
    wide = dict(
        a=_each(lambda ch, cm, lw: -ch[3] * jnp.exp(cm - lw), chunks, cum, logw),
        r=_each(lambda ch, cm: ch[0] * jnp.exp(cm), chunks, cum),
        b=_each(lambda b, p: b * p, bvec, p_inv),
        k=_each(lambda ch, p: ch[1] * p, chunks, p_inv),
        v=[ch[2] for ch in chunks],
        be=_each(lambda b, p: b * p, bvec, p_end),
        ke=_each(lambda ch, p: ch[1] * p, chunks, p_end),
        pc=_each(jnp.exp, total),
    )

    def pairs(name):
        return [x[:, p * PW:(p + 1) * PW] for x in wide[name] for p in range(N_PAIR)]

    rev = [rv for rv in crev for _ in range(N_PAIR)]
    r_t = pairs("r")
    a_rs = _each(lambda x: _rs(x, bd).astype(BF16), pairs("a"))
    r_rs = _each(lambda x: _rs(x, bd).astype(BF16), r_t)
    b_rs = _each(lambda x: jnp.concatenate([x.astype(BF16)] * PAIR_HEADS, axis=0), pairs("b"))
    k_rs = _each(lambda x: jnp.concatenate([x.astype(BF16)] * PAIR_HEADS, axis=0), pairs("k"))
    v_rs = _each(lambda x: _rs(x, bd).astype(BF16), pairs("v"))
    aa = _each(lambda a, r, b, k: _dot(jnp.concatenate([a, r], axis=0), jnp.concatenate([b, k], axis=0), nt=True),
               a_rs, r_rs, b_rs, k_rs)
    a_ab = _each(lambda rv, x: jnp.where(strict[rv], x[0:PW, 0:PW], 0.0), rev, aa)
    a_ak = _each(lambda rv, x: jnp.where(strict[rv], x[0:PW, PW:], 0.0), rev, aa)
    a_r = _each(lambda rv, x: jnp.where(incl2[rv], x[PW:, :], 0.0).astype(BF16), rev, aa)
    t_inv = _each(lambda a: jnp.where(eye, 1.0, a), a_ab)
    pw = _each(lambda a: _dot(a, a), a_ab)
    for _ in range(int(math.log2(CHUNK)) - 2):
        res = _each(lambda p, t: _dot(p, jnp.concatenate([p, t], axis=1)), pw, t_inv)
        pw = [x[:, 0:PW] for x in res]
        t_inv = _each(lambda t, x: t + x[:, PW:], t_inv, res)
    t_inv = _each(lambda p, t: (t + _dot(p, t)).astype(BF16), pw, t_inv)
    akv = _each(_dot, a_ak, v_rs)
    tw = _each(lambda t, a, x: _dot(t, jnp.concatenate([a, x.astype(BF16)], axis=1)).astype(BF16),
               t_inv, a_rs, akv)
    bk_t = _each(lambda b, k: jnp.concatenate([_rs(b, bd), _rs(k, bd)], axis=0).T.astype(BF16),
                 pairs("be"), pairs("ke"))
    fin = _each(lambda ar, bk, w, v: _dot(jnp.concatenate([ar, bk], axis=0),
                                          jnp.concatenate([w, jnp.concatenate([jnp.zeros_like(v), v], axis=1)],
                                                          axis=0)),
                a_r, bk_t, tw, v_rs)
    rh = _each(lambda r, f: r + _ls(f[0:PW, 0:PW]), r_t, fin)
    y0 = _each(lambda f: _ls(f[0:PW, PW:]), fin)
    m = _each(lambda pc, f: jnp.where(eye, pc, 0.0) + f[PW:, 0:PW], pairs("pc"), fin)
    nn = [f[PW:, PW:] for f in fin]
    out = list(zip(y0, rh, m, nn))
    return [out[i * N_PAIR:(i + 1) * N_PAIR] for i in range(len(chunks))]


def _head_sum(x, ones_bd):
    return _dot_lhs_f32(x, ones_bd)


def _wkv_kernel(nsub, rkv_f, lo_f, rkv_b, lo_b, pvec, dec_up, iclr_up, g_up, s0,
                yf_ref, yb_ref, z_ref, gate_ref, sfin_ref,
                zst):
    c = pl.program_id(2)
    nsteps = pl.num_programs(2)

    @pl.when(c == 0)
    def _():
        zst[...] = s0[...]

    masks = _block_masks(PW)
    ones_bd = jnp.where(_block_masks(GW)[0], 1.0, 0.0).astype(BF16)
    pv = pvec[...]
    k_k, k_a, r_k = pv[0:1], pv[1:2], pv[2:3]
    w0 = (pv[3:4], pv[4:5])
    a0 = (pv[5:6], pv[6:7])
    sig_scale = math.exp(-0.5)

    def streams(rkv_ref, lo_ref, d):
        rkv = rkv_ref[...]
        lo = lo_ref[...]
        r, k, v = rkv[:, 0:GW], rkv[:, GW:2 * GW], rkv[:, 2 * GW:3 * GW]
        wd = jnp.tanh(lo[:, 0:LORA_PAD])
        ad = lo[:, LORA_PAD:2 * LORA_PAD]
        gd = lo[:, 2 * LORA_PAD:]
        kk = k * k_k
        kkn = kk * lax.rsqrt(jnp.maximum(_head_sum(kk * kk, ones_bd), KK_EPS * KK_EPS))
        logw = -sig_scale * jax.nn.sigmoid(w0[d] + _dot(wd, dec_up[d]))
        aic = jax.nn.sigmoid(a0[d] + _dot(ad, iclr_up[d]))
        kd = k * (1.0 + (aic - 1.0) * k_a)
        return r, k, v, kkn, logw, aic, kd, ad, gd

    r, k, v, kkn, logw, aic, kd, ad, gd = streams(rkv_f, lo_f, 0)
    aic_o = jax.nn.sigmoid(a0[1] + _dot(ad, iclr_up[1]))
    kd_o = k * (1.0 + (aic_o - 1.0) * k_a)
    z_ref[...] = _head_sum(r * (kd + kd_o) * r_k, ones_bd) * v
    gate_ref[...] = _dot(jax.nn.sigmoid(gd), g_up[...])
    chains = [tuple(t[j * CHUNK:(j + 1) * CHUNK] for t in (r, kd, v, kkn, aic, logw)) + (False,)
              for j in range(nsub)]
    r, k, v, kkn, logw, aic, kd, ad, gd = streams(rkv_b, lo_b, 1)
    chains += [tuple(t[j * CHUNK:(j + 1) * CHUNK] for t in (r, kd, v, kkn, aic, logw)) + (True,)
               for j in range(nsub)]
    affine = _chunk_affine(chains, masks)
    fwd, bwd = affine[:nsub], affine[nsub:]

    for d, (steps, y_ref) in enumerate(((list(range(nsub)), yf_ref), (list(reversed(range(nsub))), yb_ref))):
        affine_d = fwd if d == 0 else bwd
        z = [zst[d, p] for p in range(N_PAIR)]
        for j in steps:
            mz = [_dot(jnp.concatenate([affine_d[j][p][2], affine_d[j][p][1]], axis=0), z[p])
                  for p in range(N_PAIR)]
            y_ref[j * CHUNK:(j + 1) * CHUNK, :] = jnp.concatenate(
                [affine_d[j][p][0] + mz[p][PW:] for p in range(N_PAIR)], axis=1)
            z = [mz[p][0:PW] + affine_d[j][p][3] for p in range(N_PAIR)]
        for p in range(N_PAIR):
            zst[d, p] = z[p]

    @pl.when(c == nsteps - 1)
    def _():
        sfin_ref[...] = zst[...]


def _wkv(px, lw, s0, nsub):
    b, l, _ = px.shape
    ng = lw["ng"]
    d_rwkv = ng * GW
    t = nsub * CHUNK
    nsteps = l // t
    lo_blk = 3 * d_rwkv // (4 * LORA_PAD)
    rkv_w = 3 * GW
    lo_w = 4 * LORA_PAD

    def main_specs(blk):
        return [pl.BlockSpec((None, t, rkv_w), lambda bi, g, c: (bi, blk(c), g)),
                pl.BlockSpec((None, t, lo_w), lambda bi, g, c: (bi, blk(c), lo_blk))]

    fblk = lambda c: c
    bblk = lambda c: nsteps - 1 - c
    in_specs = (main_specs(fblk) + main_specs(bblk) + [
        pl.BlockSpec((8, GW), lambda bi, g, c: (0, g)),
        pl.BlockSpec((2, LORA_PAD, GW), lambda bi, g, c: (0, 0, g)),
        pl.BlockSpec((2, LORA_PAD, GW), lambda bi, g, c: (0, 0, g)),
        pl.BlockSpec((2 * LORA_PAD, GW), lambda bi, g, c: (0, g)),
        pl.BlockSpec((None, 2, N_PAIR, PW, PW), lambda bi, g, c: (bi, 0, g, 0, 0)),
    ])
    yspec = lambda blk: pl.BlockSpec((None, t, GW), lambda bi, g, c: (bi, blk(c), g))
    out_specs = [yspec(fblk), yspec(bblk), yspec(fblk), yspec(fblk),
                 pl.BlockSpec((None, 2, N_PAIR, PW, PW), lambda bi, g, c: (bi, 0, g, 0, 0))]
    ysh = jax.ShapeDtypeStruct((b, l, d_rwkv), F32)
    return pl.pallas_call(
        functools.partial(_wkv_kernel, nsub),
        grid=(b, ng, nsteps),
        in_specs=in_specs,
        out_specs=out_specs,
        out_shape=[ysh, ysh, ysh, ysh, jax.ShapeDtypeStruct(s0.shape, F32)],
        scratch_shapes=[pltpu.VMEM((2, N_PAIR, PW, PW), F32)],
        compiler_params=_cparams(("parallel", "parallel", "arbitrary")),
        name="wkv",
    )(px, px, px, px, lw["pvec"], lw["dec_up"], lw["iclr_up"], lw["g_up"], s0)


def _dft_ch_kernel(u_ref, t_ref, o_ref):
    u = u_ref[...]
    hi = u.astype(BF16)
    lo = (u - hi.astype(F32)).astype(BF16)
    tab = t_ref[...]
    o_ref[...] = (_dot(hi, tab) + _dot(lo, tab)).astype(BF16)


def _dft_channels(px, tabs, ft_blk, d_f):
    b, l, _ = px.shape
    tm = min(512, l)
    nt = l // tm
    return pl.pallas_call(
        _dft_ch_kernel,
        grid=(b, 2, nt),
        in_specs=[pl.BlockSpec((None, tm, d_f), lambda bi, s, i: (bi, i, ft_blk)),
                  pl.BlockSpec((None, d_f, d_f), lambda bi, s, i: (s, 0, 0))],
        out_specs=pl.BlockSpec((tm, d_f), lambda bi, s, i: (s * nt + i, bi)),
        out_shape=jax.ShapeDtypeStruct((2 * l, b * d_f), BF16),
        compiler_params=_cparams(("parallel", "parallel", "parallel")),
        name="dft_channels",
    )(px, tabs)


def _dft_seq_kernel(w_ref, u_ref, o_ref, acc_ref):
    kk = pl.program_id(1)

    @pl.when(kk == 0)
    def _():
        acc_ref[...] = jnp.zeros_like(acc_ref)

    acc_ref[...] += jnp.dot(w_ref[...], u_ref[...], preferred_element_type=F32)

    @pl.when(kk == pl.num_programs(1) - 1)
    def _():
        o_ref[...] = acc_ref[...]


def _dft_seq(wtab, uu):
    l, k2 = wtab.shape
    n = uu.shape[1]
    tm = min(1024, l)
    tk = min(2048, k2)
    return pl.pallas_call(
        _dft_seq_kernel,
        grid=(l // tm, k2 // tk),
        in_specs=[pl.BlockSpec((tm, tk), lambda i, kk: (i, kk)),
                  pl.BlockSpec((tk, n), lambda i, kk: (kk, 0))],
        out_specs=pl.BlockSpec((tm, n), lambda i, kk: (i, 0)),
        out_shape=jax.ShapeDtypeStruct((l, n), F32),
        scratch_shapes=[pltpu.VMEM((tm, n), F32)],
        compiler_params=_cparams(("parallel", "arbitrary")),
        name="dft_seq",
    )(wtab, uu)


def _seq_table(l):
    f = math.gcd(l, 128)

    def thin(n, stride):
        p = lax.broadcasted_iota(jnp.int32, (l, n), 0)
        q = lax.broadcasted_iota(jnp.int32, (l, n), 1) * stride
        ang = ((p * q) % l).astype(F32) * (2.0 * math.pi / l)
        return jnp.cos(ang), jnp.sin(ang)

    c1, s1 = (t[:, :, None] for t in thin(l // f, f))
    c2, s2 = (t[:, None, :] for t in thin(f, 1))
    return jnp.concatenate([(c1 * c2 - s1 * s2).reshape(l, l), (-(s1 * c2 + c1 * s2)).reshape(l, l)],
                           axis=1).astype(BF16)


def _channel_tables(l, d_f):
    cc = lax.broadcasted_iota(jnp.int32, (d_f, d_f), 0)
    qq = lax.broadcasted_iota(jnp.int32, (d_f, d_f), 1)
    a2 = (((cc % HEAD) * (qq % HEAD)) % HEAD).astype(F32) * (2.0 * math.pi / HEAD)
    same = (cc // HEAD) == (qq // HEAD)
    scale = 1.0 / math.sqrt(l * HEAD)
    return jnp.stack([jnp.where(same, jnp.cos(a2), 0.0), jnp.where(same, jnp.sin(a2), 0.0)]) * scale


def _dot3(a, b):
    ah = a.astype(BF16)
    al = (a - ah.astype(F32)).astype(BF16)
    bh = b.astype(BF16)
    bl = (b - bh.astype(F32)).astype(BF16)
    return _dot(ah, bh) + (_dot(ah, bl) + _dot(al, bh))


FFT_SUB = 8


def _fft_stage1_kernel(*refs):
    u_refs, (ct_ref, t2_ref, tw_ref, o_ref) = refs[:-4], refs[-4:]
    i = pl.program_id(1)
    l2 = t2_ref.shape[0] // 2
    for j in range(FFT_SUB):
        u = jnp.concatenate([r[pl.ds(i * FFT_SUB + j, l2, stride=FFT_L1), :] for r in u_refs], axis=1)
        z = jnp.concatenate([_dot3(u, ct_ref[0]), -_dot3(u, ct_ref[1])], axis=0)
        bm = _dot3(t2_ref[...], z)
        br, bi = bm[0:l2], bm[l2:]
        tw = tw_ref[j]
        cs, sn = tw[:, 0:1], tw[:, 1:2]
        comp = (br * cs + bi * sn, bi * cs - br * sn)
        for c in range(2):
            for pb in range(l2 // FFT_SUB):
                o_ref[pb, c, j * FFT_SUB:(j + 1) * FFT_SUB, :] = comp[c][pb * FFT_SUB:(pb + 1) * FFT_SUB]


def _fft_stage2_kernel(t1_ref, b_ref, o_ref):
    x = jnp.concatenate([b_ref[0], b_ref[1]], axis=0)
    y = _dot3(t1_ref[...], x)
    for p1 in range(o_ref.shape[0]):
        o_ref[p1] = y[p1 * FFT_SUB:(p1 + 1) * FFT_SUB]


def _fft_tables(l):
    l1, l2 = FFT_L1, l // FFT_L1

    def cs(n, m, period):
        p = lax.broadcasted_iota(jnp.int32, (n, m), 0)
        q = lax.broadcasted_iota(jnp.int32, (n, m), 1)
        ang = ((p * q) % period).astype(F32) * (2.0 * math.pi / period)
        return jnp.cos(ang), jnp.sin(ang)

    c2, s2 = cs(l2, l2, l2)
    t2 = jnp.concatenate([jnp.concatenate([c2, s2], axis=1), jnp.concatenate([-s2, c2], axis=1)], axis=0)
    tw = jnp.stack(cs(l1, l2, l), axis=-1)
    c1, s1 = cs(l1, l1, l1)
    t1 = jnp.kron(jnp.concatenate([c1, s1], axis=1), jnp.eye(FFT_SUB, dtype=F32))
    return t2, tw, t1


def _fourier_fft(px, ctab, tabs, ft_blk, d_f):
    b, l, n = px.shape
    l1, l2 = FFT_L1, l // FFT_L1
    npb = l2 // FFT_SUB
    nslab = d_f // LANES
    t2, tw, t1 = tabs
    stage1 = pl.pallas_call(
        _fft_stage1_kernel,
        grid=(b, l1 // FFT_SUB),
        in_specs=[pl.BlockSpec((None, l, LANES), lambda bi, i, k=k: (bi, 0, ft_blk * nslab + k))
                  for k in range(nslab)] + [
                  pl.BlockSpec(ctab.shape, lambda bi, i: (0, 0, 0)),
                  pl.BlockSpec(t2.shape, lambda bi, i: (0, 0)),
                  pl.BlockSpec((FFT_SUB, l2, 2), lambda bi, i: (i, 0, 0))],
        out_specs=pl.BlockSpec((None, npb, 2, FFT_SUB * FFT_SUB, d_f), lambda bi, i: (bi, 0, 0, i, 0)),
        out_shape=jax.ShapeDtypeStruct((b, npb, 2, l1 * FFT_SUB, d_f), F32),
        compiler_params=_cparams(("parallel", "arbitrary")),
        name="fft_stage1",
    )(*([px] * nslab), ctab, t2, tw)
    y = pl.pallas_call(
        _fft_stage2_kernel,
        grid=(b, npb),
        in_specs=[pl.BlockSpec(t1.shape, lambda bi, j: (0, 0)),
                  pl.BlockSpec((None, None, 2, l1 * FFT_SUB, d_f), lambda bi, j: (bi, j, 0, 0, 0))],
        out_specs=pl.BlockSpec((None, l1, None, FFT_SUB, d_f), lambda bi, j: (bi, 0, j, 0, 0)),
        out_shape=jax.ShapeDtypeStruct((b, l1, npb, FFT_SUB, d_f), F32),
        compiler_params=_cparams(("parallel", "parallel")),
        name="fft_stage2",
    )(t1, stage1)
    return y.reshape(b, l, d_f)


def _fourier_dense(px, wtab, ctab, ft_blk, d_f):
    b, l, _ = px.shape
    y = _dft_seq(wtab, _dft_channels(px, ctab, ft_blk, d_f))
    return y.reshape(l, b, d_f).transpose(1, 0, 2)


def _mix_out_kernel(period, yf_ref, yb_ref, z_ref, gate_ref, cg_ref, cx_ref, cb_ref, ft_ref, x_ref,
                    ga_ref, lnw_ref, lnb_ref, cw_ref, w_ref, o_ref, mix_ref):
    d_rwkv = yf_ref.shape[1]
    d_conv = cg_ref.shape[1]
    tm = yf_ref.shape[0]
    r = lax.broadcasted_iota(jnp.int32, (GW, GW), 0)
    c = lax.broadcasted_iota(jnp.int32, (GW, GW), 1)
    ones_bd = jnp.where((r // HEAD) == (c // HEAD), 1.0, 0.0).astype(BF16)
    inv_n = 1.0 / HEAD
    for s in range(d_rwkv // GW):
        sl = slice(s * GW, (s + 1) * GW)
        y = yf_ref[:, sl] + yb_ref[:, sl]
        mu = _dot_lhs_f32(y, ones_bd) * inv_n
        dlt = y - mu
        var = _dot_lhs_f32(dlt * dlt, ones_bd) * inv_n
        yn = dlt * lax.rsqrt(var + GN_EPS) * lnw_ref[:, sl] + lnb_ref[:, sl] + z_ref[:, sl]
        mix_ref[:, sl] = (yn * gate_ref[:, sl]).astype(BF16)
    u = cg_ref[...] * cx_ref[...]
    rowid = lax.broadcasted_iota(jnp.int32, u.shape, 0) % period
    up = jnp.where(rowid == 0, 0.0, pltpu.roll(u, 1, 0))
    dn = jnp.where(rowid == period - 1, 0.0, pltpu.roll(u, tm - 1, 0))
    cw = cw_ref[...]
    conv = cb_ref[...] * (up * cw[0:1] + u * cw[1:2] + dn * cw[2:3])
    mix_ref[:, d_rwkv:d_rwkv + d_conv] = conv.astype(BF16)
    mix_ref[:, d_rwkv + d_conv:] = ft_ref[...].astype(BF16)
    o_ref[...] = x_ref[...] + ga_ref[...] * jnp.dot(mix_ref[...], w_ref[...], preferred_element_type=F32)


def _mix_out(yf, yb, z, gate, px, ft, x, ga, lw, period, conv_blk):
    b, l, d = x.shape
    d_rwkv = yf.shape[2]
    d_conv = lw["conv_w"].shape[1]
    d_f = ft.shape[2]
    tm = min(256, l)
    assert tm % period == 0
    yspec = pl.BlockSpec((None, tm, d_rwkv), lambda bi, i: (bi, i, 0))
    cspec = lambda off: pl.BlockSpec((None, tm, d_conv), lambda bi, i: (bi, i, conv_blk + off))
    return pl.pallas_call(
        functools.partial(_mix_out_kernel, period),
        grid=(b, l // tm),
        in_specs=[yspec, yspec, yspec, yspec, cspec(0), cspec(1), cspec(2),
                  pl.BlockSpec((None, tm, d_f), lambda bi, i: (bi, i, 0)),
                  pl.BlockSpec((None, tm, d), lambda bi, i: (bi, i, 0)),
                  pl.BlockSpec((None, 1, d), lambda bi, i: (bi, 0, 0)),
                  pl.BlockSpec((1, d_rwkv), lambda bi, i: (0, 0)),
                  pl.BlockSpec((1, d_rwkv), lambda bi, i: (0, 0)),
                  pl.BlockSpec((3, d_conv), lambda bi, i: (0, 0)),
                  pl.BlockSpec(lw["w_out"].shape, lambda bi, i: (0, 0))],
        out_specs=pl.BlockSpec((None, tm, d), lambda bi, i: (bi, i, 0)),
        out_shape=jax.ShapeDtypeStruct((b, l, d), F32),
        scratch_shapes=[pltpu.VMEM((tm, lw["w_out"].shape[0]), BF16)],
        compiler_params=_cparams(("parallel", "parallel")),
        name="mix_out",
    )(yf, yb, z, gate, px, px, px, ft, x, ga, lw["ln_w"], lw["ln_b"], lw["conv_w"], lw["w_out"])


def _ffn_kernel(final, x_ref, g_ref, sh_ref, sc_ref, ga_ref, wg_ref, wu_ref, wd_ref, gf_ref, o_ref,
                h_ref, acc_ref):
    j = pl.program_id(2)

    @pl.when(j == 0)
    def _():
        h_ref[...] = _rms_mod(x_ref[...], g_ref[...], sh_ref[...], sc_ref[...]).astype(BF16)
        acc_ref[...] = jnp.zeros_like(acc_ref)

    h = h_ref[...]
    a = jnp.dot(h, wg_ref[...], preferred_element_type=F32)
    u = jnp.dot(h, wu_ref[...], preferred_element_type=F32)
    t = (a * jax.nn.sigmoid(a)) * u
    acc_ref[...] += jnp.dot(t.astype(BF16), wd_ref[...], preferred_element_type=F32)

    @pl.when(j == pl.num_programs(2) - 1)
    def _():
        xn = x_ref[...] + ga_ref[...] * acc_ref[...]
        if final:
            ms = jnp.mean(xn * xn, axis=-1, keepdims=True)
            xn = xn * lax.rsqrt(ms + RMS_EPS) * gf_ref[...]
        o_ref[...] = xn


def _ffn(x, g, shift, scale, ga, wg, wu, wd, g_final, final):
    b, l, d = x.shape
    ff = wg.shape[1]
    tm = min(512, l)
    tf = 512
    vec = pl.BlockSpec((None, 1, d), lambda bi, i, j: (bi, 0, 0))
    gspec = pl.BlockSpec((1, d), lambda bi, i, j: (0, 0))
    return pl.pallas_call(
        functools.partial(_ffn_kernel, final),
        grid=(b, l // tm, ff // tf),
        in_specs=[pl.BlockSpec((None, tm, d), lambda bi, i, j: (bi, i, 0)),
                  gspec, vec, vec, vec,
                  pl.BlockSpec((d, tf), lambda bi, i, j: (0, j)),
                  pl.BlockSpec((d, tf), lambda bi, i, j: (0, j)),
                  pl.BlockSpec((tf, d), lambda bi, i, j: (j, 0)),
                  gspec],
        out_specs=pl.BlockSpec((None, tm, d), lambda bi, i, j: (bi, i, 0)),
        out_shape=jax.ShapeDtypeStruct((b, l, d), F32),
        scratch_shapes=[pltpu.VMEM((tm, d), BF16), pltpu.VMEM((tm, d), F32)],
        compiler_params=_cparams(("parallel", "parallel", "arbitrary")),
        name="ffn",
    )(x, g, shift, scale, ga, wg, wu, wd, g_final)


def _layer_weights(i, w_in, rw_shift, dec_w0, dec_up, iclr_a0, iclr_up, k_k, k_a, r_k, ln_w, ln_b, g_up,
                   conv_w, w_out, w_gate, w_up, w_down, dims):
    d_rwkv, d_conv, d_f, dl, il, gl = dims
    ng = d_rwkv // GW
    wi = w_in[i]
    sh = rw_shift[i]
    r0, k0, v0 = 0, d_rwkv, 2 * d_rwkv
    wd0 = 3 * d_rwkv
    ad0 = wd0 + dl
    gd0 = ad0 + il
    rest0 = gd0 + gl

    def regroup(m):
        cols = []
        for g in range(ng):
            for base in (r0, k0, v0):
                cols.append(m[:, base + g * GW: base + (g + 1) * GW])
        padl = jnp.zeros((m.shape[0], LORA_PAD - dl), m.dtype)
        padi = jnp.zeros((m.shape[0], LORA_PAD - il), m.dtype)
        cols += [m[:, wd0:ad0], padl, m[:, ad0:gd0], padi, m[:, gd0:rest0]]
        return jnp.concatenate(cols, axis=1)

    w_rw = regroup(wi)
    sh_all = regroup(sh)
    pad_rows = lambda m, n: jnp.concatenate([m, jnp.zeros(m.shape[:-2] + (n - m.shape[-2], m.shape[-1]), m.dtype)], -2)
    zero = jnp.zeros((d_rwkv,), F32)
    pvec = jnp.stack([k_k[i], k_a[i], r_k[i].reshape(-1), dec_w0[i, 0], dec_w0[i, 1],
                      iclr_a0[i, 0], iclr_a0[i, 1], zero])
    return dict(
        ng=ng,
        w_in=jnp.concatenate([w_rw, wi[:, rest0:]], axis=1).astype(BF16),
        mix=sh_all,
        pvec=pvec,
        dec_up=pad_rows(dec_up[i], LORA_PAD).astype(BF16),
        iclr_up=pad_rows(iclr_up[i], LORA_PAD).astype(BF16),
        g_up=g_up[i].astype(BF16),
        ln_w=ln_w[i][None, :], ln_b=ln_b[i][None, :],
        conv_w=conv_w[i],
        w_out=w_out[i].astype(BF16),
        w_gate=w_gate[i].astype(BF16), w_up=w_up[i].astype(BF16), w_down=w_down[i].astype(BF16),
    )


def kernel(x, c, ctx, c_ctx, w_mod, b_mod, norm_mix, w_in, rw_shift, dec_w0, dec_up, iclr_a0, iclr_up,
           k_k, k_a, r_k, ln_w, ln_b, g_up, conv_w, w_out, norm_ffn, w_gate, w_up, w_down, norm_final):
    b, l, d = x.shape
    lc = ctx.shape[1]
    depth = w_mod.shape[0]
    d_rwkv = k_k.shape[1]
    d_conv = conv_w.shape[2]
    dl, il, gl = dec_up.shape[2], iclr_up.shape[2], g_up.shape[1]
    d_f = w_out.shape[1] - d_rwkv - d_conv
    assert dl <= LORA_PAD and il <= LORA_PAD and gl == 2 * LORA_PAD and d_rwkv % GW == 0
    assert l % GRID_W == 0 and l % CHUNK == 0 and lc % CHUNK == 0
    nsub_x = math.gcd(l // CHUNK, WKV_SUB)
    nsub_c = math.gcd(lc // CHUNK, WKV_SUB)
    ng = d_rwkv // GW
    n_rw = 3 * d_rwkv + 4 * LORA_PAD
    conv_blk = n_rw // d_conv
    ft_blk = (n_rw + 3 * d_conv) // d_f
    assert conv_blk * d_conv == n_rw and ft_blk * d_f == n_rw + 3 * d_conv

    cvec = jnp.concatenate([c, c_ctx[None, :], jnp.zeros((8 - b - 1, d), F32)], axis=0)
    mods = _adaln(cvec, w_mod, b_mod[:, None, :])

    def mod(i, j, ctx_rows):
        m = mods[i, :, j * d:(j + 1) * d]
        if ctx_rows:
            return jnp.broadcast_to(m[b:b + 1], (b, d))[:, None, :]
        return m[:b][:, None, :]

    fft_x = l % (FFT_L1 * 128) == 0
    wtab_c, ctab_c = _seq_table(lc), _channel_tables(lc, d_f)
    ctab_x = _channel_tables(l, d_f)
    if fft_x:
        ftab_x = _fft_tables(l)
    else:
        wtab_x = _seq_table(l)
    dims = (d_rwkv, d_conv, d_f, dl, il, gl)
    s_zero = jnp.zeros((b, 2, ng * N_PAIR, PW, PW), F32)
    gfin = norm_final[None, :]
    xc = ctx
    for i in range(depth):
        lw = _layer_weights(i, w_in, rw_shift, dec_w0, dec_up, iclr_a0, iclr_up, k_k, k_a, r_k, ln_w, ln_b,
                            g_up, conv_w, w_out, w_gate, w_up, w_down, dims)
        gmix = norm_mix[i][None, :]
        gffn = norm_ffn[i][None, :]
        last = i == depth - 1
        pc = _in_proj(xc, gmix, mod(i, 0, True), mod(i, 1, True), lw["w_in"], lw["mix"])
        px = _in_proj(x, gmix, mod(i, 0, False), mod(i, 1, False), lw["w_in"], lw["mix"])
        yfc, ybc, zc, gc, s_ctx = _wkv(pc, lw, s_zero, nsub_c)
        yfx, ybx, zx, gx, _ = _wkv(px, lw, s_ctx, nsub_x)
        if fft_x:
            ftx = _fourier_fft(px, ctab_x, ftab_x, ft_blk, d_f)
        else:
            ftx = _fourier_dense(px, wtab_x, ctab_x, ft_blk, d_f)
        x = _mix_out(yfx, ybx, zx, gx, px, ftx, x, mod(i, 2, False), lw, GRID_W, conv_blk)
        x = _ffn(x, gffn, mod(i, 3, False), mod(i, 4, False), mod(i, 5, False),
                 lw["w_gate"], lw["w_up"], lw["w_down"], gfin, last)
        if not last:
            ftc = _fourier_dense(pc, wtab_c, ctab_c, ft_blk, d_f)
            xc = _mix_out(yfc, ybc, zc, gc, pc, ftc, xc, mod(i, 2, True), lw, lc, conv_blk)
            xc = _ffn(xc, gffn, mod(i, 3, True), mod(i, 4, True), mod(i, 5, True),
                      lw["w_gate"], lw["w_up"], lw["w_down"], gfin, False)
    return x
```

```python
import functools
import math

import jax
import jax.numpy as jnp
from jax import lax
from jax.experimental import pallas as pl
from jax.experimental.pallas import tpu as pltpu

F32 = jnp.float32
BF16 = jnp.bfloat16

HEAD = 64
GROUP_HEADS = 4
GW = GROUP_HEADS * HEAD
PAIR_HEADS = 2
PW = PAIR_HEADS * HEAD
N_PAIR = GW // PW
CHUNK = 64
WKV_SUB = 4
HALO = 16
LANES = 128
LORA_PAD = 128
GRID_W = 64
FFT_L1 = 64
N_MOD = 6
RMS_EPS = 1e-6
GN_EPS = 64e-5
KK_EPS = 1e-12
VMEM_LIMIT = 56 * 1024 * 1024


def _cparams(sem):
    return pltpu.CompilerParams(dimension_semantics=sem, vmem_limit_bytes=VMEM_LIMIT)


def _dot(a, b, nt=False):
    dn = (((1,), (1,)), ((), ())) if nt else (((1,), (0,)), ((), ()))
    return lax.dot_general(a.astype(BF16), b.astype(BF16), dn, preferred_element_type=F32)


def _split2(a):
    hi = a.astype(BF16)
    return hi, (a - hi.astype(F32)).astype(BF16)


def _dot_lhs_f32(a, b_exact):
    hi, lo = _split2(a)
    return _dot(hi, b_exact) + _dot(lo, b_exact)


def _dot_rhs_f32(a_exact, b):
    hi, lo = _split2(b)
    return _dot(a_exact, hi) + _dot(a_exact, lo)


def _rms_mod(x, g, shift, scale):
    ms = jnp.mean(x * x, axis=-1, keepdims=True)
    return (x * lax.rsqrt(ms + RMS_EPS) * g) * (1.0 + scale) + shift


def _adaln_kernel(c_ref, w_ref, b_ref, o_ref):
    c = c_ref[...]
    s = c * jax.nn.sigmoid(c)
    o_ref[...] = _dot(s, w_ref[...]) + b_ref[...]


def _adaln(cvec, w_mod, b_mod):
    nl, d, n = w_mod.shape
    tn = 1024
    return pl.pallas_call(
        _adaln_kernel,
        grid=(nl, n // tn),
        in_specs=[pl.BlockSpec((8, d), lambda l, j: (0, 0)),
                  pl.BlockSpec((None, d, tn), lambda l, j: (l, 0, j)),
                  pl.BlockSpec((None, 1, tn), lambda l, j: (l, 0, j))],
        out_specs=pl.BlockSpec((None, 8, tn), lambda l, j: (l, 0, j)),
        out_shape=jax.ShapeDtypeStruct((nl, 8, n), F32),
        compiler_params=_cparams(("parallel", "parallel")),
        name="adaln",
    )(cvec, w_mod, b_mod)


def _in_proj_kernel(n_mix_tiles, x_ref, xp_ref, xn_ref, g_ref, sh_ref, sc_ref, w_ref, mix_ref, o_ref, h_ref):
    i = pl.program_id(1)
    tm = x_ref.shape[0]

    @pl.when(pl.program_id(2) == 0)
    def _():
        g, sh, sc = g_ref[...], sh_ref[...], sc_ref[...]
        prev = jnp.where(i > 0, _rms_mod(xp_ref[...], g, sh, sc), 0.0)
        nxt = jnp.where(i < pl.num_programs(1) - 1, _rms_mod(xn_ref[...], g, sh, sc), 0.0)
        h_ref[0:HALO, :] = prev.astype(BF16)
        h_ref[HALO:HALO + tm, :] = _rms_mod(x_ref[...], g, sh, sc).astype(BF16)
        h_ref[HALO + tm:, :] = nxt.astype(BF16)

    @pl.when(pl.program_id(2) < n_mix_tiles)
    def _():
        p = jnp.dot(h_ref[...], w_ref[...], preferred_element_type=F32)
        mix = mix_ref[...]
        up = pltpu.roll(p, 1, 0)[HALO:HALO + tm]
        dn = pltpu.roll(p, tm + 2 * HALO - 1, 0)[HALO:HALO + tm]
        o_ref[...] = up * mix[0:1, :] + p[HALO:HALO + tm] * mix[1:2, :] + dn * mix[2:3, :]

    @pl.when(pl.program_id(2) >= n_mix_tiles)
    def _():
        o_ref[...] = jnp.dot(h_ref[HALO:HALO + tm, :], w_ref[...], preferred_element_type=F32)


def _in_proj(x, g, shift, scale, w, mix):
    b, l, d = x.shape
    n = w.shape[1]
    tm = min(1024, l)
    tn = 512
    n_mix_tiles = mix.shape[1] // tn
    assert n_mix_tiles * tn == mix.shape[1]
    hb = tm // HALO
    nhb = l // HALO
    vec = pl.BlockSpec((None, 1, d), lambda bi, i, j: (bi, 0, 0))
    return pl.pallas_call(
        functools.partial(_in_proj_kernel, n_mix_tiles),
        grid=(b, l // tm, n // tn),
        in_specs=[pl.BlockSpec((None, tm, d), lambda bi, i, j: (bi, i, 0)),
                  pl.BlockSpec((None, HALO, d), lambda bi, i, j: (bi, jnp.maximum(i * hb - 1, 0), 0)),
                  pl.BlockSpec((None, HALO, d), lambda bi, i, j: (bi, jnp.minimum((i + 1) * hb, nhb - 1), 0)),
                  pl.BlockSpec((1, d), lambda bi, i, j: (0, 0)),
                  vec, vec,
                  pl.BlockSpec((d, tn), lambda bi, i, j: (0, j)),
                  pl.BlockSpec((3, tn), lambda bi, i, j: (0, jnp.minimum(j, n_mix_tiles - 1)))],
        out_specs=pl.BlockSpec((None, tm, tn), lambda bi, i, j: (bi, i, j)),
        out_shape=jax.ShapeDtypeStruct((b, l, n), F32),
        scratch_shapes=[pltpu.VMEM((tm + 2 * HALO, d), BF16)],
        compiler_params=_cparams(("parallel", "parallel", "arbitrary")),
        name="in_proj",
    )(x, x, x, g, shift, scale, w, mix)


def _block_masks(width):
    r = lax.broadcasted_iota(jnp.int32, (width, width), 0)
    c = lax.broadcasted_iota(jnp.int32, (width, width), 1)
    return (r // HEAD) == (c // HEAD), r % HEAD, c % HEAD, r == c


def _rs(x, bd):
    return jnp.where(bd, jnp.concatenate([x] * PAIR_HEADS, axis=0), 0.0)


def _ls(x_rs):
    out = x_rs[0:CHUNK]
    for h in range(1, PAIR_HEADS):
        out = out + x_rs[h * CHUNK:(h + 1) * CHUNK]
    return out


def _each(f, *lists):
    return [f(*a) for a in zip(*lists)]


def _chunk_affine(chunks, masks):
    bd, tt, ss, eye = masks
    row = lax.broadcasted_iota(jnp.int32, (CHUNK, CHUNK), 0)
    col = lax.broadcasted_iota(jnp.int32, (CHUNK, CHUNK), 1)
    tri = {rev: jnp.where((row <= col) if rev else (row >= col), 1.0, 0.0).astype(BF16) for rev in (False, True)}
    strict = {False: bd & (ss < tt), True: bd & (ss > tt)}
    r2 = lax.broadcasted_iota(jnp.int32, (PW, 2 * PW), 0)
    c2 = lax.broadcasted_iota(jnp.int32, (PW, 2 * PW), 1)
    bd2 = (r2 // HEAD) == ((c2 % PW) // HEAD)
    incl2 = {False: bd2 & (c2 % HEAD <= r2 % HEAD), True: bd2 & (c2 % HEAD >= r2 % HEAD)}

    crev = [ch[6] for ch in chunks]
    logw = [ch[5] for ch in chunks]
    cum = _each(lambda rv, lw: _dot_rhs_f32(tri[rv], lw), crev, logw)
    total = _each(lambda rv, cm: cm[0:1] if rv else cm[CHUNK - 1:CHUNK], crev, cum)
    p_inv = _each(lambda cm: jnp.exp(-cm), cum)
    p_end = _each(lambda tot, cm: jnp.exp(tot - cm), total, cum)
    bvec = [ch[3] * ch[4] for ch in chunks]
    wide = dict(
        a=_each(lambda ch, cm, lw: -ch[3] * jnp.exp(cm - lw), chunks, cum, logw),
        r=_each(lambda ch, cm: ch[0] * jnp.exp(cm), chunks, cum),
        b=_each(lambda b, p: b * p, bvec, p_inv),
        k=_each(lambda ch, p: ch[1] * p, chunks, p_inv),
        v=[ch[2] for ch in chunks],
        be=_each(lambda b, p: b * p, bvec, p_end),
        ke=_each(lambda ch, p: ch[1] * p, chunks, p_end),
        pc=_each(jnp.exp, total),
    )

    def pairs(name):
        return [x[:, p * PW:(p + 1) * PW] for x in wide[name] for p in range(N_PAIR)]

    rev = [rv for rv in crev for _ in range(N_PAIR)]
    r_t = pairs("r")
    a_rs = _each(lambda x: _rs(x, bd).astype(BF16), pairs("a"))
    r_rs = _each(lambda x: _rs(x, bd).astype(BF16), r_t)
    b_rs = _each(lambda x: jnp.concatenate([x.astype(BF16)] * PAIR_HEADS, axis=0), pairs("b"))
    k_rs = _each(lambda x: jnp.concatenate([x.astype(BF16)] * PAIR_HEADS, axis=0), pairs("k"))
    v_rs = _each(lambda x: _rs(x, bd).astype(BF16), pairs("v"))
    aa = _each(lambda a, r, b, k: _dot(jnp.concatenate([a, r], axis=0), jnp.concatenate([b, k], axis=0), nt=True),
               a_rs, r_rs, b_rs, k_rs)
    a_ab = _each(lambda rv, x: jnp.where(strict[rv], x[0:PW, 0:PW], 0.0), rev, aa)
    a_ak = _each(lambda rv, x: jnp.where(strict[rv], x[0:PW, PW:], 0.0), rev, aa)
    a_r = _each(lambda rv, x: jnp.where(incl2[rv], x[PW:, :], 0.0).astype(BF16), rev, aa)
    t_inv = _each(lambda a: jnp.where(eye, 1.0, a), a_ab)
    pw = _each(lambda a: _dot(a, a), a_ab)
    for _ in range(int(math.log2(CHUNK)) - 2):
        res = _each(lambda p, t: _dot(p, jnp.concatenate([p, t], axis=1)), pw, t_inv)
        pw = [x[:, 0:PW] for x in res]
        t_inv = _each(lambda t, x: t + x[:, PW:], t_inv, res)
    t_inv = _each(lambda p, t: (t + _dot(p, t)).astype(BF16), pw, t_inv)
    akv = _each(_dot, a_ak, v_rs)
    tw = _each(lambda t, a, x: _dot(t, jnp.concatenate([a, x.astype(BF16)], axis=1)).astype(BF16),
               t_inv, a_rs, akv)
    bk_t = _each(lambda b, k: jnp.concatenate([_rs(b, bd), _rs(k, bd)], axis=0).T.astype(BF16),
                 pairs("be"), pairs("ke"))
    fin = _each(lambda ar, bk, w, v: _dot(jnp.concatenate([ar, bk], axis=0),
                                          jnp.concatenate([w, jnp.concatenate([jnp.zeros_like(v), v], axis=1)],
                                                          axis=0)),
                a_r, bk_t, tw, v_rs)
    rh = _each(lambda r, f: r + _ls(f[0:PW, 0:PW]), r_t, fin)
    y0 = _each(lambda f: _ls(f[0:PW, PW:]), fin)
    m = _each(lambda pc, f: jnp.where(eye, pc, 0.0) + f[PW:, 0:PW], pairs("pc"), fin)
    nn = [f[PW:, PW:] for f in fin]
    out = list(zip(y0, rh, m, nn))
    return [out[i * N_PAIR:(i + 1) * N_PAIR] for i in range(len(chunks))]


def _head_sum(x, ones_bd):
    return _dot_lhs_f32(x, ones_bd)


def _wkv_kernel(nsub, r_f, k_f, v_f, lo_f, r_b, k_b, v_b, lo_b, pvec, dec_up, iclr_up, g_up, s0,
                yf_ref, yb_ref, z_ref, gate_ref, sfin_ref,
                zst):
    c = pl.program_id(2)
    nsteps = pl.num_programs(2)

    @pl.when(c == 0)
    def _():
        zst[...] = s0[...]

    masks = _block_masks(PW)
    ones_bd = jnp.where(_block_masks(GW)[0], 1.0, 0.0).astype(BF16)
    pv = pvec[...]
    k_k, k_a, r_k = pv[0:1], pv[1:2], pv[2:3]
    w0 = (pv[3:4], pv[4:5])
    a0 = (pv[5:6], pv[6:7])
    sig_scale = math.exp(-0.5)

    def streams(r_ref, k_ref, v_ref, lo_ref, d):
        r, k, v, lo = r_ref[...], k_ref[...], v_ref[...], lo_ref[...]
        wd = jnp.tanh(lo[:, 0:LORA_PAD])
        ad = lo[:, LORA_PAD:2 * LORA_PAD]
        gd = lo[:, 2 * LORA_PAD:]
        kk = k * k_k
        kkn = kk * lax.rsqrt(jnp.maximum(_head_sum(kk * kk, ones_bd), KK_EPS * KK_EPS))
        logw = -sig_scale * jax.nn.sigmoid(w0[d] + _dot(wd, dec_up[d]))
        aic = jax.nn.sigmoid(a0[d] + _dot(ad, iclr_up[d]))
        kd = k * (1.0 + (aic - 1.0) * k_a)
        return r, k, v, kkn, logw, aic, kd, ad, gd

    r, k, v, kkn, logw, aic, kd, ad, gd = streams(r_f, k_f, v_f, lo_f, 0)
    aic_o = jax.nn.sigmoid(a0[1] + _dot(ad, iclr_up[1]))
    kd_o = k * (1.0 + (aic_o - 1.0) * k_a)
    z_ref[...] = _head_sum(r * (kd + kd_o) * r_k, ones_bd) * v
    gate_ref[...] = _dot(jax.nn.sigmoid(gd), g_up[...])
    chains = [tuple(t[j * CHUNK:(j + 1) * CHUNK] for t in (r, kd, v, kkn, aic, logw)) + (False,)
              for j in range(nsub)]
    r, k, v, kkn, logw, aic, kd, ad, gd = streams(r_b, k_b, v_b, lo_b, 1)
    chains += [tuple(t[j * CHUNK:(j + 1) * CHUNK] for t in (r, kd, v, kkn, aic, logw)) + (True,)
               for j in range(nsub)]
    affine = _chunk_affine(chains, masks)
    fwd, bwd = affine[:nsub], affine[nsub:]

    for d, (steps, y_ref) in enumerate(((list(range(nsub)), yf_ref), (list(reversed(range(nsub))), yb_ref))):
        affine_d = fwd if d == 0 else bwd
        z = [zst[d, p] for p in range(N_PAIR)]
        for j in steps:
            mz = [_dot(jnp.concatenate([affine_d[j][p][2], affine_d[j][p][1]], axis=0), z[p])
                  for p in range(N_PAIR)]
            y_ref[j * CHUNK:(j + 1) * CHUNK, :] = jnp.concatenate(
                [affine_d[j][p][0] + mz[p][PW:] for p in range(N_PAIR)], axis=1)
            z = [mz[p][0:PW] + affine_d[j][p][3] for p in range(N_PAIR)]
        for p in range(N_PAIR):
            zst[d, p] = z[p]

    @pl.when(c == nsteps - 1)
    def _():
        sfin_ref[...] = zst[...]


def _wkv(px, lw, s0, nsub):
    b, l, _ = px.shape
    ng = lw["ng"]
    d_rwkv = ng * GW
    t = nsub * CHUNK
    nsteps = l // t
    lo_blk = 3 * d_rwkv // (4 * LORA_PAD)
    lo_w = 4 * LORA_PAD

    def main_specs(blk):
        return [pl.BlockSpec((None, t, GW), lambda bi, g, c, s=s: (bi, blk(c), s * ng + g)) for s in range(3)] + [
            pl.BlockSpec((None, t, lo_w), lambda bi, g, c: (bi, blk(c), lo_blk))]

    fblk = lambda c: c
    bblk = lambda c: nsteps - 1 - c
    in_specs = (main_specs(fblk) + main_specs(bblk) + [
        pl.BlockSpec((8, GW), lambda bi, g, c: (0, g)),
        pl.BlockSpec((2, LORA_PAD, GW), lambda bi, g, c: (0, 0, g)),
        pl.BlockSpec((2, LORA_PAD, GW), lambda bi, g, c: (0, 0, g)),
        pl.BlockSpec((2 * LORA_PAD, GW), lambda bi, g, c: (0, g)),
        pl.BlockSpec((None, 2, N_PAIR, PW, PW), lambda bi, g, c: (bi, 0, g, 0, 0)),
    ])
    yspec = lambda blk: pl.BlockSpec((None, t, GW), lambda bi, g, c: (bi, blk(c), g))
    out_specs = [yspec(fblk), yspec(bblk), yspec(fblk), yspec(fblk),
                 pl.BlockSpec((None, 2, N_PAIR, PW, PW), lambda bi, g, c: (bi, 0, g, 0, 0))]
    ysh = jax.ShapeDtypeStruct((b, l, d_rwkv), F32)
    return pl.pallas_call(
        functools.partial(_wkv_kernel, nsub),
        grid=(b, ng, nsteps),
        in_specs=in_specs,
        out_specs=out_specs,
        out_shape=[ysh, ysh, ysh, ysh, jax.ShapeDtypeStruct(s0.shape, F32)],
        scratch_shapes=[pltpu.VMEM((2, N_PAIR, PW, PW), F32)],
        compiler_params=_cparams(("parallel", "parallel", "arbitrary")),
        name="wkv",
    )(*([px] * 8), lw["pvec"], lw["dec_up"], lw["iclr_up"], lw["g_up"], s0)


def _dft_ch_kernel(u_ref, t_ref, o_ref):
    u = u_ref[...]
    hi = u.astype(BF16)
    lo = (u - hi.astype(F32)).astype(BF16)
    tab = t_ref[...]
    o_ref[...] = (_dot(hi, tab) + _dot(lo, tab)).astype(BF16)


def _dft_channels(px, tabs, ft_blk, d_f):
    b, l, _ = px.shape
    tm = min(512, l)
    nt = l // tm
    return pl.pallas_call(
        _dft_ch_kernel,
        grid=(b, 2, nt),
        in_specs=[pl.BlockSpec((None, tm, d_f), lambda bi, s, i: (bi, i, ft_blk)),
                  pl.BlockSpec((None, d_f, d_f), lambda bi, s, i: (s, 0, 0))],
        out_specs=pl.BlockSpec((tm, d_f), lambda bi, s, i: (s * nt + i, bi)),
        out_shape=jax.ShapeDtypeStruct((2 * l, b * d_f), BF16),
        compiler_params=_cparams(("parallel", "parallel", "parallel")),
        name="dft_channels",
    )(px, tabs)


def _dft_seq_kernel(w_ref, u_ref, o_ref, acc_ref):
    kk = pl.program_id(1)

    @pl.when(kk == 0)
    def _():
        acc_ref[...] = jnp.zeros_like(acc_ref)

    acc_ref[...] += jnp.dot(w_ref[...], u_ref[...], preferred_element_type=F32)

    @pl.when(kk == pl.num_programs(1) - 1)
    def _():
        o_ref[...] = acc_ref[...]


def _dft_seq(wtab, uu):
    l, k2 = wtab.shape
    n = uu.shape[1]
    tm = min(1024, l)
    tk = min(2048, k2)
    return pl.pallas_call(
        _dft_seq_kernel,
        grid=(l // tm, k2 // tk),
        in_specs=[pl.BlockSpec((tm, tk), lambda i, kk: (i, kk)),
                  pl.BlockSpec((tk, n), lambda i, kk: (kk, 0))],
        out_specs=pl.BlockSpec((tm, n), lambda i, kk: (i, 0)),
        out_shape=jax.ShapeDtypeStruct((l, n), F32),
        scratch_shapes=[pltpu.VMEM((tm, n), F32)],
        compiler_params=_cparams(("parallel", "arbitrary")),
        name="dft_seq",
    )(wtab, uu)


def _seq_table(l):
    f = math.gcd(l, 128)

    def thin(n, stride):
        p = lax.broadcasted_iota(jnp.int32, (l, n), 0)
        q = lax.broadcasted_iota(jnp.int32, (l, n), 1) * stride
        ang = ((p * q) % l).astype(F32) * (2.0 * math.pi / l)
        return jnp.cos(ang), jnp.sin(ang)

    c1, s1 = (t[:, :, None] for t in thin(l // f, f))
    c2, s2 = (t[:, None, :] for t in thin(f, 1))
    return jnp.concatenate([(c1 * c2 - s1 * s2).reshape(l, l), (-(s1 * c2 + c1 * s2)).reshape(l, l)],
                           axis=1).astype(BF16)


def _channel_tables(l, d_f):
    cc = lax.broadcasted_iota(jnp.int32, (d_f, d_f), 0)
    qq = lax.broadcasted_iota(jnp.int32, (d_f, d_f), 1)
    a2 = (((cc % HEAD) * (qq % HEAD)) % HEAD).astype(F32) * (2.0 * math.pi / HEAD)
    same = (cc // HEAD) == (qq // HEAD)
    scale = 1.0 / math.sqrt(l * HEAD)
    return jnp.stack([jnp.where(same, jnp.cos(a2), 0.0), jnp.where(same, jnp.sin(a2), 0.0)]) * scale


def _dot3(a, b):
    ah = a.astype(BF16)
    al = (a - ah.astype(F32)).astype(BF16)
    bh = b.astype(BF16)
    bl = (b - bh.astype(F32)).astype(BF16)
    return _dot(ah, bh) + (_dot(ah, bl) + _dot(al, bh))


FFT_SUB = 8


def _fft_stage1_kernel(*refs):
    u_refs, (ct_ref, t2_ref, tw_ref, o_ref) = refs[:-4], refs[-4:]
    i = pl.program_id(1)
    l2 = t2_ref.shape[0] // 2
    for j in range(FFT_SUB):
        u = jnp.concatenate([r[pl.ds(i * FFT_SUB + j, l2, stride=FFT_L1), :] for r in u_refs], axis=1)
        z = jnp.concatenate([_dot3(u, ct_ref[0]), -_dot3(u, ct_ref[1])], axis=0)
        bm = _dot3(t2_ref[...], z)
        br, bi = bm[0:l2], bm[l2:]
        tw = tw_ref[j]
        cs, sn = tw[:, 0:1], tw[:, 1:2]
        comp = (br * cs + bi * sn, bi * cs - br * sn)
        for c in range(2):
            for pb in range(l2 // FFT_SUB):
                o_ref[pb, c, j * FFT_SUB:(j + 1) * FFT_SUB, :] = comp[c][pb * FFT_SUB:(pb + 1) * FFT_SUB]


def _fft_stage2_kernel(t1_ref, b_ref, o_ref):
    x = jnp.concatenate([b_ref[0], b_ref[1]], axis=0)
    y = _dot3(t1_ref[...], x)
    for p1 in range(o_ref.shape[0]):
        o_ref[p1] = y[p1 * FFT_SUB:(p1 + 1) * FFT_SUB]


def _fft_tables(l):
    l1, l2 = FFT_L1, l // FFT_L1

    def cs(n, m, period):
        p = lax.broadcasted_iota(jnp.int32, (n, m), 0)
        q = lax.broadcasted_iota(jnp.int32, (n, m), 1)
        ang = ((p * q) % period).astype(F32) * (2.0 * math.pi / period)
        return jnp.cos(ang), jnp.sin(ang)

    c2, s2 = cs(l2, l2, l2)
    t2 = jnp.concatenate([jnp.concatenate([c2, s2], axis=1), jnp.concatenate([-s2, c2], axis=1)], axis=0)
    tw = jnp.stack(cs(l1, l2, l), axis=-1)
    c1, s1 = cs(l1, l1, l1)
    t1 = jnp.kron(jnp.concatenate([c1, s1], axis=1), jnp.eye(FFT_SUB, dtype=F32))
    return t2, tw, t1


def _fourier_fft(px, ctab, tabs, ft_blk, d_f):
    b, l, n = px.shape
    l1, l2 = FFT_L1, l // FFT_L1
    npb = l2 // FFT_SUB
    nslab = d_f // LANES
    t2, tw, t1 = tabs
    stage1 = pl.pallas_call(
        _fft_stage1_kernel,
        grid=(b, l1 // FFT_SUB),
        in_specs=[pl.BlockSpec((None, l, LANES), lambda bi, i, k=k: (bi, 0, ft_blk * nslab + k))
                  for k in range(nslab)] + [
                  pl.BlockSpec(ctab.shape, lambda bi, i: (0, 0, 0)),
                  pl.BlockSpec(t2.shape, lambda bi, i: (0, 0)),
                  pl.BlockSpec((FFT_SUB, l2, 2), lambda bi, i: (i, 0, 0))],
        out_specs=pl.BlockSpec((None, npb, 2, FFT_SUB * FFT_SUB, d_f), lambda bi, i: (bi, 0, 0, i, 0)),
        out_shape=jax.ShapeDtypeStruct((b, npb, 2, l1 * FFT_SUB, d_f), F32),
        compiler_params=_cparams(("parallel", "arbitrary")),
        name="fft_stage1",
    )(*([px] * nslab), ctab, t2, tw)
    y = pl.pallas_call(
        _fft_stage2_kernel,
        grid=(b, npb),
        in_specs=[pl.BlockSpec(t1.shape, lambda bi, j: (0, 0)),
                  pl.BlockSpec((None, None, 2, l1 * FFT_SUB, d_f), lambda bi, j: (bi, j, 0, 0, 0))],
        out_specs=pl.BlockSpec((None, l1, None, FFT_SUB, d_f), lambda bi, j: (bi, 0, j, 0, 0)),
        out_shape=jax.ShapeDtypeStruct((b, l1, npb, FFT_SUB, d_f), F32),
        compiler_params=_cparams(("parallel", "parallel")),
        name="fft_stage2",
    )(t1, stage1)
    return y.reshape(b, l, d_f)


def _fourier_dense(px, wtab, ctab, ft_blk, d_f):
    b, l, _ = px.shape
    y = _dft_seq(wtab, _dft_channels(px, ctab, ft_blk, d_f))
    return y.reshape(l, b, d_f).transpose(1, 0, 2)


def _mix_out_kernel(period, yf_ref, yb_ref, z_ref, gate_ref, cg_ref, cx_ref, cb_ref, ft_ref, x_ref,
                    ga_ref, lnw_ref, lnb_ref, cw_ref, w_ref, o_ref, mix_ref):
    d_rwkv = yf_ref.shape[1]
    d_conv = cg_ref.shape[1]
    tm = yf_ref.shape[0]
    r = lax.broadcasted_iota(jnp.int32, (GW, GW), 0)
    c = lax.broadcasted_iota(jnp.int32, (GW, GW), 1)
    ones_bd = jnp.where((r // HEAD) == (c // HEAD), 1.0, 0.0).astype(BF16)
    inv_n = 1.0 / HEAD
    for s in range(d_rwkv // GW):
        sl = slice(s * GW, (s + 1) * GW)
        y = yf_ref[:, sl] + yb_ref[:, sl]
        mu = _dot_lhs_f32(y, ones_bd) * inv_n
        dlt = y - mu
        var = _dot_lhs_f32(dlt * dlt, ones_bd) * inv_n
        yn = dlt * lax.rsqrt(var + GN_EPS) * lnw_ref[:, sl] + lnb_ref[:, sl] + z_ref[:, sl]
        mix_ref[:, sl] = (yn * gate_ref[:, sl]).astype(BF16)
    u = cg_ref[...] * cx_ref[...]
    rowid = lax.broadcasted_iota(jnp.int32, u.shape, 0) % period
    up = jnp.where(rowid == 0, 0.0, pltpu.roll(u, 1, 0))
    dn = jnp.where(rowid == period - 1, 0.0, pltpu.roll(u, tm - 1, 0))
    cw = cw_ref[...]
    conv = cb_ref[...] * (up * cw[0:1] + u * cw[1:2] + dn * cw[2:3])
    mix_ref[:, d_rwkv:d_rwkv + d_conv] = conv.astype(BF16)
    mix_ref[:, d_rwkv + d_conv:] = ft_ref[...].astype(BF16)
    o_ref[...] = x_ref[...] + ga_ref[...] * jnp.dot(mix_ref[...], w_ref[...], preferred_element_type=F32)


def _mix_out(yf, yb, z, gate, px, ft, x, ga, lw, period, conv_blk):
    b, l, d = x.shape
    d_rwkv = yf.shape[2]
    d_conv = lw["conv_w"].shape[1]
    d_f = ft.shape[2]
    tm = min(256, l)
    assert tm % period == 0
    yspec = pl.BlockSpec((None, tm, d_rwkv), lambda bi, i: (bi, i, 0))
    cspec = lambda off: pl.BlockSpec((None, tm, d_conv), lambda bi, i: (bi, i, conv_blk + off))
    return pl.pallas_call(
        functools.partial(_mix_out_kernel, period),
        grid=(b, l // tm),
        in_specs=[yspec, yspec, yspec, yspec, cspec(0), cspec(1), cspec(2),
                  pl.BlockSpec((None, tm, d_f), lambda bi, i: (bi, i, 0)),
                  pl.BlockSpec((None, tm, d), lambda bi, i: (bi, i, 0)),
                  pl.BlockSpec((None, 1, d), lambda bi, i: (bi, 0, 0)),
                  pl.BlockSpec((1, d_rwkv), lambda bi, i: (0, 0)),
                  pl.BlockSpec((1, d_rwkv), lambda bi, i: (0, 0)),
                  pl.BlockSpec((3, d_conv), lambda bi, i: (0, 0)),
                  pl.BlockSpec(lw["w_out"].shape, lambda bi, i: (0, 0))],
        out_specs=pl.BlockSpec((None, tm, d), lambda bi, i: (bi, i, 0)),
        out_shape=jax.ShapeDtypeStruct((b, l, d), F32),
        scratch_shapes=[pltpu.VMEM((tm, lw["w_out"].shape[0]), BF16)],
        compiler_params=_cparams(("parallel", "parallel")),
        name="mix_out",
    )(yf, yb, z, gate, px, px, px, ft, x, ga, lw["ln_w"], lw["ln_b"], lw["conv_w"], lw["w_out"])


def _ffn_kernel(final, x_ref, g_ref, sh_ref, sc_ref, ga_ref, wg_ref, wu_ref, wd_ref, gf_ref, o_ref,
                h_ref, acc_ref):
    j = pl.program_id(2)

    @pl.when(j == 0)
    def _():
        h_ref[...] = _rms_mod(x_ref[...], g_ref[...], sh_ref[...], sc_ref[...]).astype(BF16)
        acc_ref[...] = jnp.zeros_like(acc_ref)

    h = h_ref[...]
    a = jnp.dot(h, wg_ref[...], preferred_element_type=F32)
    u = jnp.dot(h, wu_ref[...], preferred_element_type=F32)
    t = (a * jax.nn.sigmoid(a)) * u
    acc_ref[...] += jnp.dot(t.astype(BF16), wd_ref[...], preferred_element_type=F32)

    @pl.when(j == pl.num_programs(2) - 1)
    def _():
        xn = x_ref[...] + ga_ref[...] * acc_ref[...]
        if final:
            ms = jnp.mean(xn * xn, axis=-1, keepdims=True)
            xn = xn * lax.rsqrt(ms + RMS_EPS) * gf_ref[...]
        o_ref[...] = xn


def _ffn(x, g, shift, scale, ga, wg, wu, wd, g_final, final):
    b, l, d = x.shape
    ff = wg.shape[1]
    tm = min(512, l)
    tf = 512
    vec = pl.BlockSpec((None, 1, d), lambda bi, i, j: (bi, 0, 0))
    gspec = pl.BlockSpec((1, d), lambda bi, i, j: (0, 0))
    return pl.pallas_call(
        functools.partial(_ffn_kernel, final),
        grid=(b, l // tm, ff // tf),
        in_specs=[pl.BlockSpec((None, tm, d), lambda bi, i, j: (bi, i, 0)),
                  gspec, vec, vec, vec,
                  pl.BlockSpec((d, tf), lambda bi, i, j: (0, j)),
                  pl.BlockSpec((d, tf), lambda bi, i, j: (0, j)),
                  pl.BlockSpec((tf, d), lambda bi, i, j: (j, 0)),
                  gspec],
        out_specs=pl.BlockSpec((None, tm, d), lambda bi, i, j: (bi, i, 0)),
        out_shape=jax.ShapeDtypeStruct((b, l, d), F32),
        scratch_shapes=[pltpu.VMEM((tm, d), BF16), pltpu.VMEM((tm, d), F32)],
        compiler_params=_cparams(("parallel", "parallel", "arbitrary")),
        name="ffn",
    )(x, g, shift, scale, ga, wg, wu, wd, g_final)


def _layer_weights(i, w_in, rw_shift, dec_w0, dec_up, iclr_a0, iclr_up, k_k, k_a, r_k, ln_w, ln_b, g_up,
                   conv_w, w_out, w_gate, w_up, w_down, dims):
    d_rwkv, d_conv, d_f, dl, il, gl = dims
    ng = d_rwkv // GW
    wd0 = 3 * d_rwkv
    ad0 = wd0 + dl
    gd0 = ad0 + il
    rest0 = gd0 + gl

    def pad_lora(m, stop):
        padl = jnp.zeros((m.shape[0], LORA_PAD - dl), m.dtype)
        padi = jnp.zeros((m.shape[0], LORA_PAD - il), m.dtype)
        return jnp.concatenate([m[:, :ad0], padl, m[:, ad0:gd0], padi, m[:, gd0:stop]], axis=1)

    pad_rows = lambda m, n: jnp.concatenate([m, jnp.zeros(m.shape[:-2] + (n - m.shape[-2], m.shape[-1]), m.dtype)], -2)
    zero = jnp.zeros((d_rwkv,), F32)
    pvec = jnp.stack([k_k[i], k_a[i], r_k[i].reshape(-1), dec_w0[i, 0], dec_w0[i, 1],
                      iclr_a0[i, 0], iclr_a0[i, 1], zero])
    return dict(
        ng=ng,
        w_in=pad_lora(w_in[i].astype(BF16), w_in.shape[2]),
        mix=pad_lora(rw_shift[i], rest0),
        pvec=pvec,
        dec_up=pad_rows(dec_up[i], LORA_PAD).astype(BF16),
        iclr_up=pad_rows(iclr_up[i], LORA_PAD).astype(BF16),
        g_up=g_up[i].astype(BF16),
        ln_w=ln_w[i][None, :], ln_b=ln_b[i][None, :],
        conv_w=conv_w[i],
        w_out=w_out[i].astype(BF16),
        w_gate=w_gate[i].astype(BF16), w_up=w_up[i].astype(BF16), w_down=w_down[i].astype(BF16),
    )


def kernel(x, c, ctx, c_ctx, w_mod, b_mod, norm_mix, w_in, rw_shift, dec_w0, dec_up, iclr_a0, iclr_up,
           k_k, k_a, r_k, ln_w, ln_b, g_up, conv_w, w_out, norm_ffn, w_gate, w_up, w_down, norm_final):
    b, l, d = x.shape
    lc = ctx.shape[1]
    depth = w_mod.shape[0]
    d_rwkv = k_k.shape[1]
    d_conv = conv_w.shape[2]
    dl, il, gl = dec_up.shape[2], iclr_up.shape[2], g_up.shape[1]
    d_f = w_out.shape[1] - d_rwkv - d_conv
    assert dl <= LORA_PAD and il <= LORA_PAD and gl == 2 * LORA_PAD and d_rwkv % GW == 0
    assert l % GRID_W == 0 and l % CHUNK == 0 and lc % CHUNK == 0
    nsub_x = math.gcd(l // CHUNK, WKV_SUB)
    nsub_c = math.gcd(lc // CHUNK, WKV_SUB)
    ng = d_rwkv // GW
    n_rw = 3 * d_rwkv + 4 * LORA_PAD
    conv_blk = n_rw // d_conv
    ft_blk = (n_rw + 3 * d_conv) // d_f
    assert conv_blk * d_conv == n_rw and ft_blk * d_f == n_rw + 3 * d_conv

    cvec = jnp.concatenate([c, c_ctx[None, :], jnp.zeros((8 - b - 1, d), F32)], axis=0)
    mods = _adaln(cvec, w_mod, b_mod[:, None, :])

    def mod(i, j, ctx_rows):
        m = mods[i, :, j * d:(j + 1) * d]
        if ctx_rows:
            return jnp.broadcast_to(m[b:b + 1], (b, d))[:, None, :]
        return m[:b][:, None, :]

    fft_x = l % (FFT_L1 * 128) == 0
    wtab_c, ctab_c = _seq_table(lc), _channel_tables(lc, d_f)
    ctab_x = _channel_tables(l, d_f)
    if fft_x:
        ftab_x = _fft_tables(l)
    else:
        wtab_x = _seq_table(l)
    dims = (d_rwkv, d_conv, d_f, dl, il, gl)
    s_zero = jnp.zeros((b, 2, ng * N_PAIR, PW, PW), F32)
    gfin = norm_final[None, :]
    xc = ctx
    for i in range(depth):
        lw = _layer_weights(i, w_in, rw_shift, dec_w0, dec_up, iclr_a0, iclr_up, k_k, k_a, r_k, ln_w, ln_b,
                            g_up, conv_w, w_out, w_gate, w_up, w_down, dims)
        gmix = norm_mix[i][None, :]
        gffn = norm_ffn[i][None, :]
        last = i == depth - 1
        pc = _in_proj(xc, gmix, mod(i, 0, True), mod(i, 1, True), lw["w_in"], lw["mix"])
        px = _in_proj(x, gmix, mod(i, 0, False), mod(i, 1, False), lw["w_in"], lw["mix"])
        yfc, ybc, zc, gc, s_ctx = _wkv(pc, lw, s_zero, nsub_c)
        yfx, ybx, zx, gx, _ = _wkv(px, lw, s_ctx, nsub_x)
        if fft_x:
            ftx = _fourier_fft(px, ctab_x, ftab_x, ft_blk, d_f)
        else:
            ftx = _fourier_dense(px, wtab_x, ctab_x, ft_blk, d_f)
        x = _mix_out(yfx, ybx, zx, gx, px, ftx, x, mod(i, 2, False), lw, GRID_W, conv_blk)
        x = _ffn(x, gffn, mod(i, 3, False), mod(i, 4, False), mod(i, 5, False),
                 lw["w_gate"], lw["w_up"], lw["w_down"], gfin, last)
        if not last:
            ftc = _fourier_dense(pc, wtab_c, ctab_c, ft_blk, d_f)
            xc = _mix_out(yfc, ybc, zc, gc, pc, ftc, xc, mod(i, 2, True), lw, lc, conv_blk)
            xc = _ffn(xc, gffn, mod(i, 3, True), mod(i, 4, True), mod(i, 5, True),
                      lw["w_gate"], lw["w_up"], lw["w_down"], gfin, False)
    return x
```

```python
import functools
import math

import jax
import jax.numpy as jnp
from jax import lax
from jax.experimental import pallas as pl
from jax.experimental.pallas import tpu as pltpu

F32 = jnp.float32
BF16 = jnp.bfloat16

HEAD = 64
GROUP_HEADS = 4
GW = GROUP_HEADS * HEAD
PAIR_HEADS = 2
PW = PAIR_HEADS * HEAD
N_PAIR = GW // PW
CHUNK = 64
WKV_SUB = 8
HALO = 16
LANES = 128
COL_TILE = 512
LORA_PAD = 128
GRID_W = 64
FFT_L1 = 64
N_MOD = 6
RMS_EPS = 1e-6
GN_EPS = 64e-5
KK_EPS = 1e-12
VMEM_LIMIT = 56 * 1024 * 1024


def _cparams(sem):
    return pltpu.CompilerParams(dimension_semantics=sem, vmem_limit_bytes=VMEM_LIMIT)


def _dot(a, b, nt=False):
    dn = (((1,), (1,)), ((), ())) if nt else (((1,), (0,)), ((), ()))
    return lax.dot_general(a.astype(BF16), b.astype(BF16), dn, preferred_element_type=F32)


def _split2(a):
    hi = a.astype(BF16)
    return hi, (a - hi.astype(F32)).astype(BF16)


def _dot_lhs_f32(a, b_exact):
    hi, lo = _split2(a)
    return _dot(hi, b_exact) + _dot(lo, b_exact)


def _dot_rhs_f32(a_exact, b):
    hi, lo = _split2(b)
    return _dot(a_exact, hi) + _dot(a_exact, lo)


def _rms_mod(x, g, shift, scale):
    ms = jnp.mean(x * x, axis=-1, keepdims=True)
    return (x * lax.rsqrt(ms + RMS_EPS) * g) * (1.0 + scale) + shift


def _adaln_kernel(c_ref, w_ref, b_ref, o_ref):
    c = c_ref[...]
    s = c * jax.nn.sigmoid(c)
    o_ref[...] = _dot(s, w_ref[...]) + b_ref[...]


def _adaln(cvec, w_mod, b_mod):
    nl, d, n = w_mod.shape
    tn = 1024
    return pl.pallas_call(
        _adaln_kernel,
        grid=(nl, n // tn),
        in_specs=[pl.BlockSpec((8, d), lambda l, j: (0, 0)),
                  pl.BlockSpec((None, d, tn), lambda l, j: (l, 0, j)),
                  pl.BlockSpec((None, 1, tn), lambda l, j: (l, 0, j))],
        out_specs=pl.BlockSpec((None, 8, tn), lambda l, j: (l, 0, j)),
        out_shape=jax.ShapeDtypeStruct((nl, 8, n), F32),
        compiler_params=_cparams(("parallel", "parallel")),
        name="adaln",
    )(cvec, w_mod, b_mod)


def _in_proj_kernel(n_mix_tiles, x_ref, xp_ref, xn_ref, g_ref, sh_ref, sc_ref, w_ref, mix_ref, o_ref, h_ref):
    i = pl.program_id(1)
    tm = x_ref.shape[0]

    @pl.when(pl.program_id(2) == 0)
    def _():
        g, sh, sc = g_ref[...], sh_ref[...], sc_ref[...]
        prev = jnp.where(i > 0, _rms_mod(xp_ref[...], g, sh, sc), 0.0)
        nxt = jnp.where(i < pl.num_programs(1) - 1, _rms_mod(xn_ref[...], g, sh, sc), 0.0)
        h_ref[0:HALO, :] = prev.astype(BF16)
        h_ref[HALO:HALO + tm, :] = _rms_mod(x_ref[...], g, sh, sc).astype(BF16)
        h_ref[HALO + tm:, :] = nxt.astype(BF16)

    @pl.when(pl.program_id(2) < n_mix_tiles)
    def _():
        p = jnp.dot(h_ref[...], w_ref[...], preferred_element_type=F32)
        mix = mix_ref[...]
        up = pltpu.roll(p, 1, 0)[HALO:HALO + tm]
        dn = pltpu.roll(p, tm + 2 * HALO - 1, 0)[HALO:HALO + tm]
        o_ref[...] = up * mix[0:1, :] + p[HALO:HALO + tm] * mix[1:2, :] + dn * mix[2:3, :]

    @pl.when(pl.program_id(2) >= n_mix_tiles)
    def _():
        o_ref[...] = jnp.dot(h_ref[HALO:HALO + tm, :], w_ref[...], preferred_element_type=F32)


def _in_proj(x, g, shift, scale, w, mix):
    b, l, d = x.shape
    tn = COL_TILE
    n = w.shape[0] * tn
    tm = min(1024, l)
    n_mix_tiles = mix.shape[1] // tn
    assert n_mix_tiles * tn == mix.shape[1]
    hb = tm // HALO
    nhb = l // HALO
    vec = pl.BlockSpec((None, 1, d), lambda bi, i, j: (bi, 0, 0))
    return pl.pallas_call(
        functools.partial(_in_proj_kernel, n_mix_tiles),
        grid=(b, l // tm, n // tn),
        in_specs=[pl.BlockSpec((None, tm, d), lambda bi, i, j: (bi, i, 0)),
                  pl.BlockSpec((None, HALO, d), lambda bi, i, j: (bi, jnp.maximum(i * hb - 1, 0), 0)),
                  pl.BlockSpec((None, HALO, d), lambda bi, i, j: (bi, jnp.minimum((i + 1) * hb, nhb - 1), 0)),
                  pl.BlockSpec((1, d), lambda bi, i, j: (0, 0)),
                  vec, vec,
                  pl.BlockSpec((None, d, tn), lambda bi, i, j: (j, 0, 0)),
                  pl.BlockSpec((3, tn), lambda bi, i, j: (0, jnp.minimum(j, n_mix_tiles - 1)))],
        out_specs=pl.BlockSpec((None, tm, tn), lambda bi, i, j: (bi, i, j)),
        out_shape=jax.ShapeDtypeStruct((b, l, n), F32),
        scratch_shapes=[pltpu.VMEM((tm + 2 * HALO, d), BF16)],
        compiler_params=_cparams(("parallel", "parallel", "arbitrary")),
        name="in_proj",
    )(x, x, x, g, shift, scale, w, mix)


def _block_masks(width):
    r = lax.broadcasted_iota(jnp.int32, (width, width), 0)
    c = lax.broadcasted_iota(jnp.int32, (width, width), 1)
    return (r // HEAD) == (c // HEAD), r % HEAD, c % HEAD, r == c


def _rs(x, bd):
    return jnp.where(bd, jnp.concatenate([x] * PAIR_HEADS, axis=0), 0.0)


def _ls(x_rs):
    out = x_rs[0:CHUNK]
    for h in range(1, PAIR_HEADS):
        out = out + x_rs[h * CHUNK:(h + 1) * CHUNK]
    return out


def _each(f, *lists):
    return [f(*a) for a in zip(*lists)]


def _chunk_affine(chunks, masks):
    bd, tt, ss, eye = masks
    row = lax.broadcasted_iota(jnp.int32, (CHUNK, CHUNK), 0)
    col = lax.broadcasted_iota(jnp.int32, (CHUNK, CHUNK), 1)
    tri = {rev: jnp.where((row <= col) if rev else (row >= col), 1.0, 0.0).astype(BF16) for rev in (False, True)}
    strict = {False: bd & (ss < tt), True: bd & (ss > tt)}
    r2 = lax.broadcasted_iota(jnp.int32, (PW, 2 * PW), 0)
    c2 = lax.broadcasted_iota(jnp.int32, (PW, 2 * PW), 1)
    bd2 = (r2 // HEAD) == ((c2 % PW) // HEAD)
    incl2 = {False: bd2 & (c2 % HEAD <= r2 % HEAD), True: bd2 & (c2 % HEAD >= r2 % HEAD)}

    crev = [ch[6] for ch in chunks]
    logw = [ch[5] for ch in chunks]
    cum = _each(lambda rv, lw: _dot_rhs_f32(tri[rv], lw), crev, logw)
    total = _each(lambda rv, cm: cm[0:1] if rv else cm[CHUNK - 1:CHUNK], crev, cum)
    p_inv = _each(lambda cm: jnp.exp(-cm), cum)
    p_end = _each(lambda tot, cm: jnp.exp(tot - cm), total, cum)
    bvec = [ch[3] * ch[4] for ch in chunks]
    wide = dict(
        a=_each(lambda ch, cm, lw: -ch[3] * jnp.exp(cm - lw), chunks, cum, logw),
        r=_each(lambda ch, cm: ch[0] * jnp.exp(cm), chunks, cum),
        b=_each(lambda b, p: b * p, bvec, p_inv),
        k=_each(lambda ch, p: ch[1] * p, chunks, p_inv),
        v=[ch[2] for ch in chunks],
        be=_each(lambda b, p: b * p, bvec, p_end),
        ke=_each(lambda ch, p: ch[1] * p, chunks, p_end),
        pc=_each(jnp.exp, total),
    )

    def pairs(name):
        return [x[:, p * PW:(p + 1) * PW] for x in wide[name] for p in range(N_PAIR)]

    rev = [rv for rv in crev for _ in range(N_PAIR)]
    r_t = pairs("r")
    a_rs = _each(lambda x: _rs(x, bd).astype(BF16), pairs("a"))
    r_rs = _each(lambda x: _rs(x, bd).astype(BF16), r_t)
    b_rs = _each(lambda x: jnp.concatenate([x.astype(BF16)] * PAIR_HEADS, axis=0), pairs("b"))
    k_rs = _each(lambda x: jnp.concatenate([x.astype(BF16)] * PAIR_HEADS, axis=0), pairs("k"))
    v_rs = _each(lambda x: _rs(x, bd).astype(BF16), pairs("v"))
    aa = _each(lambda a, r, b, k: _dot(jnp.concatenate([a, r], axis=0), jnp.concatenate([b, k], axis=0), nt=True),
               a_rs, r_rs, b_rs, k_rs)
    a_ab = _each(lambda rv, x: jnp.where(strict[rv], x[0:PW, 0:PW], 0.0), rev, aa)
    a_ak = _each(lambda rv, x: jnp.where(strict[rv], x[0:PW, PW:], 0.0), rev, aa)
    a_r = _each(lambda rv, x: jnp.where(incl2[rv], x[PW:, :], 0.0).astype(BF16), rev, aa)
    t_inv = _each(lambda a: jnp.where(eye, 1.0, a), a_ab)
    pw = _each(lambda a: _dot(a, a), a_ab)
    for _ in range(int(math.log2(CHUNK)) - 2):
        res = _each(lambda p, t: _dot(p, jnp.concatenate([p, t], axis=1)), pw, t_inv)
        pw = [x[:, 0:PW] for x in res]
        t_inv = _each(lambda t, x: t + x[:, PW:], t_inv, res)
    t_inv = _each(lambda p, t: (t + _dot(p, t)).astype(BF16), pw, t_inv)
    akv = _each(_dot, a_ak, v_rs)
    tw = _each(lambda t, a, x: _dot(t, jnp.concatenate([a, x.astype(BF16)], axis=1)).astype(BF16),
               t_inv, a_rs, akv)
    bk_t = _each(lambda b, k: jnp.concatenate([_rs(b, bd), _rs(k, bd)], axis=0).T.astype(BF16),
                 pairs("be"), pairs("ke"))
    fin = _each(lambda ar, bk, w, v: _dot(jnp.concatenate([ar, bk], axis=0),
                                          jnp.concatenate([w, jnp.concatenate([jnp.zeros_like(v), v], axis=1)],
                                                          axis=0)),
                a_r, bk_t, tw, v_rs)
    rh = _each(lambda r, f: r + _ls(f[0:PW, 0:PW]), r_t, fin)
    y0 = _each(lambda f: _ls(f[0:PW, PW:]), fin)
    m = _each(lambda pc, f: jnp.where(eye, pc, 0.0) + f[PW:, 0:PW], pairs("pc"), fin)
    nn = [f[PW:, PW:] for f in fin]
    out = list(zip(y0, rh, m, nn))
    return [out[i * N_PAIR:(i + 1) * N_PAIR] for i in range(len(chunks))]


def _head_sum(x, ones_bd):
    return _dot_lhs_f32(x, ones_bd)


def _wkv_kernel(nsub, r_f, k_f, v_f, lo_f, r_b, k_b, v_b, lo_b, pvec, dec_up, iclr_up, g_up, s0,
                yf_ref, yb_ref, z_ref, gate_ref, sfin_ref,
                zst):
    c = pl.program_id(2)
    nsteps = pl.num_programs(2)

    @pl.when(c == 0)
    def _():
        zst[...] = s0[...]

    masks = _block_masks(PW)
    ones_bd = jnp.where(_block_masks(GW)[0], 1.0, 0.0).astype(BF16)
    pv = pvec[...]
    k_k, k_a, r_k = pv[0:1], pv[1:2], pv[2:3]
    w0 = (pv[3:4], pv[4:5])
    a0 = (pv[5:6], pv[6:7])
    sig_scale = math.exp(-0.5)

    def streams(r_ref, k_ref, v_ref, lo_ref, d):
        r, k, v, lo = r_ref[...], k_ref[...], v_ref[...], lo_ref[...]
        wd = jnp.tanh(lo[:, 0:LORA_PAD])
        ad = lo[:, LORA_PAD:2 * LORA_PAD]
        gd = lo[:, 2 * LORA_PAD:]
        kk = k * k_k
        kkn = kk * lax.rsqrt(jnp.maximum(_head_sum(kk * kk, ones_bd), KK_EPS * KK_EPS))
        logw = -sig_scale * jax.nn.sigmoid(w0[d] + _dot(wd, dec_up[d]))
        aic = jax.nn.sigmoid(a0[d] + _dot(ad, iclr_up[d]))
        kd = k * (1.0 + (aic - 1.0) * k_a)
        return r, k, v, kkn, logw, aic, kd, ad, gd

    r, k, v, kkn, logw, aic, kd, ad, gd = streams(r_f, k_f, v_f, lo_f, 0)
    aic_o = jax.nn.sigmoid(a0[1] + _dot(ad, iclr_up[1]))
    kd_o = k * (1.0 + (aic_o - 1.0) * k_a)
    z_ref[...] = _head_sum(r * (kd + kd_o) * r_k, ones_bd) * v
    gate_ref[...] = _dot(jax.nn.sigmoid(gd), g_up[...])
    chains = [tuple(t[j * CHUNK:(j + 1) * CHUNK] for t in (r, kd, v, kkn, aic, logw)) + (False,)
              for j in range(nsub)]
    r, k, v, kkn, logw, aic, kd, ad, gd = streams(r_b, k_b, v_b, lo_b, 1)
    chains += [tuple(t[j * CHUNK:(j + 1) * CHUNK] for t in (r, kd, v, kkn, aic, logw)) + (True,)
               for j in range(nsub)]
    affine = _chunk_affine(chains, masks)
    fwd, bwd = affine[:nsub], affine[nsub:]

    for d, (steps, y_ref) in enumerate(((list(range(nsub)), yf_ref), (list(reversed(range(nsub))), yb_ref))):
        affine_d = fwd if d == 0 else bwd
        z = [zst[d, p] for p in range(N_PAIR)]
        for j in steps:
            mz = [_dot(jnp.concatenate([affine_d[j][p][2], affine_d[j][p][1]], axis=0), z[p])
                  for p in range(N_PAIR)]
            y_ref[j * CHUNK:(j + 1) * CHUNK, :] = jnp.concatenate(
                [affine_d[j][p][0] + mz[p][PW:] for p in range(N_PAIR)], axis=1)
            z = [mz[p][0:PW] + affine_d[j][p][3] for p in range(N_PAIR)]
        for p in range(N_PAIR):
            zst[d, p] = z[p]

    @pl.when(c == nsteps - 1)
    def _():
        sfin_ref[...] = zst[...]


def _wkv(px, lw, s0, nsub):
    b, l, _ = px.shape
    ng = lw["ng"]
    d_rwkv = ng * GW
    t = nsub * CHUNK
    nsteps = l // t
    lo_blk = 3 * d_rwkv // (4 * LORA_PAD)
    lo_w = 4 * LORA_PAD

    def main_specs(blk):
        return [pl.BlockSpec((None, t, GW), lambda bi, g, c, s=s: (bi, blk(c), s * ng + g)) for s in range(3)] + [
            pl.BlockSpec((None, t, lo_w), lambda bi, g, c: (bi, blk(c), lo_blk))]

    fblk = lambda c: c
    bblk = lambda c: nsteps - 1 - c
    in_specs = (main_specs(fblk) + main_specs(bblk) + [
        pl.BlockSpec((8, GW), lambda bi, g, c: (0, g)),
        pl.BlockSpec((2, LORA_PAD, GW), lambda bi, g, c: (0, 0, g)),
        pl.BlockSpec((2, LORA_PAD, GW), lambda bi, g, c: (0, 0, g)),
        pl.BlockSpec((2 * LORA_PAD, GW), lambda bi, g, c: (0, g)),
        pl.BlockSpec((None, 2, N_PAIR, PW, PW), lambda bi, g, c: (bi, 0, g, 0, 0)),
    ])
    yspec = lambda blk: pl.BlockSpec((None, t, GW), lambda bi, g, c: (bi, blk(c), g))
    out_specs = [yspec(fblk), yspec(bblk), yspec(fblk), yspec(fblk),
                 pl.BlockSpec((None, 2, N_PAIR, PW, PW), lambda bi, g, c: (bi, 0, g, 0, 0))]
    ysh = jax.ShapeDtypeStruct((b, l, d_rwkv), F32)
    return pl.pallas_call(
        functools.partial(_wkv_kernel, nsub),
        grid=(b, ng, nsteps),
        in_specs=in_specs,
        out_specs=out_specs,
        out_shape=[ysh, ysh, ysh, ysh, jax.ShapeDtypeStruct(s0.shape, F32)],
        scratch_shapes=[pltpu.VMEM((2, N_PAIR, PW, PW), F32)],
        compiler_params=_cparams(("parallel", "parallel", "arbitrary")),
        name="wkv",
    )(*([px] * 8), lw["pvec"], lw["dec_up"], lw["iclr_up"], lw["g_up"], s0)


def _dft_ch_kernel(u_ref, t_ref, o_ref):
    u = u_ref[...]
    hi = u.astype(BF16)
    lo = (u - hi.astype(F32)).astype(BF16)
    tab = t_ref[...]
    o_ref[...] = (_dot(hi, tab) + _dot(lo, tab)).astype(BF16)


def _dft_channels(px, tabs, ft_blk, d_f):
    b, l, _ = px.shape
    tm = min(512, l)
    nt = l // tm
    return pl.pallas_call(
        _dft_ch_kernel,
        grid=(b, 2, nt),
        in_specs=[pl.BlockSpec((None, tm, d_f), lambda bi, s, i: (bi, i, ft_blk)),
                  pl.BlockSpec((None, d_f, d_f), lambda bi, s, i: (s, 0, 0))],
        out_specs=pl.BlockSpec((tm, d_f), lambda bi, s, i: (s * nt + i, bi)),
        out_shape=jax.ShapeDtypeStruct((2 * l, b * d_f), BF16),
        compiler_params=_cparams(("parallel", "parallel", "parallel")),
        name="dft_channels",
    )(px, tabs)


def _dft_seq_kernel(w_ref, u_ref, o_ref, acc_ref):
    kk = pl.program_id(1)

    @pl.when(kk == 0)
    def _():
        acc_ref[...] = jnp.zeros_like(acc_ref)

    acc_ref[...] += jnp.dot(w_ref[...], u_ref[...], preferred_element_type=F32)

    @pl.when(kk == pl.num_programs(1) - 1)
    def _():
        o_ref[...] = acc_ref[...]


def _dft_seq(wtab, uu):
    l, k2 = wtab.shape
    n = uu.shape[1]
    tm = min(1024, l)
    tk = min(2048, k2)
    return pl.pallas_call(
        _dft_seq_kernel,
        grid=(l // tm, k2 // tk),
        in_specs=[pl.BlockSpec((tm, tk), lambda i, kk: (i, kk)),
                  pl.BlockSpec((tk, n), lambda i, kk: (kk, 0))],
        out_specs=pl.BlockSpec((tm, n), lambda i, kk: (i, 0)),
        out_shape=jax.ShapeDtypeStruct((l, n), F32),
        scratch_shapes=[pltpu.VMEM((tm, n), F32)],
        compiler_params=_cparams(("parallel", "arbitrary")),
        name="dft_seq",
    )(wtab, uu)


def _seq_table(l):
    f = math.gcd(l, 128)

    def thin(n, stride):
        p = lax.broadcasted_iota(jnp.int32, (l, n), 0)
        q = lax.broadcasted_iota(jnp.int32, (l, n), 1) * stride
        ang = ((p * q) % l).astype(F32) * (2.0 * math.pi / l)
        return jnp.cos(ang), jnp.sin(ang)

    c1, s1 = (t[:, :, None] for t in thin(l // f, f))
    c2, s2 = (t[:, None, :] for t in thin(f, 1))
    return jnp.concatenate([(c1 * c2 - s1 * s2).reshape(l, l), (-(s1 * c2 + c1 * s2)).reshape(l, l)],
                           axis=1).astype(BF16)


def _channel_tables(l, d_f):
    cc = lax.broadcasted_iota(jnp.int32, (d_f, d_f), 0)
    qq = lax.broadcasted_iota(jnp.int32, (d_f, d_f), 1)
    a2 = (((cc % HEAD) * (qq % HEAD)) % HEAD).astype(F32) * (2.0 * math.pi / HEAD)
    same = (cc // HEAD) == (qq // HEAD)
    scale = 1.0 / math.sqrt(l * HEAD)
    return jnp.stack([jnp.where(same, jnp.cos(a2), 0.0), jnp.where(same, jnp.sin(a2), 0.0)]) * scale


def _dot3(a, b):
    ah = a.astype(BF16)
    al = (a - ah.astype(F32)).astype(BF16)
    bh = b.astype(BF16)
    bl = (b - bh.astype(F32)).astype(BF16)
    return _dot(ah, bh) + (_dot(ah, bl) + _dot(al, bh))


FFT_SUB = 8


def _fft_stage1_kernel(*refs):
    u_refs, (ct_ref, t2_ref, tw_ref, o_ref) = refs[:-4], refs[-4:]
    i = pl.program_id(1)
    l2 = t2_ref.shape[0] // 2
    for j in range(FFT_SUB):
        u = jnp.concatenate([r[pl.ds(i * FFT_SUB + j, l2, stride=FFT_L1), :] for r in u_refs], axis=1)
        z = jnp.concatenate([_dot3(u, ct_ref[0]), -_dot3(u, ct_ref[1])], axis=0)
        bm = _dot3(t2_ref[...], z)
        br, bi = bm[0:l2], bm[l2:]
        tw = tw_ref[j]
        cs, sn = tw[:, 0:1], tw[:, 1:2]
        comp = (br * cs + bi * sn, bi * cs - br * sn)
        for c in range(2):
            for pb in range(l2 // FFT_SUB):
                o_ref[pb, c, j * FFT_SUB:(j + 1) * FFT_SUB, :] = comp[c][pb * FFT_SUB:(pb + 1) * FFT_SUB]


def _fft_stage2_kernel(t1_ref, b_ref, o_ref):
    x = jnp.concatenate([b_ref[0], b_ref[1]], axis=0)
    y = _dot3(t1_ref[...], x)
    for p1 in range(o_ref.shape[0]):
        o_ref[p1] = y[p1 * FFT_SUB:(p1 + 1) * FFT_SUB]


def _fft_tables(l):
    l1, l2 = FFT_L1, l // FFT_L1

    def cs(n, m, period):
        p = lax.broadcasted_iota(jnp.int32, (n, m), 0)
        q = lax.broadcasted_iota(jnp.int32, (n, m), 1)
        ang = ((p * q) % period).astype(F32) * (2.0 * math.pi / period)
        return jnp.cos(ang), jnp.sin(ang)

    c2, s2 = cs(l2, l2, l2)
    t2 = jnp.concatenate([jnp.concatenate([c2, s2], axis=1), jnp.concatenate([-s2, c2], axis=1)], axis=0)
    tw = jnp.stack(cs(l1, l2, l), axis=-1)
    c1, s1 = cs(l1, l1, l1)
    t1 = jnp.kron(jnp.concatenate([c1, s1], axis=1), jnp.eye(FFT_SUB, dtype=F32))
    return t2, tw, t1


def _fourier_fft(px, ctab, tabs, ft_blk, d_f):
    b, l, n = px.shape
    l1, l2 = FFT_L1, l // FFT_L1
    npb = l2 // FFT_SUB
    nslab = d_f // LANES
    t2, tw, t1 = tabs
    stage1 = pl.pallas_call(
        _fft_stage1_kernel,
        grid=(b, l1 // FFT_SUB),
        in_specs=[pl.BlockSpec((None, l, LANES), lambda bi, i, k=k: (bi, 0, ft_blk * nslab + k))
                  for k in range(nslab)] + [
                  pl.BlockSpec(ctab.shape, lambda bi, i: (0, 0, 0)),
                  pl.BlockSpec(t2.shape, lambda bi, i: (0, 0)),
                  pl.BlockSpec((FFT_SUB, l2, 2), lambda bi, i: (i, 0, 0))],
        out_specs=pl.BlockSpec((None, npb, 2, FFT_SUB * FFT_SUB, d_f), lambda bi, i: (bi, 0, 0, i, 0)),
        out_shape=jax.ShapeDtypeStruct((b, npb, 2, l1 * FFT_SUB, d_f), F32),
        compiler_params=_cparams(("parallel", "arbitrary")),
        name="fft_stage1",
    )(*([px] * nslab), ctab, t2, tw)
    y = pl.pallas_call(
        _fft_stage2_kernel,
        grid=(b, npb),
        in_specs=[pl.BlockSpec(t1.shape, lambda bi, j: (0, 0)),
                  pl.BlockSpec((None, None, 2, l1 * FFT_SUB, d_f), lambda bi, j: (bi, j, 0, 0, 0))],
        out_specs=pl.BlockSpec((None, l1, None, FFT_SUB, d_f), lambda bi, j: (bi, 0, j, 0, 0)),
        out_shape=jax.ShapeDtypeStruct((b, l1, npb, FFT_SUB, d_f), F32),
        compiler_params=_cparams(("parallel", "parallel")),
        name="fft_stage2",
    )(t1, stage1)
    return y.reshape(b, l, d_f)


def _fourier_dense(px, wtab, ctab, ft_blk, d_f):
    b, l, _ = px.shape
    y = _dft_seq(wtab, _dft_channels(px, ctab, ft_blk, d_f))
    return y.reshape(l, b, d_f).transpose(1, 0, 2)


def _mix_out_kernel(period, yf_ref, yb_ref, z_ref, gate_ref, cg_ref, cx_ref, cb_ref, ft_ref, x_ref,
                    ga_ref, lnw_ref, lnb_ref, cw_ref, w_ref, o_ref, mix_ref):
    d_rwkv = yf_ref.shape[1]
    d_conv = cg_ref.shape[1]
    tm = yf_ref.shape[0]
    r = lax.broadcasted_iota(jnp.int32, (GW, GW), 0)
    c = lax.broadcasted_iota(jnp.int32, (GW, GW), 1)
    ones_bd = jnp.where((r // HEAD) == (c // HEAD), 1.0, 0.0).astype(BF16)
    inv_n = 1.0 / HEAD
    for s in range(d_rwkv // GW):
        sl = slice(s * GW, (s + 1) * GW)
        y = yf_ref[:, sl] + yb_ref[:, sl]
        mu = _dot_lhs_f32(y, ones_bd) * inv_n
        dlt = y - mu
        var = _dot_lhs_f32(dlt * dlt, ones_bd) * inv_n
        yn = dlt * lax.rsqrt(var + GN_EPS) * lnw_ref[:, sl] + lnb_ref[:, sl] + z_ref[:, sl]
        mix_ref[:, sl] = (yn * gate_ref[:, sl]).astype(BF16)
    u = cg_ref[...] * cx_ref[...]
    rowid = lax.broadcasted_iota(jnp.int32, u.shape, 0) % period
    up = jnp.where(rowid == 0, 0.0, pltpu.roll(u, 1, 0))
    dn = jnp.where(rowid == period - 1, 0.0, pltpu.roll(u, tm - 1, 0))
    cw = cw_ref[...]
    conv = cb_ref[...] * (up * cw[0:1] + u * cw[1:2] + dn * cw[2:3])
    mix_ref[:, d_rwkv:d_rwkv + d_conv] = conv.astype(BF16)
    mix_ref[:, d_rwkv + d_conv:] = ft_ref[...].astype(BF16)
    o_ref[...] = x_ref[...] + ga_ref[...] * jnp.dot(mix_ref[...], w_ref[...], preferred_element_type=F32)


def _mix_out(yf, yb, z, gate, px, ft, x, ga, lw, period, conv_blk):
    b, l, d = x.shape
    d_rwkv = yf.shape[2]
    d_conv = lw["conv_w"].shape[1]
    d_f = ft.shape[2]
    tm = min(256, l)
    assert tm % period == 0
    yspec = pl.BlockSpec((None, tm, d_rwkv), lambda bi, i: (bi, i, 0))
    cspec = lambda off: pl.BlockSpec((None, tm, d_conv), lambda bi, i: (bi, i, conv_blk + off))
    return pl.pallas_call(
        functools.partial(_mix_out_kernel, period),
        grid=(b, l // tm),
        in_specs=[yspec, yspec, yspec, yspec, cspec(0), cspec(1), cspec(2),
                  pl.BlockSpec((None, tm, d_f), lambda bi, i: (bi, i, 0)),
                  pl.BlockSpec((None, tm, d), lambda bi, i: (bi, i, 0)),
                  pl.BlockSpec((None, 1, d), lambda bi, i: (bi, 0, 0)),
                  pl.BlockSpec((1, d_rwkv), lambda bi, i: (0, 0)),
                  pl.BlockSpec((1, d_rwkv), lambda bi, i: (0, 0)),
                  pl.BlockSpec((3, d_conv), lambda bi, i: (0, 0)),
                  pl.BlockSpec(lw["w_out"].shape, lambda bi, i: (0, 0))],
        out_specs=pl.BlockSpec((None, tm, d), lambda bi, i: (bi, i, 0)),
        out_shape=jax.ShapeDtypeStruct((b, l, d), F32),
        scratch_shapes=[pltpu.VMEM((tm, lw["w_out"].shape[0]), BF16)],
        compiler_params=_cparams(("parallel", "parallel")),
        name="mix_out",
    )(yf, yb, z, gate, px, px, px, ft, x, ga, lw["ln_w"], lw["ln_b"], lw["conv_w"], lw["w_out"])


def _ffn_kernel(final, x_ref, g_ref, sh_ref, sc_ref, ga_ref, wg_ref, wu_ref, wd_ref, gf_ref, o_ref,
                h_ref, acc_ref):
    j = pl.program_id(2)

    @pl.when(j == 0)
    def _():
        h_ref[...] = _rms_mod(x_ref[...], g_ref[...], sh_ref[...], sc_ref[...]).astype(BF16)
        acc_ref[...] = jnp.zeros_like(acc_ref)

    h = h_ref[...]
    a = jnp.dot(h, wg_ref[...], preferred_element_type=F32)
    u = jnp.dot(h, wu_ref[...], preferred_element_type=F32)
    t = (a * jax.nn.sigmoid(a)) * u
    acc_ref[...] += jnp.dot(t.astype(BF16), wd_ref[...], preferred_element_type=F32)

    @pl.when(j == pl.num_programs(2) - 1)
    def _():
        xn = x_ref[...] + ga_ref[...] * acc_ref[...]
        if final:
            ms = jnp.mean(xn * xn, axis=-1, keepdims=True)
            xn = xn * lax.rsqrt(ms + RMS_EPS) * gf_ref[...]
        o_ref[...] = xn


def _ffn(x, g, shift, scale, ga, wg, wu, wd, g_final, final):
    b, l, d = x.shape
    tf = COL_TILE
    ff = wg.shape[0] * tf
    tm = min(512, l)
    vec = pl.BlockSpec((None, 1, d), lambda bi, i, j: (bi, 0, 0))
    gspec = pl.BlockSpec((1, d), lambda bi, i, j: (0, 0))
    return pl.pallas_call(
        functools.partial(_ffn_kernel, final),
        grid=(b, l // tm, ff // tf),
        in_specs=[pl.BlockSpec((None, tm, d), lambda bi, i, j: (bi, i, 0)),
                  gspec, vec, vec, vec,
                  pl.BlockSpec((None, d, tf), lambda bi, i, j: (j, 0, 0)),
                  pl.BlockSpec((None, d, tf), lambda bi, i, j: (j, 0, 0)),
                  pl.BlockSpec((tf, d), lambda bi, i, j: (j, 0)),
                  gspec],
        out_specs=pl.BlockSpec((None, tm, d), lambda bi, i, j: (bi, i, 0)),
        out_shape=jax.ShapeDtypeStruct((b, l, d), F32),
        scratch_shapes=[pltpu.VMEM((tm, d), BF16), pltpu.VMEM((tm, d), F32)],
        compiler_params=_cparams(("parallel", "parallel", "arbitrary")),
        name="ffn",
    )(x, g, shift, scale, ga, wg, wu, wd, g_final)


def _layer_weights(i, w_in, rw_shift, dec_w0, dec_up, iclr_a0, iclr_up, k_k, k_a, r_k, ln_w, ln_b, g_up,
                   conv_w, w_out, w_gate, w_up, w_down, dims):
    d_rwkv, d_conv, d_f, dl, il, gl = dims
    ng = d_rwkv // GW
    wd0 = 3 * d_rwkv
    ad0 = wd0 + dl
    gd0 = ad0 + il
    rest0 = gd0 + gl

    def pad_lora(m, stop):
        padl = jnp.zeros((m.shape[0], LORA_PAD - dl), m.dtype)
        padi = jnp.zeros((m.shape[0], LORA_PAD - il), m.dtype)
        return jnp.concatenate([m[:, :ad0], padl, m[:, ad0:gd0], padi, m[:, gd0:stop]], axis=1)

    col_tiles = lambda m: m.reshape(m.shape[0], m.shape[1] // COL_TILE, COL_TILE).transpose(1, 0, 2)
    pad_rows = lambda m, n: jnp.concatenate([m, jnp.zeros(m.shape[:-2] + (n - m.shape[-2], m.shape[-1]), m.dtype)], -2)
    zero = jnp.zeros((d_rwkv,), F32)
    pvec = jnp.stack([k_k[i], k_a[i], r_k[i].reshape(-1), dec_w0[i, 0], dec_w0[i, 1],
                      iclr_a0[i, 0], iclr_a0[i, 1], zero])
    return dict(
        ng=ng,
        w_in=col_tiles(pad_lora(w_in[i].astype(BF16), w_in.shape[2])),
        mix=pad_lora(rw_shift[i], rest0),
        pvec=pvec,
        dec_up=pad_rows(dec_up[i], LORA_PAD).astype(BF16),
        iclr_up=pad_rows(iclr_up[i], LORA_PAD).astype(BF16),
        g_up=g_up[i].astype(BF16),
        ln_w=ln_w[i][None, :], ln_b=ln_b[i][None, :],
        conv_w=conv_w[i],
        w_out=w_out[i].astype(BF16),
        w_gate=col_tiles(w_gate[i].astype(BF16)), w_up=col_tiles(w_up[i].astype(BF16)),
        w_down=w_down[i].astype(BF16),
    )


def kernel(x, c, ctx, c_ctx, w_mod, b_mod, norm_mix, w_in, rw_shift, dec_w0, dec_up, iclr_a0, iclr_up,
           k_k, k_a, r_k, ln_w, ln_b, g_up, conv_w, w_out, norm_ffn, w_gate, w_up, w_down, norm_final):
    b, l, d = x.shape
    lc = ctx.shape[1]
    depth = w_mod.shape[0]
    d_rwkv = k_k.shape[1]
    d_conv = conv_w.shape[2]
    dl, il, gl = dec_up.shape[2], iclr_up.shape[2], g_up.shape[1]
    d_f = w_out.shape[1] - d_rwkv - d_conv
    assert dl <= LORA_PAD and il <= LORA_PAD and gl == 2 * LORA_PAD and d_rwkv % GW == 0
    assert l % GRID_W == 0 and l % CHUNK == 0 and lc % CHUNK == 0
    nsub_x = math.gcd(l // CHUNK, WKV_SUB)
    nsub_c = math.gcd(lc // CHUNK, WKV_SUB)
    ng = d_rwkv // GW
    n_rw = 3 * d_rwkv + 4 * LORA_PAD
    conv_blk = n_rw // d_conv
    ft_blk = (n_rw + 3 * d_conv) // d_f
    assert conv_blk * d_conv == n_rw and ft_blk * d_f == n_rw + 3 * d_conv

    cvec = jnp.concatenate([c, c_ctx[None, :], jnp.zeros((8 - b - 1, d), F32)], axis=0)
    mods = _adaln(cvec, w_mod, b_mod[:, None, :])

    def mod(i, j, ctx_rows):
        m = mods[i, :, j * d:(j + 1) * d]
        if ctx_rows:
            return jnp.broadcast_to(m[b:b + 1], (b, d))[:, None, :]
        return m[:b][:, None, :]

    fft_x = l % (FFT_L1 * 128) == 0
    wtab_c, ctab_c = _seq_table(lc), _channel_tables(lc, d_f)
    ctab_x = _channel_tables(l, d_f)
    if fft_x:
        ftab_x = _fft_tables(l)
    else:
        wtab_x = _seq_table(l)
    dims = (d_rwkv, d_conv, d_f, dl, il, gl)
    s_zero = jnp.zeros((b, 2, ng * N_PAIR, PW, PW), F32)
    gfin = norm_final[None, :]
    xc = ctx
    for i in range(depth):
        lw = _layer_weights(i, w_in, rw_shift, dec_w0, dec_up, iclr_a0, iclr_up, k_k, k_a, r_k, ln_w, ln_b,
                            g_up, conv_w, w_out, w_gate, w_up, w_down, dims)
        gmix = norm_mix[i][None, :]
        gffn = norm_ffn[i][None, :]
        last = i == depth - 1
        pc = _in_proj(xc, gmix, mod(i, 0, True), mod(i, 1, True), lw["w_in"], lw["mix"])
        px = _in_proj(x, gmix, mod(i, 0, False), mod(i, 1, False), lw["w_in"], lw["mix"])
        yfc, ybc, zc, gc, s_ctx = _wkv(pc, lw, s_zero, nsub_c)
        yfx, ybx, zx, gx, _ = _wkv(px, lw, s_ctx, nsub_x)
        if fft_x:
            ftx = _fourier_fft(px, ctab_x, ftab_x, ft_blk, d_f)
        else:
            ftx = _fourier_dense(px, wtab_x, ctab_x, ft_blk, d_f)
        x = _mix_out(yfx, ybx, zx, gx, px, ftx, x, mod(i, 2, False), lw, GRID_W, conv_blk)
        x = _ffn(x, gffn, mod(i, 3, False), mod(i, 4, False), mod(i, 5, False),
                 lw["w_gate"], lw["w_up"], lw["w_down"], gfin, last)
        if not last:
            ftc = _fourier_dense(pc, wtab_c, ctab_c, ft_blk, d_f)
            xc = _mix_out(yfc, ybc, zc, gc, pc, ftc, xc, mod(i, 2, True), lw, lc, conv_blk)
            xc = _ffn(xc, gffn, mod(i, 3, True), mod(i, 4, True), mod(i, 5, True),
                      lw["w_gate"], lw["w_up"], lw["w_down"], gfin, False)
    return x
```

```python
import functools
import math

import jax
import jax.numpy as jnp
from jax import lax
from jax.experimental import pallas as pl
from jax.experimental.pallas import tpu as pltpu

F32 = jnp.float32
BF16 = jnp.bfloat16

HEAD = 64
GROUP_HEADS = 4
GW = GROUP_HEADS * HEAD
PAIR_HEADS = 2
PW = PAIR_HEADS * HEAD
N_PAIR = GW // PW
CHUNK = 64
WKV_SUB = 8
HALO = 16
LANES = 128
COL_TILE = 512
LORA_PAD = 128
GRID_W = 64
FFT_L1 = 64
N_MOD = 6
RMS_EPS = 1e-6
GN_EPS = 64e-5
KK_EPS = 1e-12
VMEM_LIMIT = 56 * 1024 * 1024


def _cparams(sem):
    return pltpu.CompilerParams(dimension_semantics=sem, vmem_limit_bytes=VMEM_LIMIT)


def _dot(a, b, nt=False):
    dn = (((1,), (1,)), ((), ())) if nt else (((1,), (0,)), ((), ()))
    return lax.dot_general(a.astype(BF16), b.astype(BF16), dn, preferred_element_type=F32)


def _split2(a):
    hi = a.astype(BF16)
    return hi, (a - hi.astype(F32)).astype(BF16)


def _dot_lhs_f32(a, b_exact):
    hi, lo = _split2(a)
    return _dot(hi, b_exact) + _dot(lo, b_exact)


def _dot_rhs_f32(a_exact, b):
    hi, lo = _split2(b)
    return _dot(a_exact, hi) + _dot(a_exact, lo)


def _rms_mod(x, g, shift, scale):
    ms = jnp.mean(x * x, axis=-1, keepdims=True)
    return (x * lax.rsqrt(ms + RMS_EPS) * g) * (1.0 + scale) + shift


def _adaln_kernel(c_ref, w_ref, b_ref, o_ref):
    c = c_ref[...]
    s = c * jax.nn.sigmoid(c)
    o_ref[...] = _dot(s, w_ref[...]) + b_ref[...]


def _adaln(cvec, w_mod, b_mod):
    nl, d, n = w_mod.shape
    tn = 1024
    return pl.pallas_call(
        _adaln_kernel,
        grid=(nl, n // tn),
        in_specs=[pl.BlockSpec((8, d), lambda l, j: (0, 0)),
                  pl.BlockSpec((None, d, tn), lambda l, j: (l, 0, j)),
                  pl.BlockSpec((None, 1, tn), lambda l, j: (l, 0, j))],
        out_specs=pl.BlockSpec((None, 8, tn), lambda l, j: (l, 0, j)),
        out_shape=jax.ShapeDtypeStruct((nl, 8, n), F32),
        compiler_params=_cparams(("parallel", "parallel")),
        name="adaln",
    )(cvec, w_mod, b_mod)


def _in_proj_kernel(n_mix_tiles, x_ref, xp_ref, xn_ref, g_ref, sh_ref, sc_ref, w_ref, mix_ref, o_ref, of_ref,
                    h_ref):
    i = pl.program_id(1)
    j = pl.program_id(2)
    tm = x_ref.shape[0]

    @pl.when(pl.program_id(2) == 0)
    def _():
        g, sh, sc = g_ref[...], sh_ref[...], sc_ref[...]
        prev = jnp.where(i > 0, _rms_mod(xp_ref[...], g, sh, sc), 0.0)
        nxt = jnp.where(i < pl.num_programs(1) - 1, _rms_mod(xn_ref[...], g, sh, sc), 0.0)
        h_ref[0:HALO, :] = prev.astype(BF16)
        h_ref[HALO:HALO + tm, :] = _rms_mod(x_ref[...], g, sh, sc).astype(BF16)
        h_ref[HALO + tm:, :] = nxt.astype(BF16)

    @pl.when(j < n_mix_tiles)
    def _():
        p = jnp.dot(h_ref[...], w_ref[...], preferred_element_type=F32)
        mix = mix_ref[...]
        up = pltpu.roll(p, 1, 0)[HALO:HALO + tm]
        dn = pltpu.roll(p, tm + 2 * HALO - 1, 0)[HALO:HALO + tm]
        o_ref[...] = (up * mix[0:1, :] + p[HALO:HALO + tm] * mix[1:2, :] + dn * mix[2:3, :]).astype(o_ref.dtype)

    last = pl.num_programs(2) - 1

    @pl.when((j >= n_mix_tiles) & (j < last))
    def _():
        o_ref[...] = jnp.dot(h_ref[HALO:HALO + tm, :], w_ref[...],
                             preferred_element_type=F32).astype(o_ref.dtype)

    @pl.when(j == last)
    def _():
        of_ref[...] = jnp.dot(h_ref[HALO:HALO + tm, :], w_ref[...], preferred_element_type=F32)


def _in_proj(x, g, shift, scale, w, mix):
    b, l, d = x.shape
    tn = COL_TILE
    n = w.shape[0] * tn
    tm = min(1024, l)
    n_mix_tiles = mix.shape[1] // tn
    assert n_mix_tiles * tn == mix.shape[1]
    hb = tm // HALO
    nhb = l // HALO
    vec = pl.BlockSpec((None, 1, d), lambda bi, i, j: (bi, 0, 0))
    return pl.pallas_call(
        functools.partial(_in_proj_kernel, n_mix_tiles),
        grid=(b, l // tm, n // tn),
        in_specs=[pl.BlockSpec((None, tm, d), lambda bi, i, j: (bi, i, 0)),
                  pl.BlockSpec((None, HALO, d), lambda bi, i, j: (bi, jnp.maximum(i * hb - 1, 0), 0)),
                  pl.BlockSpec((None, HALO, d), lambda bi, i, j: (bi, jnp.minimum((i + 1) * hb, nhb - 1), 0)),
                  pl.BlockSpec((1, d), lambda bi, i, j: (0, 0)),
                  vec, vec,
                  pl.BlockSpec((None, d, tn), lambda bi, i, j: (j, 0, 0)),
                  pl.BlockSpec((3, tn), lambda bi, i, j: (0, jnp.minimum(j, n_mix_tiles - 1)))],
        out_specs=[pl.BlockSpec((None, tm, tn), lambda bi, i, j: (bi, i, jnp.minimum(j, n // tn - 2))),
                   pl.BlockSpec((None, tm, tn), lambda bi, i, j: (bi, i, 0))],
        out_shape=[jax.ShapeDtypeStruct((b, l, n - tn), BF16), jax.ShapeDtypeStruct((b, l, tn), F32)],
        scratch_shapes=[pltpu.VMEM((tm + 2 * HALO, d), BF16)],
        compiler_params=_cparams(("parallel", "parallel", "arbitrary")),
        name="in_proj",
    )(x, x, x, g, shift, scale, w, mix)


def _block_masks(width):
    r = lax.broadcasted_iota(jnp.int32, (width, width), 0)
    c = lax.broadcasted_iota(jnp.int32, (width, width), 1)
    return (r // HEAD) == (c // HEAD), r % HEAD, c % HEAD, r == c


def _rs(x, bd):
    return jnp.where(bd, jnp.concatenate([x] * PAIR_HEADS, axis=0), 0.0)


def _ls(x_rs):
    out = x_rs[0:CHUNK]
    for h in range(1, PAIR_HEADS):
        out = out + x_rs[h * CHUNK:(h + 1) * CHUNK]
    return out


def _each(f, *lists):
    return [f(*a) for a in zip(*lists)]


def _chunk_affine(chunks, masks):
    bd, tt, ss, eye = masks
    row = lax.broadcasted_iota(jnp.int32, (CHUNK, CHUNK), 0)
    col = lax.broadcasted_iota(jnp.int32, (CHUNK, CHUNK), 1)
    tri = {rev: jnp.where((row <= col) if rev else (row >= col), 1.0, 0.0).astype(BF16) for rev in (False, True)}
    strict = {False: bd & (ss < tt), True: bd & (ss > tt)}
    r2 = lax.broadcasted_iota(jnp.int32, (PW, 2 * PW), 0)
    c2 = lax.broadcasted_iota(jnp.int32, (PW, 2 * PW), 1)
    bd2 = (r2 // HEAD) == ((c2 % PW) // HEAD)
    incl2 = {False: bd2 & (c2 % HEAD <= r2 % HEAD), True: bd2 & (c2 % HEAD >= r2 % HEAD)}

    crev = [ch[6] for ch in chunks]
    logw = [ch[5] for ch in chunks]
    cum = _each(lambda rv, lw: _dot_rhs_f32(tri[rv], lw), crev, logw)
    total = _each(lambda rv, cm: cm[0:1] if rv else cm[CHUNK - 1:CHUNK], crev, cum)
    p_inv = _each(lambda cm: jnp.exp(-cm), cum)
    p_end = _each(lambda tot, cm: jnp.exp(tot - cm), total, cum)
    bvec = [ch[3] * ch[4] for ch in chunks]
    wide = dict(
        a=_each(lambda ch, cm, lw: -ch[3] * jnp.exp(cm - lw), chunks, cum, logw),
        r=_each(lambda ch, cm: ch[0] * jnp.exp(cm), chunks, cum),
        b=_each(lambda b, p: b * p, bvec, p_inv),
        k=_each(lambda ch, p: ch[1] * p, chunks, p_inv),
        v=[ch[2] for ch in chunks],
        be=_each(lambda b, p: b * p, bvec, p_end),
        ke=_each(lambda ch, p: ch[1] * p, chunks, p_end),
        pc=_each(jnp.exp, total),
    )

    def pairs(name):
        return [x[:, p * PW:(p + 1) * PW] for x in wide[name] for p in range(N_PAIR)]

    rev = [rv for rv in crev for _ in range(N_PAIR)]
    r_t = pairs("r")
    a_rs = _each(lambda x: _rs(x, bd).astype(BF16), pairs("a"))
    r_rs = _each(lambda x: _rs(x, bd).astype(BF16), r_t)
    b_rs = _each(lambda x: jnp.concatenate([x.astype(BF16)] * PAIR_HEADS, axis=0), pairs("b"))
    k_rs = _each(lambda x: jnp.concatenate([x.astype(BF16)] * PAIR_HEADS, axis=0), pairs("k"))
    v_rs = _each(lambda x: _rs(x, bd).astype(BF16), pairs("v"))
    aa = _each(lambda a, r, b, k: _dot(jnp.concatenate([a, r], axis=0), jnp.concatenate([b, k], axis=0), nt=True),
               a_rs, r_rs, b_rs, k_rs)
    a_ab = _each(lambda rv, x: jnp.where(strict[rv], x[0:PW, 0:PW], 0.0), rev, aa)
    a_ak = _each(lambda rv, x: jnp.where(strict[rv], x[0:PW, PW:], 0.0), rev, aa)
    a_r = _each(lambda rv, x: jnp.where(incl2[rv], x[PW:, :], 0.0).astype(BF16), rev, aa)
    t_inv = _each(lambda a: jnp.where(eye, 1.0, a), a_ab)
    pw = _each(lambda a: _dot(a, a), a_ab)
    for _ in range(int(math.log2(CHUNK)) - 2):
        res = _each(lambda p, t: _dot(p, jnp.concatenate([p, t], axis=1)), pw, t_inv)
        pw = [x[:, 0:PW] for x in res]
        t_inv = _each(lambda t, x: t + x[:, PW:], t_inv, res)
    t_inv = _each(lambda p, t: (t + _dot(p, t)).astype(BF16), pw, t_inv)
    akv = _each(_dot, a_ak, v_rs)
    tw = _each(lambda t, a, x: _dot(t, jnp.concatenate([a, x.astype(BF16)], axis=1)).astype(BF16),
               t_inv, a_rs, akv)
    bk_t = _each(lambda b, k: jnp.concatenate([_rs(b, bd), _rs(k, bd)], axis=0).T.astype(BF16),
                 pairs("be"), pairs("ke"))
    fin = _each(lambda ar, bk, w, v: _dot(jnp.concatenate([ar, bk], axis=0),
                                          jnp.concatenate([w, jnp.concatenate([jnp.zeros_like(v), v], axis=1)],
                                                          axis=0)),
                a_r, bk_t, tw, v_rs)
    rh = _each(lambda r, f: r + _ls(f[0:PW, 0:PW]), r_t, fin)
    y0 = _each(lambda f: _ls(f[0:PW, PW:]), fin)
    m = _each(lambda pc, f: jnp.where(eye, pc, 0.0) + f[PW:, 0:PW], pairs("pc"), fin)
    nn = [f[PW:, PW:] for f in fin]
    out = list(zip(y0, rh, m, nn))
    return [out[i * N_PAIR:(i + 1) * N_PAIR] for i in range(len(chunks))]


def _head_sum(x, ones_bd):
    return _dot_lhs_f32(x, ones_bd)


def _wkv_kernel(nsub, r_f, k_f, v_f, lo_f, r_b, k_b, v_b, lo_b, pvec, dec_up, iclr_up, g_up, s0,
                yf_ref, yb_ref, z_ref, gate_ref, sfin_ref,
                zst):
    c = pl.program_id(2)
    nsteps = pl.num_programs(2)

    @pl.when(c == 0)
    def _():
        zst[...] = s0[...]

    masks = _block_masks(PW)
    ones_bd = jnp.where(_block_masks(GW)[0], 1.0, 0.0).astype(BF16)
    pv = pvec[...]
    k_k, k_a, r_k = pv[0:1], pv[1:2], pv[2:3]
    w0 = (pv[3:4], pv[4:5])
    a0 = (pv[5:6], pv[6:7])
    sig_scale = math.exp(-0.5)

    def streams(r_ref, k_ref, v_ref, lo_ref, d):
        r, k, v, lo = (ref[...].astype(F32) for ref in (r_ref, k_ref, v_ref, lo_ref))
        wd = jnp.tanh(lo[:, 0:LORA_PAD])
        ad = lo[:, LORA_PAD:2 * LORA_PAD]
        gd = lo[:, 2 * LORA_PAD:]
        kk = k * k_k
        kkn = kk * lax.rsqrt(jnp.maximum(_head_sum(kk * kk, ones_bd), KK_EPS * KK_EPS))
        logw = -sig_scale * jax.nn.sigmoid(w0[d] + _dot(wd, dec_up[d]))
        aic = jax.nn.sigmoid(a0[d] + _dot(ad, iclr_up[d]))
        kd = k * (1.0 + (aic - 1.0) * k_a)
        return r, k, v, kkn, logw, aic, kd, ad, gd

    r, k, v, kkn, logw, aic, kd, ad, gd = streams(r_f, k_f, v_f, lo_f, 0)
    aic_o = jax.nn.sigmoid(a0[1] + _dot(ad, iclr_up[1]))
    kd_o = k * (1.0 + (aic_o - 1.0) * k_a)
    z_ref[...] = _head_sum(r * (kd + kd_o) * r_k, ones_bd) * v
    gate_ref[...] = _dot(jax.nn.sigmoid(gd), g_up[...])
    chains = [tuple(t[j * CHUNK:(j + 1) * CHUNK] for t in (r, kd, v, kkn, aic, logw)) + (False,)
              for j in range(nsub)]
    r, k, v, kkn, logw, aic, kd, ad, gd = streams(r_b, k_b, v_b, lo_b, 1)
    chains += [tuple(t[j * CHUNK:(j + 1) * CHUNK] for t in (r, kd, v, kkn, aic, logw)) + (True,)
               for j in range(nsub)]
    affine = _chunk_affine(chains, masks)
    fwd, bwd = affine[:nsub], affine[nsub:]

    for d, (steps, y_ref) in enumerate(((list(range(nsub)), yf_ref), (list(reversed(range(nsub))), yb_ref))):
        affine_d = fwd if d == 0 else bwd
        z = [zst[d, p] for p in range(N_PAIR)]
        for j in steps:
            mz = [_dot(jnp.concatenate([affine_d[j][p][2], affine_d[j][p][1]], axis=0), z[p])
                  for p in range(N_PAIR)]
            y_ref[j * CHUNK:(j + 1) * CHUNK, :] = jnp.concatenate(
                [affine_d[j][p][0] + mz[p][PW:] for p in range(N_PAIR)], axis=1)
            z = [mz[p][0:PW] + affine_d[j][p][3] for p in range(N_PAIR)]
        for p in range(N_PAIR):
            zst[d, p] = z[p]

    @pl.when(c == nsteps - 1)
    def _():
        sfin_ref[...] = zst[...]


def _wkv(px, lw, s0, nsub):
    b, l, _ = px.shape
    ng = lw["ng"]
    d_rwkv = ng * GW
    t = nsub * CHUNK
    nsteps = l // t
    lo_blk = 3 * d_rwkv // (4 * LORA_PAD)
    lo_w = 4 * LORA_PAD

    def main_specs(blk):
        return [pl.BlockSpec((None, t, GW), lambda bi, g, c, s=s: (bi, blk(c), s * ng + g)) for s in range(3)] + [
            pl.BlockSpec((None, t, lo_w), lambda bi, g, c: (bi, blk(c), lo_blk))]

    fblk = lambda c: c
    bblk = lambda c: nsteps - 1 - c
    in_specs = (main_specs(fblk) + main_specs(bblk) + [
        pl.BlockSpec((8, GW), lambda bi, g, c: (0, g)),
        pl.BlockSpec((2, LORA_PAD, GW), lambda bi, g, c: (0, 0, g)),
        pl.BlockSpec((2, LORA_PAD, GW), lambda bi, g, c: (0, 0, g)),
        pl.BlockSpec((2 * LORA_PAD, GW), lambda bi, g, c: (0, g)),
        pl.BlockSpec((None, 2, N_PAIR, PW, PW), lambda bi, g, c: (bi, 0, g, 0, 0)),
    ])
    yspec = lambda blk: pl.BlockSpec((None, t, GW), lambda bi, g, c: (bi, blk(c), g))
    out_specs = [yspec(fblk), yspec(bblk), yspec(fblk), yspec(fblk),
                 pl.BlockSpec((None, 2, N_PAIR, PW, PW), lambda bi, g, c: (bi, 0, g, 0, 0))]
    ysh = jax.ShapeDtypeStruct((b, l, d_rwkv), F32)
    return pl.pallas_call(
        functools.partial(_wkv_kernel, nsub),
        grid=(b, ng, nsteps),
        in_specs=in_specs,
        out_specs=out_specs,
        out_shape=[ysh, ysh, ysh, ysh, jax.ShapeDtypeStruct(s0.shape, F32)],
        scratch_shapes=[pltpu.VMEM((2, N_PAIR, PW, PW), F32)],
        compiler_params=_cparams(("parallel", "parallel", "arbitrary")),
        name="wkv",
    )(*([px] * 8), lw["pvec"], lw["dec_up"], lw["iclr_up"], lw["g_up"], s0)


def _dft_ch_kernel(u_ref, t_ref, o_ref):
    u = u_ref[...]
    hi = u.astype(BF16)
    lo = (u - hi.astype(F32)).astype(BF16)
    tab = t_ref[...]
    o_ref[...] = (_dot(hi, tab) + _dot(lo, tab)).astype(BF16)


def _dft_channels(px, tabs, ft_blk, d_f):
    b, l, _ = px.shape
    tm = min(512, l)
    nt = l // tm
    return pl.pallas_call(
        _dft_ch_kernel,
        grid=(b, 2, nt),
        in_specs=[pl.BlockSpec((None, tm, d_f), lambda bi, s, i: (bi, i, ft_blk)),
                  pl.BlockSpec((None, d_f, d_f), lambda bi, s, i: (s, 0, 0))],
        out_specs=pl.BlockSpec((tm, d_f), lambda bi, s, i: (s * nt + i, bi)),
        out_shape=jax.ShapeDtypeStruct((2 * l, b * d_f), BF16),
        compiler_params=_cparams(("parallel", "parallel", "parallel")),
        name="dft_channels",
    )(px, tabs)


def _dft_seq_kernel(w_ref, u_ref, o_ref, acc_ref):
    kk = pl.program_id(1)

    @pl.when(kk == 0)
    def _():
        acc_ref[...] = jnp.zeros_like(acc_ref)

    acc_ref[...] += jnp.dot(w_ref[...], u_ref[...], preferred_element_type=F32)

    @pl.when(kk == pl.num_programs(1) - 1)
    def _():
        o_ref[...] = acc_ref[...]


def _dft_seq(wtab, uu):
    l, k2 = wtab.shape
    n = uu.shape[1]
    tm = min(1024, l)
    tk = min(2048, k2)
    return pl.pallas_call(
        _dft_seq_kernel,
        grid=(l // tm, k2 // tk),
        in_specs=[pl.BlockSpec((tm, tk), lambda i, kk: (i, kk)),
                  pl.BlockSpec((tk, n), lambda i, kk: (kk, 0))],
        out_specs=pl.BlockSpec((tm, n), lambda i, kk: (i, 0)),
        out_shape=jax.ShapeDtypeStruct((l, n), F32),
        scratch_shapes=[pltpu.VMEM((tm, n), F32)],
        compiler_params=_cparams(("parallel", "arbitrary")),
        name="dft_seq",
    )(wtab, uu)


def _seq_table(l):
    f = math.gcd(l, 128)

    def thin(n, stride):
        p = lax.broadcasted_iota(jnp.int32, (l, n), 0)
        q = lax.broadcasted_iota(jnp.int32, (l, n), 1) * stride
        ang = ((p * q) % l).astype(F32) * (2.0 * math.pi / l)
        return jnp.cos(ang), jnp.sin(ang)

    c1, s1 = (t[:, :, None] for t in thin(l // f, f))
    c2, s2 = (t[:, None, :] for t in thin(f, 1))
    return jnp.concatenate([(c1 * c2 - s1 * s2).reshape(l, l), (-(s1 * c2 + c1 * s2)).reshape(l, l)],
                           axis=1).astype(BF16)


def _channel_tables(l, d_f):
    cc = lax.broadcasted_iota(jnp.int32, (d_f, d_f), 0)
    qq = lax.broadcasted_iota(jnp.int32, (d_f, d_f), 1)
    a2 = (((cc % HEAD) * (qq % HEAD)) % HEAD).astype(F32) * (2.0 * math.pi / HEAD)
    same = (cc // HEAD) == (qq // HEAD)
    scale = 1.0 / math.sqrt(l * HEAD)
    return jnp.stack([jnp.where(same, jnp.cos(a2), 0.0), jnp.where(same, jnp.sin(a2), 0.0)]) * scale


def _dot3(a, b):
    ah = a.astype(BF16)
    al = (a - ah.astype(F32)).astype(BF16)
    bh = b.astype(BF16)
    bl = (b - bh.astype(F32)).astype(BF16)
    return _dot(ah, bh) + (_dot(ah, bl) + _dot(al, bh))


FFT_SUB = 8


def _fft_stage1_kernel(*refs):
    u_refs, (ct_ref, t2_ref, tw_ref, o_ref) = refs[:-4], refs[-4:]
    i = pl.program_id(1)
    l2 = t2_ref.shape[0] // 2
    for j in range(FFT_SUB):
        u = jnp.concatenate([r[pl.ds(i * FFT_SUB + j, l2, stride=FFT_L1), :] for r in u_refs], axis=1)
        z = jnp.concatenate([_dot3(u, ct_ref[0]), -_dot3(u, ct_ref[1])], axis=0)
        bm = _dot3(t2_ref[...], z)
        br, bi = bm[0:l2], bm[l2:]
        tw = tw_ref[j]
        cs, sn = tw[:, 0:1], tw[:, 1:2]
        comp = (br * cs + bi * sn, bi * cs - br * sn)
        for c in range(2):
            for pb in range(l2 // FFT_SUB):
                o_ref[pb, c, j * FFT_SUB:(j + 1) * FFT_SUB, :] = comp[c][pb * FFT_SUB:(pb + 1) * FFT_SUB]


def _fft_stage2_kernel(t1_ref, b_ref, o_ref):
    x = jnp.concatenate([b_ref[0], b_ref[1]], axis=0)
    y = _dot3(t1_ref[...], x)
    for p1 in range(o_ref.shape[0]):
        o_ref[p1] = y[p1 * FFT_SUB:(p1 + 1) * FFT_SUB]


def _fft_tables(l):
    l1, l2 = FFT_L1, l // FFT_L1

    def cs(n, m, period):
        p = lax.broadcasted_iota(jnp.int32, (n, m), 0)
        q = lax.broadcasted_iota(jnp.int32, (n, m), 1)
        ang = ((p * q) % period).astype(F32) * (2.0 * math.pi / period)
        return jnp.cos(ang), jnp.sin(ang)

    c2, s2 = cs(l2, l2, l2)
    t2 = jnp.concatenate([jnp.concatenate([c2, s2], axis=1), jnp.concatenate([-s2, c2], axis=1)], axis=0)
    tw = jnp.stack(cs(l1, l2, l), axis=-1)
    c1, s1 = cs(l1, l1, l1)
    t1 = jnp.kron(jnp.concatenate([c1, s1], axis=1), jnp.eye(FFT_SUB, dtype=F32))
    return t2, tw, t1


def _fourier_fft(px, ctab, tabs, ft_blk, d_f):
    b, l, n = px.shape
    l1, l2 = FFT_L1, l // FFT_L1
    npb = l2 // FFT_SUB
    nslab = d_f // LANES
    t2, tw, t1 = tabs
    stage1 = pl.pallas_call(
        _fft_stage1_kernel,
        grid=(b, l1 // FFT_SUB),
        in_specs=[pl.BlockSpec((None, l, LANES), lambda bi, i, k=k: (bi, 0, ft_blk * nslab + k))
                  for k in range(nslab)] + [
                  pl.BlockSpec(ctab.shape, lambda bi, i: (0, 0, 0)),
                  pl.BlockSpec(t2.shape, lambda bi, i: (0, 0)),
                  pl.BlockSpec((FFT_SUB, l2, 2), lambda bi, i: (i, 0, 0))],
        out_specs=pl.BlockSpec((None, npb, 2, FFT_SUB * FFT_SUB, d_f), lambda bi, i: (bi, 0, 0, i, 0)),
        out_shape=jax.ShapeDtypeStruct((b, npb, 2, l1 * FFT_SUB, d_f), F32),
        compiler_params=_cparams(("parallel", "arbitrary")),
        name="fft_stage1",
    )(*([px] * nslab), ctab, t2, tw)
    y = pl.pallas_call(
        _fft_stage2_kernel,
        grid=(b, npb),
        in_specs=[pl.BlockSpec(t1.shape, lambda bi, j: (0, 0)),
                  pl.BlockSpec((None, None, 2, l1 * FFT_SUB, d_f), lambda bi, j: (bi, j, 0, 0, 0))],
        out_specs=pl.BlockSpec((None, l1, None, FFT_SUB, d_f), lambda bi, j: (bi, 0, j, 0, 0)),
        out_shape=jax.ShapeDtypeStruct((b, l1, npb, FFT_SUB, d_f), F32),
        compiler_params=_cparams(("parallel", "parallel")),
        name="fft_stage2",
    )(t1, stage1)
    return y.reshape(b, l, d_f)


def _fourier_dense(px, wtab, ctab, ft_blk, d_f):
    b, l, _ = px.shape
    y = _dft_seq(wtab, _dft_channels(px, ctab, ft_blk, d_f))
    return y.reshape(l, b, d_f).transpose(1, 0, 2)


def _mix_out_kernel(period, yf_ref, yb_ref, z_ref, gate_ref, cg_ref, cx_ref, cb_ref, ft_ref, x_ref,
                    ga_ref, lnw_ref, lnb_ref, cw_ref, w_ref, o_ref, mix_ref):
    d_rwkv = yf_ref.shape[1]
    d_conv = cg_ref.shape[1]
    tm = yf_ref.shape[0]
    r = lax.broadcasted_iota(jnp.int32, (GW, GW), 0)
    c = lax.broadcasted_iota(jnp.int32, (GW, GW), 1)
    ones_bd = jnp.where((r // HEAD) == (c // HEAD), 1.0, 0.0).astype(BF16)
    inv_n = 1.0 / HEAD
    for s in range(d_rwkv // GW):
        sl = slice(s * GW, (s + 1) * GW)
        y = yf_ref[:, sl] + yb_ref[:, sl]
        mu = _dot_lhs_f32(y, ones_bd) * inv_n
        dlt = y - mu
        var = _dot_lhs_f32(dlt * dlt, ones_bd) * inv_n
        yn = dlt * lax.rsqrt(var + GN_EPS) * lnw_ref[:, sl] + lnb_ref[:, sl] + z_ref[:, sl]
        mix_ref[:, sl] = (yn * gate_ref[:, sl]).astype(BF16)
    u = cg_ref[...].astype(F32) * cx_ref[...].astype(F32)
    rowid = lax.broadcasted_iota(jnp.int32, u.shape, 0) % period
    up = jnp.where(rowid == 0, 0.0, pltpu.roll(u, 1, 0))
    dn = jnp.where(rowid == period - 1, 0.0, pltpu.roll(u, tm - 1, 0))
    cw = cw_ref[...]
    conv = cb_ref[...].astype(F32) * (up * cw[0:1] + u * cw[1:2] + dn * cw[2:3])
    mix_ref[:, d_rwkv:d_rwkv + d_conv] = conv.astype(BF16)
    mix_ref[:, d_rwkv + d_conv:] = ft_ref[...].astype(BF16)
    o_ref[...] = x_ref[...] + ga_ref[...] * jnp.dot(mix_ref[...], w_ref[...], preferred_element_type=F32)


def _mix_out(yf, yb, z, gate, px, ft, x, ga, lw, period, conv_blk):
    b, l, d = x.shape
    d_rwkv = yf.shape[2]
    d_conv = lw["conv_w"].shape[1]
    d_f = ft.shape[2]
    tm = min(256, l)
    assert tm % period == 0
    yspec = pl.BlockSpec((None, tm, d_rwkv), lambda bi, i: (bi, i, 0))
    cspec = lambda off: pl.BlockSpec((None, tm, d_conv), lambda bi, i: (bi, i, conv_blk + off))
    return pl.pallas_call(
        functools.partial(_mix_out_kernel, period),
        grid=(b, l // tm),
        in_specs=[yspec, yspec, yspec, yspec, cspec(0), cspec(1), cspec(2),
                  pl.BlockSpec((None, tm, d_f), lambda bi, i: (bi, i, 0)),
                  pl.BlockSpec((None, tm, d), lambda bi, i: (bi, i, 0)),
                  pl.BlockSpec((None, 1, d), lambda bi, i: (bi, 0, 0)),
                  pl.BlockSpec((1, d_rwkv), lambda bi, i: (0, 0)),
                  pl.BlockSpec((1, d_rwkv), lambda bi, i: (0, 0)),
                  pl.BlockSpec((3, d_conv), lambda bi, i: (0, 0)),
                  pl.BlockSpec(lw["w_out"].shape, lambda bi, i: (0, 0))],
        out_specs=pl.BlockSpec((None, tm, d), lambda bi, i: (bi, i, 0)),
        out_shape=jax.ShapeDtypeStruct((b, l, d), F32),
        scratch_shapes=[pltpu.VMEM((tm, lw["w_out"].shape[0]), BF16)],
        compiler_params=_cparams(("parallel", "parallel")),
        name="mix_out",
    )(yf, yb, z, gate, px, px, px, ft, x, ga, lw["ln_w"], lw["ln_b"], lw["conv_w"], lw["w_out"])


def _ffn_kernel(final, x_ref, g_ref, sh_ref, sc_ref, ga_ref, wg_ref, wu_ref, wd_ref, gf_ref, o_ref,
                h_ref, acc_ref):
    j = pl.program_id(2)

    @pl.when(j == 0)
    def _():
        h_ref[...] = _rms_mod(x_ref[...], g_ref[...], sh_ref[...], sc_ref[...]).astype(BF16)
        acc_ref[...] = jnp.zeros_like(acc_ref)

    h = h_ref[...]
    a = jnp.dot(h, wg_ref[...], preferred_element_type=F32)
    u = jnp.dot(h, wu_ref[...], preferred_element_type=F32)
    t = (a * jax.nn.sigmoid(a)) * u
    acc_ref[...] += jnp.dot(t.astype(BF16), wd_ref[...], preferred_element_type=F32)

    @pl.when(j == pl.num_programs(2) - 1)
    def _():
        xn = x_ref[...] + ga_ref[...] * acc_ref[...]
        if final:
            ms = jnp.mean(xn * xn, axis=-1, keepdims=True)
            xn = xn * lax.rsqrt(ms + RMS_EPS) * gf_ref[...]
        o_ref[...] = xn


def _ffn(x, g, shift, scale, ga, wg, wu, wd, g_final, final):
    b, l, d = x.shape
    tf = COL_TILE
    ff = wg.shape[0] * tf
    tm = min(512, l)
    vec = pl.BlockSpec((None, 1, d), lambda bi, i, j: (bi, 0, 0))
    gspec = pl.BlockSpec((1, d), lambda bi, i, j: (0, 0))
    return pl.pallas_call(
        functools.partial(_ffn_kernel, final),
        grid=(b, l // tm, ff // tf),
        in_specs=[pl.BlockSpec((None, tm, d), lambda bi, i, j: (bi, i, 0)),
                  gspec, vec, vec, vec,
                  pl.BlockSpec((None, d, tf), lambda bi, i, j: (j, 0, 0)),
                  pl.BlockSpec((None, d, tf), lambda bi, i, j: (j, 0, 0)),
                  pl.BlockSpec((tf, d), lambda bi, i, j: (j, 0)),
                  gspec],
        out_specs=pl.BlockSpec((None, tm, d), lambda bi, i, j: (bi, i, 0)),
        out_shape=jax.ShapeDtypeStruct((b, l, d), F32),
        scratch_shapes=[pltpu.VMEM((tm, d), BF16), pltpu.VMEM((tm, d), F32)],
        compiler_params=_cparams(("parallel", "parallel", "arbitrary")),
        name="ffn",
    )(x, g, shift, scale, ga, wg, wu, wd, g_final)


def _layer_weights(i, w_in, rw_shift, dec_w0, dec_up, iclr_a0, iclr_up, k_k, k_a, r_k, ln_w, ln_b, g_up,
                   conv_w, w_out, w_gate, w_up, w_down, dims):
    d_rwkv, d_conv, d_f, dl, il, gl = dims
    ng = d_rwkv // GW
    wd0 = 3 * d_rwkv
    ad0 = wd0 + dl
    gd0 = ad0 + il
    rest0 = gd0 + gl

    def pad_lora(m, stop):
        padl = jnp.zeros((m.shape[0], LORA_PAD - dl), m.dtype)
        padi = jnp.zeros((m.shape[0], LORA_PAD - il), m.dtype)
        return jnp.concatenate([m[:, :ad0], padl, m[:, ad0:gd0], padi, m[:, gd0:stop]], axis=1)

    col_tiles = lambda m: m.reshape(m.shape[0], m.shape[1] // COL_TILE, COL_TILE).transpose(1, 0, 2)
    pad_rows = lambda m, n: jnp.concatenate([m, jnp.zeros(m.shape[:-2] + (n - m.shape[-2], m.shape[-1]), m.dtype)], -2)
    zero = jnp.zeros((d_rwkv,), F32)
    pvec = jnp.stack([k_k[i], k_a[i], r_k[i].reshape(-1), dec_w0[i, 0], dec_w0[i, 1],
                      iclr_a0[i, 0], iclr_a0[i, 1], zero])
    return dict(
        ng=ng,
        w_in=col_tiles(pad_lora(w_in[i].astype(BF16), w_in.shape[2])),
        mix=pad_lora(rw_shift[i], rest0),
        pvec=pvec,
        dec_up=pad_rows(dec_up[i], LORA_PAD).astype(BF16),
        iclr_up=pad_rows(iclr_up[i], LORA_PAD).astype(BF16),
        g_up=g_up[i].astype(BF16),
        ln_w=ln_w[i][None, :], ln_b=ln_b[i][None, :],
        conv_w=conv_w[i],
        w_out=w_out[i].astype(BF16),
        w_gate=col_tiles(w_gate[i].astype(BF16)), w_up=col_tiles(w_up[i].astype(BF16)),
        w_down=w_down[i].astype(BF16),
    )


def kernel(x, c, ctx, c_ctx, w_mod, b_mod, norm_mix, w_in, rw_shift, dec_w0, dec_up, iclr_a0, iclr_up,
           k_k, k_a, r_k, ln_w, ln_b, g_up, conv_w, w_out, norm_ffn, w_gate, w_up, w_down, norm_final):
    b, l, d = x.shape
    lc = ctx.shape[1]
    depth = w_mod.shape[0]
    d_rwkv = k_k.shape[1]
    d_conv = conv_w.shape[2]
    dl, il, gl = dec_up.shape[2], iclr_up.shape[2], g_up.shape[1]
    d_f = w_out.shape[1] - d_rwkv - d_conv
    assert dl <= LORA_PAD and il <= LORA_PAD and gl == 2 * LORA_PAD and d_rwkv % GW == 0
    assert l % GRID_W == 0 and l % CHUNK == 0 and lc % CHUNK == 0
    nsub_x = math.gcd(l // CHUNK, WKV_SUB)
    nsub_c = math.gcd(lc // CHUNK, WKV_SUB)
    ng = d_rwkv // GW
    n_rw = 3 * d_rwkv + 4 * LORA_PAD
    conv_blk = n_rw // d_conv
    assert conv_blk * d_conv == n_rw and d_f == COL_TILE

    cvec = jnp.concatenate([c, c_ctx[None, :], jnp.zeros((8 - b - 1, d), F32)], axis=0)
    mods = _adaln(cvec, w_mod, b_mod[:, None, :])

    def mod(i, j, ctx_rows):
        m = mods[i, :, j * d:(j + 1) * d]
        if ctx_rows:
            return jnp.broadcast_to(m[b:b + 1], (b, d))[:, None, :]
        return m[:b][:, None, :]

    fft_x = l % (FFT_L1 * 128) == 0
    wtab_c, ctab_c = _seq_table(lc), _channel_tables(lc, d_f)
    ctab_x = _channel_tables(l, d_f)
    if fft_x:
        ftab_x = _fft_tables(l)
    else:
        wtab_x = _seq_table(l)
    dims = (d_rwkv, d_conv, d_f, dl, il, gl)
    s_zero = jnp.zeros((b, 2, ng * N_PAIR, PW, PW), F32)
    gfin = norm_final[None, :]
    xc = ctx
    for i in range(depth):
        lw = _layer_weights(i, w_in, rw_shift, dec_w0, dec_up, iclr_a0, iclr_up, k_k, k_a, r_k, ln_w, ln_b,
                            g_up, conv_w, w_out, w_gate, w_up, w_down, dims)
        gmix = norm_mix[i][None, :]
        gffn = norm_ffn[i][None, :]
        last = i == depth - 1
        pc, pcf = _in_proj(xc, gmix, mod(i, 0, True), mod(i, 1, True), lw["w_in"], lw["mix"])
        px, pxf = _in_proj(x, gmix, mod(i, 0, False), mod(i, 1, False), lw["w_in"], lw["mix"])
        yfc, ybc, zc, gc, s_ctx = _wkv(pc, lw, s_zero, nsub_c)
        yfx, ybx, zx, gx, _ = _wkv(px, lw, s_ctx, nsub_x)
        if fft_x:
            ftx = _fourier_fft(pxf, ctab_x, ftab_x, 0, d_f)
        else:
            ftx = _fourier_dense(pxf, wtab_x, ctab_x, 0, d_f)
        x = _mix_out(yfx, ybx, zx, gx, px, ftx, x, mod(i, 2, False), lw, GRID_W, conv_blk)
        x = _ffn(x, gffn, mod(i, 3, False), mod(i, 4, False), mod(i, 5, False),
                 lw["w_gate"], lw["w_up"], lw["w_down"], gfin, last)
        if not last:
            ftc = _fourier_dense(pcf, wtab_c, ctab_c, 0, d_f)
            xc = _mix_out(yfc, ybc, zc, gc, pc, ftc, xc, mod(i, 2, True), lw, lc, conv_blk)
            xc = _ffn(xc, gffn, mod(i, 3, True), mod(i, 4, True), mod(i, 5, True),
                      lw["w_gate"], lw["w_up"], lw["w_down"], gfin, False)
    return x
```

```python
import functools
import math

import jax
import jax.numpy as jnp
from jax import lax
from jax.experimental import pallas as pl
from jax.experimental.pallas import tpu as pltpu

F32 = jnp.float32
BF16 = jnp.bfloat16

HEAD = 64
GROUP_HEADS = 4
GW = GROUP_HEADS * HEAD
PAIR_HEADS = 2
PW = PAIR_HEADS * HEAD
N_PAIR = GW // PW
CHUNK = 64
WKV_SUB = 8
HALO = 16
LANES = 128
COL_TILE = 512
LORA_PAD = 128
GRID_W = 64
FFT_L1 = 64
N_MOD = 6
RMS_EPS = 1e-6
GN_EPS = 64e-5
KK_EPS = 1e-12
VMEM_LIMIT = 56 * 1024 * 1024


def _cparams(sem):
    return pltpu.CompilerParams(dimension_semantics=sem, vmem_limit_bytes=VMEM_LIMIT)


def _dot(a, b, nt=False):
    dn = (((1,), (1,)), ((), ())) if nt else (((1,), (0,)), ((), ()))
    return lax.dot_general(a.astype(BF16), b.astype(BF16), dn, preferred_element_type=F32)


def _split2(a):
    hi = a.astype(BF16)
    return hi, (a - hi.astype(F32)).astype(BF16)


def _dot_lhs_f32(a, b_exact):
    hi, lo = _split2(a)
    return _dot(hi, b_exact) + _dot(lo, b_exact)


def _dot_rhs_f32(a_exact, b):
    hi, lo = _split2(b)
    return _dot(a_exact, hi) + _dot(a_exact, lo)


def _rms_mod(x, g, shift, scale):
    ms = jnp.mean(x * x, axis=-1, keepdims=True)
    return (x * lax.rsqrt(ms + RMS_EPS) * g) * (1.0 + scale) + shift


def _adaln_kernel(c_ref, w_ref, b_ref, o_ref):
    c = c_ref[...]
    s = c * jax.nn.sigmoid(c)
    o_ref[...] = _dot(s, w_ref[...]) + b_ref[...]


def _adaln(cvec, w_mod, b_mod):
    nl, d, n = w_mod.shape
    tn = 1024
    return pl.pallas_call(
        _adaln_kernel,
        grid=(nl, n // tn),
        in_specs=[pl.BlockSpec((8, d), lambda l, j: (0, 0)),
                  pl.BlockSpec((None, d, tn), lambda l, j: (l, 0, j)),
                  pl.BlockSpec((None, 1, tn), lambda l, j: (l, 0, j))],
        out_specs=pl.BlockSpec((None, 8, tn), lambda l, j: (l, 0, j)),
        out_shape=jax.ShapeDtypeStruct((nl, 8, n), F32),
        compiler_params=_cparams(("parallel", "parallel")),
        name="adaln",
    )(cvec, w_mod, b_mod)


def _in_proj_kernel(n_mix_tiles, x_ref, xp_ref, xn_ref, g_ref, sh_ref, sc_ref, w_ref, mix_ref, o_ref, of_ref,
                    h_ref):
    i = pl.program_id(1)
    j = pl.program_id(2)
    tm = x_ref.shape[0]

    @pl.when(pl.program_id(2) == 0)
    def _():
        g, sh, sc = g_ref[...], sh_ref[...], sc_ref[...]
        prev = jnp.where(i > 0, _rms_mod(xp_ref[...], g, sh, sc), 0.0)
        nxt = jnp.where(i < pl.num_programs(1) - 1, _rms_mod(xn_ref[...], g, sh, sc), 0.0)
        h_ref[0:HALO, :] = prev.astype(BF16)
        h_ref[HALO:HALO + tm, :] = _rms_mod(x_ref[...], g, sh, sc).astype(BF16)
        h_ref[HALO + tm:, :] = nxt.astype(BF16)

    @pl.when(j < n_mix_tiles)
    def _():
        p = jnp.dot(h_ref[...], w_ref[...], preferred_element_type=F32)
        mix = mix_ref[...]
        up = pltpu.roll(p, 1, 0)[HALO:HALO + tm]
        dn = pltpu.roll(p, tm + 2 * HALO - 1, 0)[HALO:HALO + tm]
        o_ref[...] = (up * mix[0:1, :] + p[HALO:HALO + tm] * mix[1:2, :] + dn * mix[2:3, :]).astype(o_ref.dtype)

    last = pl.num_programs(2) - 1

    @pl.when((j >= n_mix_tiles) & (j < last))
    def _():
        o_ref[...] = jnp.dot(h_ref[HALO:HALO + tm, :], w_ref[...],
                             preferred_element_type=F32).astype(o_ref.dtype)

    @pl.when(j == last)
    def _():
        of_ref[...] = jnp.dot(h_ref[HALO:HALO + tm, :], w_ref[...], preferred_element_type=F32)


def _in_proj(x, g, shift, scale, w, mix):
    b, l, d = x.shape
    tn = COL_TILE
    n = w.shape[1]
    tm = min(1024, l)
    n_mix_tiles = mix.shape[1] // tn
    assert n_mix_tiles * tn == mix.shape[1]
    hb = tm // HALO
    nhb = l // HALO
    vec = pl.BlockSpec((None, 1, d), lambda bi, i, j: (bi, 0, 0))
    return pl.pallas_call(
        functools.partial(_in_proj_kernel, n_mix_tiles),
        grid=(b, l // tm, n // tn),
        in_specs=[pl.BlockSpec((None, tm, d), lambda bi, i, j: (bi, i, 0)),
                  pl.BlockSpec((None, HALO, d), lambda bi, i, j: (bi, jnp.maximum(i * hb - 1, 0), 0)),
                  pl.BlockSpec((None, HALO, d), lambda bi, i, j: (bi, jnp.minimum((i + 1) * hb, nhb - 1), 0)),
                  pl.BlockSpec((1, d), lambda bi, i, j: (0, 0)),
                  vec, vec,
                  pl.BlockSpec((d, tn), lambda bi, i, j: (0, j)),
                  pl.BlockSpec((3, tn), lambda bi, i, j: (0, jnp.minimum(j, n_mix_tiles - 1)))],
        out_specs=[pl.BlockSpec((None, tm, tn), lambda bi, i, j: (bi, i, jnp.minimum(j, n // tn - 2))),
                   pl.BlockSpec((None, tm, tn), lambda bi, i, j: (bi, i, 0))],
        out_shape=[jax.ShapeDtypeStruct((b, l, n - tn), BF16), jax.ShapeDtypeStruct((b, l, tn), F32)],
        scratch_shapes=[pltpu.VMEM((tm + 2 * HALO, d), BF16)],
        compiler_params=_cparams(("parallel", "parallel", "arbitrary")),
        name="in_proj",
    )(x, x, x, g, shift, scale, w, mix)


def _block_masks(width):
    r = lax.broadcasted_iota(jnp.int32, (width, width), 0)
    c = lax.broadcasted_iota(jnp.int32, (width, width), 1)
    return (r // HEAD) == (c // HEAD), r % HEAD, c % HEAD, r == c


def _rs(x, bd):
    return jnp.where(bd, jnp.concatenate([x] * PAIR_HEADS, axis=0), 0.0)


def _ls(x_rs):
    out = x_rs[0:CHUNK]
    for h in range(1, PAIR_HEADS):
        out = out + x_rs[h * CHUNK:(h + 1) * CHUNK]
    return out


def _each(f, *lists):
    return [f(*a) for a in zip(*lists)]


def _chunk_affine(chunks, masks):
    bd, tt, ss, eye = masks
    row = lax.broadcasted_iota(jnp.int32, (CHUNK, CHUNK), 0)
    col = lax.broadcasted_iota(jnp.int32, (CHUNK, CHUNK), 1)
    tri = {rev: jnp.where((row <= col) if rev else (row >= col), 1.0, 0.0).astype(BF16) for rev in (False, True)}
    strict = {False: bd & (ss < tt), True: bd & (ss > tt)}
    r2 = lax.broadcasted_iota(jnp.int32, (PW, 2 * PW), 0)
    c2 = lax.broadcasted_iota(jnp.int32, (PW, 2 * PW), 1)
    bd2 = (r2 // HEAD) == ((c2 % PW) // HEAD)
    incl2 = {False: bd2 & (c2 % HEAD <= r2 % HEAD), True: bd2 & (c2 % HEAD >= r2 % HEAD)}

    crev = [ch[6] for ch in chunks]
    logw = [ch[5] for ch in chunks]
    cum = _each(lambda rv, lw: _dot_rhs_f32(tri[rv], lw), crev, logw)
    total = _each(lambda rv, cm: cm[0:1] if rv else cm[CHUNK - 1:CHUNK], crev, cum)
    p_inv = _each(lambda cm: jnp.exp(-cm), cum)
    p_end = _each(lambda tot, cm: jnp.exp(tot - cm), total, cum)
    bvec = [ch[3] * ch[4] for ch in chunks]
    wide = dict(
        a=_each(lambda ch, cm, lw: -ch[3] * jnp.exp(cm - lw), chunks, cum, logw),
        r=_each(lambda ch, cm: ch[0] * jnp.exp(cm), chunks, cum),
        b=_each(lambda b, p: b * p, bvec, p_inv),
        k=_each(lambda ch, p: ch[1] * p, chunks, p_inv),
        v=[ch[2] for ch in chunks],
        be=_each(lambda b, p: b * p, bvec, p_end),
        ke=_each(lambda ch, p: ch[1] * p, chunks, p_end),
        pc=_each(jnp.exp, total),
    )

    def pairs(name):
        return [x[:, p * PW:(p + 1) * PW] for x in wide[name] for p in range(N_PAIR)]

    rev = [rv for rv in crev for _ in range(N_PAIR)]
    r_t = pairs("r")
    a_rs = _each(lambda x: _rs(x, bd).astype(BF16), pairs("a"))
    r_rs = _each(lambda x: _rs(x, bd).astype(BF16), r_t)
    b_rs = _each(lambda x: jnp.concatenate([x.astype(BF16)] * PAIR_HEADS, axis=0), pairs("b"))
    k_rs = _each(lambda x: jnp.concatenate([x.astype(BF16)] * PAIR_HEADS, axis=0), pairs("k"))
    v_rs = _each(lambda x: _rs(x, bd).astype(BF16), pairs("v"))
    aa = _each(lambda a, r, b, k: _dot(jnp.concatenate([a, r], axis=0), jnp.concatenate([b, k], axis=0), nt=True),
               a_rs, r_rs, b_rs, k_rs)
    a_ab = _each(lambda rv, x: jnp.where(strict[rv], x[0:PW, 0:PW], 0.0), rev, aa)
    a_ak = _each(lambda rv, x: jnp.where(strict[rv], x[0:PW, PW:], 0.0), rev, aa)
    a_r = _each(lambda rv, x: jnp.where(incl2[rv], x[PW:, :], 0.0).astype(BF16), rev, aa)
    t_inv = _each(lambda a: jnp.where(eye, 1.0, a), a_ab)
    pw = _each(lambda a: _dot(a, a), a_ab)
    for _ in range(int(math.log2(CHUNK)) - 2):
        res = _each(lambda p, t: _dot(p, jnp.concatenate([p, t], axis=1)), pw, t_inv)
        pw = [x[:, 0:PW] for x in res]
        t_inv = _each(lambda t, x: t + x[:, PW:], t_inv, res)
    t_inv = _each(lambda p, t: (t + _dot(p, t)).astype(BF16), pw, t_inv)
    akv = _each(_dot, a_ak, v_rs)
    tw = _each(lambda t, a, x: _dot(t, jnp.concatenate([a, x.astype(BF16)], axis=1)).astype(BF16),
               t_inv, a_rs, akv)
    bk_t = _each(lambda b, k: jnp.concatenate([_rs(b, bd), _rs(k, bd)], axis=0).T.astype(BF16),
                 pairs("be"), pairs("ke"))
    fin = _each(lambda ar, bk, w, v: _dot(jnp.concatenate([ar, bk], axis=0),
                                          jnp.concatenate([w, jnp.concatenate([jnp.zeros_like(v), v], axis=1)],
                                                          axis=0)),
                a_r, bk_t, tw, v_rs)
    rh = _each(lambda r, f: r + _ls(f[0:PW, 0:PW]), r_t, fin)
    y0 = _each(lambda f: _ls(f[0:PW, PW:]), fin)
    m = _each(lambda pc, f: jnp.where(eye, pc, 0.0) + f[PW:, 0:PW], pairs("pc"), fin)
    nn = [f[PW:, PW:] for f in fin]
    out = list(zip(y0, rh, m, nn))
    return [out[i * N_PAIR:(i + 1) * N_PAIR] for i in range(len(chunks))]


def _head_sum(x, ones_bd):
    return _dot_lhs_f32(x, ones_bd)


def _wkv_kernel(nsub, r_f, k_f, v_f, lo_f, r_b, k_b, v_b, lo_b, pvec, dec_up, iclr_up, g_up, s0,
                yf_ref, yb_ref, z_ref, gate_ref, sfin_ref,
                zst):
    c = pl.program_id(2)
    nsteps = pl.num_programs(2)

    @pl.when(c == 0)
    def _():
        zst[...] = s0[...]

    masks = _block_masks(PW)
    ones_bd = jnp.where(_block_masks(GW)[0], 1.0, 0.0).astype(BF16)
    pv = pvec[...]
    k_k, k_a, r_k = pv[0:1], pv[1:2], pv[2:3]
    w0 = (pv[3:4], pv[4:5])
    a0 = (pv[5:6], pv[6:7])
    sig_scale = math.exp(-0.5)

    def streams(r_ref, k_ref, v_ref, lo_ref, d):
        r, k, v, lo = (ref[...].astype(F32) for ref in (r_ref, k_ref, v_ref, lo_ref))
        wd = jnp.tanh(lo[:, 0:LORA_PAD])
        ad = lo[:, LORA_PAD:2 * LORA_PAD]
        gd = lo[:, 2 * LORA_PAD:]
        kk = k * k_k
        kkn = kk * lax.rsqrt(jnp.maximum(_head_sum(kk * kk, ones_bd), KK_EPS * KK_EPS))
        logw = -sig_scale * jax.nn.sigmoid(w0[d] + _dot(wd, dec_up[d]))
        aic = jax.nn.sigmoid(a0[d] + _dot(ad, iclr_up[d]))
        kd = k * (1.0 + (aic - 1.0) * k_a)
        return r, k, v, kkn, logw, aic, kd, ad, gd

    r, k, v, kkn, logw, aic, kd, ad, gd = streams(r_f, k_f, v_f, lo_f, 0)
    aic_o = jax.nn.sigmoid(a0[1] + _dot(ad, iclr_up[1]))
    kd_o = k * (1.0 + (aic_o - 1.0) * k_a)
    z_ref[...] = _head_sum(r * (kd + kd_o) * r_k, ones_bd) * v
    gate_ref[...] = _dot(jax.nn.sigmoid(gd), g_up[...])
    chains = [tuple(t[j * CHUNK:(j + 1) * CHUNK] for t in (r, kd, v, kkn, aic, logw)) + (False,)
              for j in range(nsub)]
    r, k, v, kkn, logw, aic, kd, ad, gd = streams(r_b, k_b, v_b, lo_b, 1)
    chains += [tuple(t[j * CHUNK:(j + 1) * CHUNK] for t in (r, kd, v, kkn, aic, logw)) + (True,)
               for j in range(nsub)]
    affine = _chunk_affine(chains, masks)
    fwd, bwd = affine[:nsub], affine[nsub:]

    for d, (steps, y_ref) in enumerate(((list(range(nsub)), yf_ref), (list(reversed(range(nsub))), yb_ref))):
        affine_d = fwd if d == 0 else bwd
        z = [zst[d, p] for p in range(N_PAIR)]
        for j in steps:
            mz = [_dot(jnp.concatenate([affine_d[j][p][2], affine_d[j][p][1]], axis=0), z[p])
                  for p in range(N_PAIR)]
            y_ref[j * CHUNK:(j + 1) * CHUNK, :] = jnp.concatenate(
                [affine_d[j][p][0] + mz[p][PW:] for p in range(N_PAIR)], axis=1)
            z = [mz[p][0:PW] + affine_d[j][p][3] for p in range(N_PAIR)]
        for p in range(N_PAIR):
            zst[d, p] = z[p]

    @pl.when(c == nsteps - 1)
    def _():
        sfin_ref[...] = zst[...]


def _wkv(px, lw, s0, nsub):
    b, l, _ = px.shape
    ng = lw["ng"]
    d_rwkv = ng * GW
    t = nsub * CHUNK
    nsteps = l // t
    lo_blk = 3 * d_rwkv // (4 * LORA_PAD)
    lo_w = 4 * LORA_PAD

    def main_specs(blk):
        return [pl.BlockSpec((None, t, GW), lambda bi, g, c, s=s: (bi, blk(c), s * ng + g)) for s in range(3)] + [
            pl.BlockSpec((None, t, lo_w), lambda bi, g, c: (bi, blk(c), lo_blk))]

    fblk = lambda c: c
    bblk = lambda c: nsteps - 1 - c
    in_specs = (main_specs(fblk) + main_specs(bblk) + [
        pl.BlockSpec((8, GW), lambda bi, g, c: (0, g)),
        pl.BlockSpec((2, LORA_PAD, GW), lambda bi, g, c: (0, 0, g)),
        pl.BlockSpec((2, LORA_PAD, GW), lambda bi, g, c: (0, 0, g)),
        pl.BlockSpec((2 * LORA_PAD, GW), lambda bi, g, c: (0, g)),
        pl.BlockSpec((None, 2, N_PAIR, PW, PW), lambda bi, g, c: (bi, 0, g, 0, 0)),
    ])
    yspec = lambda blk: pl.BlockSpec((None, t, GW), lambda bi, g, c: (bi, blk(c), g))
    out_specs = [yspec(fblk), yspec(bblk), yspec(fblk), yspec(fblk),
                 pl.BlockSpec((None, 2, N_PAIR, PW, PW), lambda bi, g, c: (bi, 0, g, 0, 0))]
    ysh = jax.ShapeDtypeStruct((b, l, d_rwkv), F32)
    return pl.pallas_call(
        functools.partial(_wkv_kernel, nsub),
        grid=(b, ng, nsteps),
        in_specs=in_specs,
        out_specs=out_specs,
        out_shape=[ysh, ysh, ysh, ysh, jax.ShapeDtypeStruct(s0.shape, F32)],
        scratch_shapes=[pltpu.VMEM((2, N_PAIR, PW, PW), F32)],
        compiler_params=_cparams(("parallel", "parallel", "arbitrary")),
        name="wkv",
    )(*([px] * 8), lw["pvec"], lw["dec_up"], lw["iclr_up"], lw["g_up"], s0)


def _dft_ch_kernel(u_ref, t_ref, o_ref):
    u = u_ref[...]
    hi = u.astype(BF16)
    lo = (u - hi.astype(F32)).astype(BF16)
    tab = t_ref[...]
    o_ref[...] = (_dot(hi, tab) + _dot(lo, tab)).astype(BF16)


def _dft_channels(px, tabs, ft_blk, d_f):
    b, l, _ = px.shape
    tm = min(512, l)
    nt = l // tm
    return pl.pallas_call(
        _dft_ch_kernel,
        grid=(b, 2, nt),
        in_specs=[pl.BlockSpec((None, tm, d_f), lambda bi, s, i: (bi, i, ft_blk)),
                  pl.BlockSpec((None, d_f, d_f), lambda bi, s, i: (s, 0, 0))],
        out_specs=pl.BlockSpec((tm, d_f), lambda bi, s, i: (s * nt + i, bi)),
        out_shape=jax.ShapeDtypeStruct((2 * l, b * d_f), BF16),
        compiler_params=_cparams(("parallel", "parallel", "parallel")),
        name="dft_channels",
    )(px, tabs)


def _dft_seq_kernel(w_ref, u_ref, o_ref, acc_ref):
    kk = pl.program_id(1)

    @pl.when(kk == 0)
    def _():
        acc_ref[...] = jnp.zeros_like(acc_ref)

    acc_ref[...] += jnp.dot(w_ref[...], u_ref[...], preferred_element_type=F32)

    @pl.when(kk == pl.num_programs(1) - 1)
    def _():
        o_ref[...] = acc_ref[...]


def _dft_seq(wtab, uu):
    l, k2 = wtab.shape
    n = uu.shape[1]
    tm = min(1024, l)
    tk = min(2048, k2)
    return pl.pallas_call(
        _dft_seq_kernel,
        grid=(l // tm, k2 // tk),
        in_specs=[pl.BlockSpec((tm, tk), lambda i, kk: (i, kk)),
                  pl.BlockSpec((tk, n), lambda i, kk: (kk, 0))],
        out_specs=pl.BlockSpec((tm, n), lambda i, kk: (i, 0)),
        out_shape=jax.ShapeDtypeStruct((l, n), F32),
        scratch_shapes=[pltpu.VMEM((tm, n), F32)],
        compiler_params=_cparams(("parallel", "arbitrary")),
        name="dft_seq",
    )(wtab, uu)


def _seq_table(l):
    f = math.gcd(l, 128)

    def thin(n, stride):
        p = lax.broadcasted_iota(jnp.int32, (l, n), 0)
        q = lax.broadcasted_iota(jnp.int32, (l, n), 1) * stride
        ang = ((p * q) % l).astype(F32) * (2.0 * math.pi / l)
        return jnp.cos(ang), jnp.sin(ang)

    c1, s1 = (t[:, :, None] for t in thin(l // f, f))
    c2, s2 = (t[:, None, :] for t in thin(f, 1))
    return jnp.concatenate([(c1 * c2 - s1 * s2).reshape(l, l), (-(s1 * c2 + c1 * s2)).reshape(l, l)],
                           axis=1).astype(BF16)


def _channel_tables(l, d_f):
    cc = lax.broadcasted_iota(jnp.int32, (d_f, d_f), 0)
    qq = lax.broadcasted_iota(jnp.int32, (d_f, d_f), 1)
    a2 = (((cc % HEAD) * (qq % HEAD)) % HEAD).astype(F32) * (2.0 * math.pi / HEAD)
    same = (cc // HEAD) == (qq // HEAD)
    scale = 1.0 / math.sqrt(l * HEAD)
    return jnp.stack([jnp.where(same, jnp.cos(a2), 0.0), jnp.where(same, jnp.sin(a2), 0.0)]) * scale


FFT_SUB = 8


def _fft_stage1_kernel(*refs):
    u_refs, (ct_ref, t2_ref, tw_ref, o_ref) = refs[:-4], refs[-4:]
    i = pl.program_id(1)
    l2 = t2_ref.shape[0] // 2
    for j in range(FFT_SUB):
        u = jnp.concatenate([r[pl.ds(i * FFT_SUB + j, l2, stride=FFT_L1), :] for r in u_refs], axis=1)
        z = jnp.concatenate([_dot(u, ct_ref[0]), -_dot(u, ct_ref[1])], axis=0)
        bm = _dot(t2_ref[...], z)
        br, bi = bm[0:l2], bm[l2:]
        tw = tw_ref[j]
        cs, sn = tw[:, 0:1], tw[:, 1:2]
        comp = (br * cs + bi * sn, bi * cs - br * sn)
        for c in range(2):
            for pb in range(l2 // FFT_SUB):
                o_ref[pb, c, j * FFT_SUB:(j + 1) * FFT_SUB, :] = comp[c][pb * FFT_SUB:(pb + 1) * FFT_SUB]


def _fft_stage2_kernel(t1_ref, b_ref, o_ref):
    x = jnp.concatenate([b_ref[0], b_ref[1]], axis=0)
    y = _dot(t1_ref[...], x)
    for p1 in range(o_ref.shape[0]):
        o_ref[p1] = y[p1 * FFT_SUB:(p1 + 1) * FFT_SUB]


def _fft_tables(l):
    l1, l2 = FFT_L1, l // FFT_L1

    def cs(n, m, period):
        p = lax.broadcasted_iota(jnp.int32, (n, m), 0)
        q = lax.broadcasted_iota(jnp.int32, (n, m), 1)
        ang = ((p * q) % period).astype(F32) * (2.0 * math.pi / period)
        return jnp.cos(ang), jnp.sin(ang)

    c2, s2 = cs(l2, l2, l2)
    t2 = jnp.concatenate([jnp.concatenate([c2, s2], axis=1), jnp.concatenate([-s2, c2], axis=1)], axis=0)
    tw = jnp.stack(cs(l1, l2, l), axis=-1)
    c1, s1 = cs(l1, l1, l1)
    t1 = jnp.kron(jnp.concatenate([c1, s1], axis=1), jnp.eye(FFT_SUB, dtype=F32))
    return t2.astype(BF16), tw, t1.astype(BF16)


def _fourier_fft(px, ctab, tabs, ft_blk, d_f):
    b, l, n = px.shape
    l1, l2 = FFT_L1, l // FFT_L1
    npb = l2 // FFT_SUB
    nslab = d_f // LANES
    t2, tw, t1 = tabs
    stage1 = pl.pallas_call(
        _fft_stage1_kernel,
        grid=(b, l1 // FFT_SUB),
        in_specs=[pl.BlockSpec((None, l, LANES), lambda bi, i, k=k: (bi, 0, ft_blk * nslab + k))
                  for k in range(nslab)] + [
                  pl.BlockSpec(ctab.shape, lambda bi, i: (0, 0, 0)),
                  pl.BlockSpec(t2.shape, lambda bi, i: (0, 0)),
                  pl.BlockSpec((FFT_SUB, l2, 2), lambda bi, i: (i, 0, 0))],
        out_specs=pl.BlockSpec((None, npb, 2, FFT_SUB * FFT_SUB, d_f), lambda bi, i: (bi, 0, 0, i, 0)),
        out_shape=jax.ShapeDtypeStruct((b, npb, 2, l1 * FFT_SUB, d_f), F32),
        compiler_params=_cparams(("parallel", "arbitrary")),
        name="fft_stage1",
    )(*([px] * nslab), ctab, t2, tw)
    y = pl.pallas_call(
        _fft_stage2_kernel,
        grid=(b, npb),
        in_specs=[pl.BlockSpec(t1.shape, lambda bi, j: (0, 0)),
                  pl.BlockSpec((None, None, 2, l1 * FFT_SUB, d_f), lambda bi, j: (bi, j, 0, 0, 0))],
        out_specs=pl.BlockSpec((None, l1, None, FFT_SUB, d_f), lambda bi, j: (bi, 0, j, 0, 0)),
        out_shape=jax.ShapeDtypeStruct((b, l1, npb, FFT_SUB, d_f), F32),
        compiler_params=_cparams(("parallel", "parallel")),
        name="fft_stage2",
    )(t1, stage1)
    return y.reshape(b, l, d_f)


def _fourier_dense(px, wtab, ctab, ft_blk, d_f):
    b, l, _ = px.shape
    y = _dft_seq(wtab, _dft_channels(px, ctab, ft_blk, d_f))
    return y.reshape(l, b, d_f).transpose(1, 0, 2)


def _mix_out_kernel(period, yf_ref, yb_ref, z_ref, gate_ref, cg_ref, cx_ref, cb_ref, ft_ref, x_ref,
                    ga_ref, lnw_ref, lnb_ref, cw_ref, w_ref, o_ref, mix_ref):
    d_rwkv = yf_ref.shape[1]
    d_conv = cg_ref.shape[1]
    tm = yf_ref.shape[0]
    r = lax.broadcasted_iota(jnp.int32, (GW, GW), 0)
    c = lax.broadcasted_iota(jnp.int32, (GW, GW), 1)
    ones_bd = jnp.where((r // HEAD) == (c // HEAD), 1.0, 0.0).astype(BF16)
    inv_n = 1.0 / HEAD
    for s in range(d_rwkv // GW):
        sl = slice(s * GW, (s + 1) * GW)
        y = yf_ref[:, sl] + yb_ref[:, sl]
        mu = _dot_lhs_f32(y, ones_bd) * inv_n
        dlt = y - mu
        var = _dot_lhs_f32(dlt * dlt, ones_bd) * inv_n
        yn = dlt * lax.rsqrt(var + GN_EPS) * lnw_ref[:, sl] + lnb_ref[:, sl] + z_ref[:, sl]
        mix_ref[:, sl] = (yn * gate_ref[:, sl]).astype(BF16)
    u = cg_ref[...].astype(F32) * cx_ref[...].astype(F32)
    rowid = lax.broadcasted_iota(jnp.int32, u.shape, 0) % period
    up = jnp.where(rowid == 0, 0.0, pltpu.roll(u, 1, 0))
    dn = jnp.where(rowid == period - 1, 0.0, pltpu.roll(u, tm - 1, 0))
    cw = cw_ref[...]
    conv = cb_ref[...].astype(F32) * (up * cw[0:1] + u * cw[1:2] + dn * cw[2:3])
    mix_ref[:, d_rwkv:d_rwkv + d_conv] = conv.astype(BF16)
    mix_ref[:, d_rwkv + d_conv:] = ft_ref[...].astype(BF16)
    o_ref[...] = x_ref[...] + ga_ref[...] * jnp.dot(mix_ref[...], w_ref[...], preferred_element_type=F32)


def _mix_out(yf, yb, z, gate, px, ft, x, ga, lw, period, conv_blk):
    b, l, d = x.shape
    d_rwkv = yf.shape[2]
    d_conv = lw["conv_w"].shape[1]
    d_f = ft.shape[2]
    tm = min(256, l)
    assert tm % period == 0
    yspec = pl.BlockSpec((None, tm, d_rwkv), lambda bi, i: (bi, i, 0))
    cspec = lambda off: pl.BlockSpec((None, tm, d_conv), lambda bi, i: (bi, i, conv_blk + off))
    return pl.pallas_call(
        functools.partial(_mix_out_kernel, period),
        grid=(b, l // tm),
        in_specs=[yspec, yspec, yspec, yspec, cspec(0), cspec(1), cspec(2),
                  pl.BlockSpec((None, tm, d_f), lambda bi, i: (bi, i, 0)),
                  pl.BlockSpec((None, tm, d), lambda bi, i: (bi, i, 0)),
                  pl.BlockSpec((None, 1, d), lambda bi, i: (bi, 0, 0)),
                  pl.BlockSpec((1, d_rwkv), lambda bi, i: (0, 0)),
                  pl.BlockSpec((1, d_rwkv), lambda bi, i: (0, 0)),
                  pl.BlockSpec((3, d_conv), lambda bi, i: (0, 0)),
                  pl.BlockSpec(lw["w_out"].shape, lambda bi, i: (0, 0))],
        out_specs=pl.BlockSpec((None, tm, d), lambda bi, i: (bi, i, 0)),
        out_shape=jax.ShapeDtypeStruct((b, l, d), F32),
        scratch_shapes=[pltpu.VMEM((tm, lw["w_out"].shape[0]), BF16)],
        compiler_params=_cparams(("parallel", "parallel")),
        name="mix_out",
    )(yf, yb, z, gate, px, px, px, ft, x, ga, lw["ln_w"], lw["ln_b"], lw["conv_w"], lw["w_out"])


def _ffn_kernel(final, x_ref, g_ref, sh_ref, sc_ref, ga_ref, wg_ref, wu_ref, wd_ref, gf_ref, o_ref,
                h_ref, acc_ref):
    j = pl.program_id(2)

    @pl.when(j == 0)
    def _():
        h_ref[...] = _rms_mod(x_ref[...], g_ref[...], sh_ref[...], sc_ref[...]).astype(BF16)
        acc_ref[...] = jnp.zeros_like(acc_ref)

    h = h_ref[...]
    a = jnp.dot(h, wg_ref[...], preferred_element_type=F32)
    u = jnp.dot(h, wu_ref[...], preferred_element_type=F32)
    t = (a * jax.nn.sigmoid(a)) * u
    acc_ref[...] += jnp.dot(t.astype(BF16), wd_ref[...], preferred_element_type=F32)

    @pl.when(j == pl.num_programs(2) - 1)
    def _():
        xn = x_ref[...] + ga_ref[...] * acc_ref[...]
        if final:
            ms = jnp.mean(xn * xn, axis=-1, keepdims=True)
            xn = xn * lax.rsqrt(ms + RMS_EPS) * gf_ref[...]
        o_ref[...] = xn


def _ffn(x, g, shift, scale, ga, wg, wu, wd, g_final, final):
    b, l, d = x.shape
    tf = COL_TILE
    ff = wg.shape[1]
    tm = min(512, l)
    vec = pl.BlockSpec((None, 1, d), lambda bi, i, j: (bi, 0, 0))
    gspec = pl.BlockSpec((1, d), lambda bi, i, j: (0, 0))
    return pl.pallas_call(
        functools.partial(_ffn_kernel, final),
        grid=(b, l // tm, ff // tf),
        in_specs=[pl.BlockSpec((None, tm, d), lambda bi, i, j: (bi, i, 0)),
                  gspec, vec, vec, vec,
                  pl.BlockSpec((d, tf), lambda bi, i, j: (0, j)),
                  pl.BlockSpec((d, tf), lambda bi, i, j: (0, j)),
                  pl.BlockSpec((tf, d), lambda bi, i, j: (j, 0)),
                  gspec],
        out_specs=pl.BlockSpec((None, tm, d), lambda bi, i, j: (bi, i, 0)),
        out_shape=jax.ShapeDtypeStruct((b, l, d), F32),
        scratch_shapes=[pltpu.VMEM((tm, d), BF16), pltpu.VMEM((tm, d), F32)],
        compiler_params=_cparams(("parallel", "parallel", "arbitrary")),
        name="ffn",
    )(x, g, shift, scale, ga, wg, wu, wd, g_final)


def _layer_weights(i, w_in, rw_shift, dec_w0, dec_up, iclr_a0, iclr_up, k_k, k_a, r_k, ln_w, ln_b, g_up,
                   conv_w, w_out, w_gate, w_up, w_down, dims):
    d_rwkv, d_conv, d_f, dl, il, gl = dims
    ng = d_rwkv // GW
    wd0 = 3 * d_rwkv
    ad0 = wd0 + dl
    gd0 = ad0 + il
    rest0 = gd0 + gl

    def pad_lora(m, stop):
        padl = jnp.zeros((m.shape[0], LORA_PAD - dl), m.dtype)
        padi = jnp.zeros((m.shape[0], LORA_PAD - il), m.dtype)
        return jnp.concatenate([m[:, :ad0], padl, m[:, ad0:gd0], padi, m[:, gd0:stop]], axis=1)

    pad_rows = lambda m, n: jnp.concatenate([m, jnp.zeros(m.shape[:-2] + (n - m.shape[-2], m.shape[-1]), m.dtype)], -2)
    zero = jnp.zeros((d_rwkv,), F32)
    pvec = jnp.stack([k_k[i], k_a[i], r_k[i].reshape(-1), dec_w0[i, 0], dec_w0[i, 1],
                      iclr_a0[i, 0], iclr_a0[i, 1], zero])
    return dict(
        ng=ng,
        w_in=pad_lora(w_in[i].astype(BF16), w_in.shape[2]),
        mix=pad_lora(rw_shift[i], rest0),
        pvec=pvec,
        dec_up=pad_rows(dec_up[i], LORA_PAD).astype(BF16),
        iclr_up=pad_rows(iclr_up[i], LORA_PAD).astype(BF16),
        g_up=g_up[i].astype(BF16),
        ln_w=ln_w[i][None, :], ln_b=ln_b[i][None, :],
        conv_w=conv_w[i],
        w_out=w_out[i].astype(BF16),
        w_gate=w_gate[i].astype(BF16), w_up=w_up[i].astype(BF16), w_down=w_down[i].astype(BF16),
    )


def kernel(x, c, ctx, c_ctx, w_mod, b_mod, norm_mix, w_in, rw_shift, dec_w0, dec_up, iclr_a0, iclr_up,
           k_k, k_a, r_k, ln_w, ln_b, g_up, conv_w, w_out, norm_ffn, w_gate, w_up, w_down, norm_final):
    b, l, d = x.shape
    lc = ctx.shape[1]
    depth = w_mod.shape[0]
    d_rwkv = k_k.shape[1]
    d_conv = conv_w.shape[2]
    dl, il, gl = dec_up.shape[2], iclr_up.shape[2], g_up.shape[1]
    d_f = w_out.shape[1] - d_rwkv - d_conv
    assert dl <= LORA_PAD and il <= LORA_PAD and gl == 2 * LORA_PAD and d_rwkv % GW == 0
    assert l % GRID_W == 0 and l % CHUNK == 0 and lc % CHUNK == 0
    nsub_x = math.gcd(l // CHUNK, WKV_SUB)
    nsub_c = math.gcd(lc // CHUNK, WKV_SUB)
    ng = d_rwkv // GW
    n_rw = 3 * d_rwkv + 4 * LORA_PAD
    conv_blk = n_rw // d_conv
    assert conv_blk * d_conv == n_rw and d_f == COL_TILE

    cvec = jnp.concatenate([c, c_ctx[None, :], jnp.zeros((8 - b - 1, d), F32)], axis=0)
    mods = _adaln(cvec, w_mod, b_mod[:, None, :])

    def mod(i, j, ctx_rows):
        m = mods[i, :, j * d:(j + 1) * d]
        if ctx_rows:
            return jnp.broadcast_to(m[b:b + 1], (b, d))[:, None, :]
        return m[:b][:, None, :]

    fft_x = l % (FFT_L1 * 128) == 0
    wtab_c, ctab_c = _seq_table(lc), _channel_tables(lc, d_f)
    ctab_x = _channel_tables(l, d_f)
    if fft_x:
        ftab_x = _fft_tables(l)
    else:
        wtab_x = _seq_table(l)
    dims = (d_rwkv, d_conv, d_f, dl, il, gl)
    s_zero = jnp.zeros((b, 2, ng * N_PAIR, PW, PW), F32)
    gfin = norm_final[None, :]
    xc = ctx
    for i in range(depth):
        lw = _layer_weights(i, w_in, rw_shift, dec_w0, dec_up, iclr_a0, iclr_up, k_k, k_a, r_k, ln_w, ln_b,
                            g_up, conv_w, w_out, w_gate, w_up, w_down, dims)
        gmix = norm_mix[i][None, :]
        gffn = norm_ffn[i][None, :]
        last = i == depth - 1
        pc, pcf = _in_proj(xc, gmix, mod(i, 0, True), mod(i, 1, True), lw["w_in"], lw["mix"])
        px, pxf = _in_proj(x, gmix, mod(i, 0, False), mod(i, 1, False), lw["w_in"], lw["mix"])
        yfc, ybc, zc, gc, s_ctx = _wkv(pc, lw, s_zero, nsub_c)
        yfx, ybx, zx, gx, _ = _wkv(px, lw, s_ctx, nsub_x)
        if fft_x:
            ftx = _fourier_fft(pxf, ctab_x.astype(BF16), ftab_x, 0, d_f)
        else:
            ftx = _fourier_dense(pxf, wtab_x, ctab_x, 0, d_f)
        x = _mix_out(yfx, ybx, zx, gx, px, ftx, x, mod(i, 2, False), lw, GRID_W, conv_blk)
        x = _ffn(x, gffn, mod(i, 3, False), mod(i, 4, False), mod(i, 5, False),
                 lw["w_gate"], lw["w_up"], lw["w_down"], gfin, last)
        if not last:
            ftc = _fourier_dense(pcf, wtab_c, ctab_c, 0, d_f)
            xc = _mix_out(yfc, ybc, zc, gc, pc, ftc, xc, mod(i, 2, True), lw, lc, conv_blk)
            xc = _ffn(xc, gffn, mod(i, 3, True), mod(i, 4, True), mod(i, 5, True),
                      lw["w_gate"], lw["w_up"], lw["w_down"], gfin, False)
    return x
```

```python
import functools
import math

import jax
import jax.numpy as jnp
from jax import lax
from jax.experimental import pallas as pl
from jax.experimental.pallas import tpu as pltpu

F32 = jnp.float32
BF16 = jnp.bfloat16

HEAD = 64
GROUP_HEADS = 4
GW = GROUP_HEADS * HEAD
PAIR_HEADS = 2
PW = PAIR_HEADS * HEAD
N_PAIR = GW // PW
CHUNK = 64
WKV_SUB = 8
HALO = 16
LANES = 128
COL_TILE = 512
LORA_PAD = 128
GRID_W = 64
FFT_L1 = 64
N_MOD = 6
RMS_EPS = 1e-6
GN_EPS = 64e-5
KK_EPS = 1e-12
VMEM_LIMIT = 56 * 1024 * 1024


def _cparams(sem):
    return pltpu.CompilerParams(dimension_semantics=sem, vmem_limit_bytes=VMEM_LIMIT)


def _dot(a, b, nt=False):
    dn = (((1,), (1,)), ((), ())) if nt else (((1,), (0,)), ((), ()))
    return lax.dot_general(a.astype(BF16), b.astype(BF16), dn, preferred_element_type=F32)


def _split2(a):
    hi = a.astype(BF16)
    return hi, (a - hi.astype(F32)).astype(BF16)


def _dot_lhs_f32(a, b_exact):
    hi, lo = _split2(a)
    return _dot(hi, b_exact) + _dot(lo, b_exact)


def _dot_rhs_f32(a_exact, b):
    hi, lo = _split2(b)
    return _dot(a_exact, hi) + _dot(a_exact, lo)


def _rms_mod(x, g, shift, scale):
    ms = jnp.mean(x * x, axis=-1, keepdims=True)
    return (x * lax.rsqrt(ms + RMS_EPS) * g) * (1.0 + scale) + shift


def _adaln_kernel(c_ref, w_ref, b_ref, o_ref):
    c = c_ref[...]
    s = c * jax.nn.sigmoid(c)
    o_ref[...] = _dot(s, w_ref[...]) + b_ref[...]


def _adaln(cvec, w_mod, b_mod):
    nl, d, n = w_mod.shape
    tn = 1024
    return pl.pallas_call(
        _adaln_kernel,
        grid=(nl, n // tn),
        in_specs=[pl.BlockSpec((8, d), lambda l, j: (0, 0)),
                  pl.BlockSpec((None, d, tn), lambda l, j: (l, 0, j)),
                  pl.BlockSpec((None, 1, tn), lambda l, j: (l, 0, j))],
        out_specs=pl.BlockSpec((None, 8, tn), lambda l, j: (l, 0, j)),
        out_shape=jax.ShapeDtypeStruct((nl, 8, n), F32),
        compiler_params=_cparams(("parallel", "parallel")),
        name="adaln",
    )(cvec, w_mod, b_mod)


def _in_proj_kernel(n_mix_tiles, x_ref, xp_ref, xn_ref, g_ref, sh_ref, sc_ref, w_ref, mix_ref, o_ref, of_ref,
                    h_ref):
    i = pl.program_id(1)
    j = pl.program_id(2)
    tm = x_ref.shape[0]

    @pl.when(pl.program_id(2) == 0)
    def _():
        g, sh, sc = g_ref[...], sh_ref[...], sc_ref[...]
        prev = jnp.where(i > 0, _rms_mod(xp_ref[...], g, sh, sc), 0.0)
        nxt = jnp.where(i < pl.num_programs(1) - 1, _rms_mod(xn_ref[...], g, sh, sc), 0.0)
        h_ref[0:HALO, :] = prev.astype(BF16)
        h_ref[HALO:HALO + tm, :] = _rms_mod(x_ref[...], g, sh, sc).astype(BF16)
        h_ref[HALO + tm:, :] = nxt.astype(BF16)

    @pl.when(j < n_mix_tiles)
    def _():
        p = jnp.dot(h_ref[...], w_ref[...], preferred_element_type=F32)
        mix = mix_ref[...]
        up = pltpu.roll(p, 1, 0)[HALO:HALO + tm]
        dn = pltpu.roll(p, tm + 2 * HALO - 1, 0)[HALO:HALO + tm]
        o_ref[...] = (up * mix[0:1, :] + p[HALO:HALO + tm] * mix[1:2, :] + dn * mix[2:3, :]).astype(o_ref.dtype)

    last = pl.num_programs(2) - 1

    @pl.when((j >= n_mix_tiles) & (j < last))
    def _():
        o_ref[...] = jnp.dot(h_ref[HALO:HALO + tm, :], w_ref[...],
                             preferred_element_type=F32).astype(o_ref.dtype)

    @pl.when(j == last)
    def _():
        of_ref[...] = jnp.dot(h_ref[HALO:HALO + tm, :], w_ref[...], preferred_element_type=F32)


def _in_proj(x, g, shift, scale, w, mix):
    b, l, d = x.shape
    tn = COL_TILE
    w, layer = w
    n = w.shape[2]
    tm = min(1024, l)
    n_mix_tiles = mix.shape[1] // tn
    assert n_mix_tiles * tn == mix.shape[1]
    hb = tm // HALO
    nhb = l // HALO
    vec = pl.BlockSpec((None, 1, d), lambda bi, i, j: (bi, 0, 0))
    return pl.pallas_call(
        functools.partial(_in_proj_kernel, n_mix_tiles),
        grid=(b, l // tm, n // tn),
        in_specs=[pl.BlockSpec((None, tm, d), lambda bi, i, j: (bi, i, 0)),
                  pl.BlockSpec((None, HALO, d), lambda bi, i, j: (bi, jnp.maximum(i * hb - 1, 0), 0)),
                  pl.BlockSpec((None, HALO, d), lambda bi, i, j: (bi, jnp.minimum((i + 1) * hb, nhb - 1), 0)),
                  pl.BlockSpec((1, d), lambda bi, i, j: (0, 0)),
                  vec, vec,
                  pl.BlockSpec((None, d, tn), lambda bi, i, j: (layer, 0, j)),
                  pl.BlockSpec((3, tn), lambda bi, i, j: (0, jnp.minimum(j, n_mix_tiles - 1)))],
        out_specs=[pl.BlockSpec((None, tm, tn), lambda bi, i, j: (bi, i, jnp.minimum(j, n // tn - 2))),
                   pl.BlockSpec((None, tm, tn), lambda bi, i, j: (bi, i, 0))],
        out_shape=[jax.ShapeDtypeStruct((b, l, n - tn), BF16), jax.ShapeDtypeStruct((b, l, tn), F32)],
        scratch_shapes=[pltpu.VMEM((tm + 2 * HALO, d), BF16)],
        compiler_params=_cparams(("parallel", "parallel", "arbitrary")),
        name="in_proj",
    )(x, x, x, g, shift, scale, w, mix)


def _block_masks(width):
    r = lax.broadcasted_iota(jnp.int32, (width, width), 0)
    c = lax.broadcasted_iota(jnp.int32, (width, width), 1)
    return (r // HEAD) == (c // HEAD), r % HEAD, c % HEAD, r == c


def _rs(x, bd):
    return jnp.where(bd, jnp.concatenate([x] * PAIR_HEADS, axis=0), 0.0)


def _ls(x_rs):
    out = x_rs[0:CHUNK]
    for h in range(1, PAIR_HEADS):
        out = out + x_rs[h * CHUNK:(h + 1) * CHUNK]
    return out


def _each(f, *lists):
    return [f(*a) for a in zip(*lists)]


def _chunk_affine(chunks, masks):
    bd, tt, ss, eye = masks
    row = lax.broadcasted_iota(jnp.int32, (CHUNK, CHUNK), 0)
    col = lax.broadcasted_iota(jnp.int32, (CHUNK, CHUNK), 1)
    tri = {rev: jnp.where((row <= col) if rev else (row >= col), 1.0, 0.0).astype(BF16) for rev in (False, True)}
    strict = {False: bd & (ss < tt), True: bd & (ss > tt)}
    r2 = lax.broadcasted_iota(jnp.int32, (PW, 2 * PW), 0)
    c2 = lax.broadcasted_iota(jnp.int32, (PW, 2 * PW), 1)
    bd2 = (r2 // HEAD) == ((c2 % PW) // HEAD)
    incl2 = {False: bd2 & (c2 % HEAD <= r2 % HEAD), True: bd2 & (c2 % HEAD >= r2 % HEAD)}

    crev = [ch[6] for ch in chunks]
    logw = [ch[5] for ch in chunks]
    cum = _each(lambda rv, lw: _dot_rhs_f32(tri[rv], lw), crev, logw)
    total = _each(lambda rv, cm: cm[0:1] if rv else cm[CHUNK - 1:CHUNK], crev, cum)
    p_inv = _each(lambda cm: jnp.exp(-cm), cum)
    p_end = _each(lambda tot, cm: jnp.exp(tot - cm), total, cum)
    bvec = [ch[3] * ch[4] for ch in chunks]
    wide = dict(
        a=_each(lambda ch, cm, lw: -ch[3] * jnp.exp(cm - lw), chunks, cum, logw),
        r=_each(lambda ch, cm: ch[0] * jnp.exp(cm), chunks, cum),
        b=_each(lambda b, p: b * p, bvec, p_inv),
        k=_each(lambda ch, p: ch[1] * p, chunks, p_inv),
        v=[ch[2] for ch in chunks],
        be=_each(lambda b, p: b * p, bvec, p_end),
        ke=_each(lambda ch, p: ch[1] * p, chunks, p_end),
        pc=_each(jnp.exp, total),
    )

    def pairs(name):
        return [x[:, p * PW:(p + 1) * PW] for x in wide[name] for p in range(N_PAIR)]

    rev = [rv for rv in crev for _ in range(N_PAIR)]
    r_t = pairs("r")
    a_rs = _each(lambda x: _rs(x, bd).astype(BF16), pairs("a"))
    r_rs = _each(lambda x: _rs(x, bd).astype(BF16), r_t)
    b_rs = _each(lambda x: jnp.concatenate([x.astype(BF16)] * PAIR_HEADS, axis=0), pairs("b"))
    k_rs = _each(lambda x: jnp.concatenate([x.astype(BF16)] * PAIR_HEADS, axis=0), pairs("k"))
    v_rs = _each(lambda x: _rs(x, bd).astype(BF16), pairs("v"))
    aa = _each(lambda a, r, b, k: _dot(jnp.concatenate([a, r], axis=0), jnp.concatenate([b, k], axis=0), nt=True),
               a_rs, r_rs, b_rs, k_rs)
    a_ab = _each(lambda rv, x: jnp.where(strict[rv], x[0:PW, 0:PW], 0.0), rev, aa)
    a_ak = _each(lambda rv, x: jnp.where(strict[rv], x[0:PW, PW:], 0.0), rev, aa)
    a_r = _each(lambda rv, x: jnp.where(incl2[rv], x[PW:, :], 0.0).astype(BF16), rev, aa)
    t_inv = _each(lambda a: jnp.where(eye, 1.0, a), a_ab)
    pw = _each(lambda a: _dot(a, a), a_ab)
    for _ in range(int(math.log2(CHUNK)) - 2):
        res = _each(lambda p, t: _dot(p, jnp.concatenate([p, t], axis=1)), pw, t_inv)
        pw = [x[:, 0:PW] for x in res]
        t_inv = _each(lambda t, x: t + x[:, PW:], t_inv, res)
    t_inv = _each(lambda p, t: (t + _dot(p, t)).astype(BF16), pw, t_inv)
    akv = _each(_dot, a_ak, v_rs)
    tw = _each(lambda t, a, x: _dot(t, jnp.concatenate([a, x.astype(BF16)], axis=1)).astype(BF16),
               t_inv, a_rs, akv)
    bk_t = _each(lambda b, k: jnp.concatenate([_rs(b, bd), _rs(k, bd)], axis=0).T.astype(BF16),
                 pairs("be"), pairs("ke"))
    fin = _each(lambda ar, bk, w, v: _dot(jnp.concatenate([ar, bk], axis=0),
                                          jnp.concatenate([w, jnp.concatenate([jnp.zeros_like(v), v], axis=1)],
                                                          axis=0)),
                a_r, bk_t, tw, v_rs)
    rh = _each(lambda r, f: r + _ls(f[0:PW, 0:PW]), r_t, fin)
    y0 = _each(lambda f: _ls(f[0:PW, PW:]), fin)
    m = _each(lambda pc, f: jnp.where(eye, pc, 0.0) + f[PW:, 0:PW], pairs("pc"), fin)
    nn = [f[PW:, PW:] for f in fin]
    out = list(zip(y0, rh, m, nn))
    return [out[i * N_PAIR:(i + 1) * N_PAIR] for i in range(len(chunks))]


def _head_sum(x, ones_bd):
    return _dot_lhs_f32(x, ones_bd)


def _wkv_kernel(nsub, r_f, k_f, v_f, lo_f, r_b, k_b, v_b, lo_b, pvec, dec_up, iclr_up, g_up, s0,
                yf_ref, yb_ref, z_ref, gate_ref, sfin_ref,
                zst):
    c = pl.program_id(2)
    nsteps = pl.num_programs(2)

    @pl.when(c == 0)
    def _():
        zst[...] = s0[...]

    masks = _block_masks(PW)
    ones_bd = jnp.where(_block_masks(GW)[0], 1.0, 0.0).astype(BF16)
    pv = pvec[...]
    k_k, k_a, r_k = pv[0:1], pv[1:2], pv[2:3]
    w0 = (pv[3:4], pv[4:5])
    a0 = (pv[5:6], pv[6:7])
    sig_scale = math.exp(-0.5)

    def streams(r_ref, k_ref, v_ref, lo_ref, d):
        r, k, v, lo = (ref[...].astype(F32) for ref in (r_ref, k_ref, v_ref, lo_ref))
        wd = jnp.tanh(lo[:, 0:LORA_PAD])
        ad = lo[:, LORA_PAD:2 * LORA_PAD]
        gd = lo[:, 2 * LORA_PAD:]
        kk = k * k_k
        kkn = kk * lax.rsqrt(jnp.maximum(_head_sum(kk * kk, ones_bd), KK_EPS * KK_EPS))
        logw = -sig_scale * jax.nn.sigmoid(w0[d] + _dot(wd, dec_up[d]))
        aic = jax.nn.sigmoid(a0[d] + _dot(ad, iclr_up[d]))
        kd = k * (1.0 + (aic - 1.0) * k_a)
        return r, k, v, kkn, logw, aic, kd, ad, gd

    r, k, v, kkn, logw, aic, kd, ad, gd = streams(r_f, k_f, v_f, lo_f, 0)
    aic_o = jax.nn.sigmoid(a0[1] + _dot(ad, iclr_up[1]))
    kd_o = k * (1.0 + (aic_o - 1.0) * k_a)
    z_ref[...] = _head_sum(r * (kd + kd_o) * r_k, ones_bd) * v
    gate_ref[...] = _dot(jax.nn.sigmoid(gd), g_up[...])
    chains = [tuple(t[j * CHUNK:(j + 1) * CHUNK] for t in (r, kd, v, kkn, aic, logw)) + (False,)
              for j in range(nsub)]
    r, k, v, kkn, logw, aic, kd, ad, gd = streams(r_b, k_b, v_b, lo_b, 1)
    chains += [tuple(t[j * CHUNK:(j + 1) * CHUNK] for t in (r, kd, v, kkn, aic, logw)) + (True,)
               for j in range(nsub)]
    affine = _chunk_affine(chains, masks)
    fwd, bwd = affine[:nsub], affine[nsub:]

    for d, (steps, y_ref) in enumerate(((list(range(nsub)), yf_ref), (list(reversed(range(nsub))), yb_ref))):
        affine_d = fwd if d == 0 else bwd
        z = [zst[d, p] for p in range(N_PAIR)]
        for j in steps:
            mz = [_dot(jnp.concatenate([affine_d[j][p][2], affine_d[j][p][1]], axis=0), z[p])
                  for p in range(N_PAIR)]
            y_ref[j * CHUNK:(j + 1) * CHUNK, :] = jnp.concatenate(
                [affine_d[j][p][0] + mz[p][PW:] for p in range(N_PAIR)], axis=1)
            z = [mz[p][0:PW] + affine_d[j][p][3] for p in range(N_PAIR)]
        for p in range(N_PAIR):
            zst[d, p] = z[p]

    @pl.when(c == nsteps - 1)
    def _():
        sfin_ref[...] = zst[...]


def _wkv(px, lw, s0, nsub):
    b, l, _ = px.shape
    ng = lw["ng"]
    d_rwkv = ng * GW
    t = nsub * CHUNK
    nsteps = l // t
    lo_blk = 3 * d_rwkv // (4 * LORA_PAD)
    lo_w = 4 * LORA_PAD

    def main_specs(blk):
        return [pl.BlockSpec((None, t, GW), lambda bi, g, c, s=s: (bi, blk(c), s * ng + g)) for s in range(3)] + [
            pl.BlockSpec((None, t, lo_w), lambda bi, g, c: (bi, blk(c), lo_blk))]

    fblk = lambda c: c
    bblk = lambda c: nsteps - 1 - c
    in_specs = (main_specs(fblk) + main_specs(bblk) + [
        pl.BlockSpec((8, GW), lambda bi, g, c: (0, g)),
        pl.BlockSpec((2, LORA_PAD, GW), lambda bi, g, c: (0, 0, g)),
        pl.BlockSpec((2, LORA_PAD, GW), lambda bi, g, c: (0, 0, g)),
        pl.BlockSpec((2 * LORA_PAD, GW), lambda bi, g, c: (0, g)),
        pl.BlockSpec((None, 2, N_PAIR, PW, PW), lambda bi, g, c: (bi, 0, g, 0, 0)),
    ])
    yspec = lambda blk: pl.BlockSpec((None, t, GW), lambda bi, g, c: (bi, blk(c), g))
    out_specs = [yspec(fblk), yspec(bblk), yspec(fblk), yspec(fblk),
                 pl.BlockSpec((None, 2, N_PAIR, PW, PW), lambda bi, g, c: (bi, 0, g, 0, 0))]
    ysh = jax.ShapeDtypeStruct((b, l, d_rwkv), F32)
    return pl.pallas_call(
        functools.partial(_wkv_kernel, nsub),
        grid=(b, ng, nsteps),
        in_specs=in_specs,
        out_specs=out_specs,
        out_shape=[ysh, ysh, ysh, ysh, jax.ShapeDtypeStruct(s0.shape, F32)],
        scratch_shapes=[pltpu.VMEM((2, N_PAIR, PW, PW), F32)],
        compiler_params=_cparams(("parallel", "parallel", "arbitrary")),
        name="wkv",
    )(*([px] * 8), lw["pvec"], lw["dec_up"], lw["iclr_up"], lw["g_up"], s0)


def _dft_ch_kernel(u_ref, t_ref, o_ref):
    u = u_ref[...]
    hi = u.astype(BF16)
    lo = (u - hi.astype(F32)).astype(BF16)
    tab = t_ref[...]
    o_ref[...] = (_dot(hi, tab) + _dot(lo, tab)).astype(BF16)


def _dft_channels(px, tabs, ft_blk, d_f):
    b, l, _ = px.shape
    tm = min(512, l)
    nt = l // tm
    return pl.pallas_call(
        _dft_ch_kernel,
        grid=(b, 2, nt),
        in_specs=[pl.BlockSpec((None, tm, d_f), lambda bi, s, i: (bi, i, ft_blk)),
                  pl.BlockSpec((None, d_f, d_f), lambda bi, s, i: (s, 0, 0))],
        out_specs=pl.BlockSpec((tm, d_f), lambda bi, s, i: (s * nt + i, bi)),
        out_shape=jax.ShapeDtypeStruct((2 * l, b * d_f), BF16),
        compiler_params=_cparams(("parallel", "parallel", "parallel")),
        name="dft_channels",
    )(px, tabs)


def _dft_seq_kernel(w_ref, u_ref, o_ref, acc_ref):
    kk = pl.program_id(1)

    @pl.when(kk == 0)
    def _():
        acc_ref[...] = jnp.zeros_like(acc_ref)

    acc_ref[...] += jnp.dot(w_ref[...], u_ref[...], preferred_element_type=F32)

    @pl.when(kk == pl.num_programs(1) - 1)
    def _():
        o_ref[...] = acc_ref[...]


def _dft_seq(wtab, uu):
    l, k2 = wtab.shape
    n = uu.shape[1]
    tm = min(1024, l)
    tk = min(2048, k2)
    return pl.pallas_call(
        _dft_seq_kernel,
        grid=(l // tm, k2 // tk),
        in_specs=[pl.BlockSpec((tm, tk), lambda i, kk: (i, kk)),
                  pl.BlockSpec((tk, n), lambda i, kk: (kk, 0))],
        out_specs=pl.BlockSpec((tm, n), lambda i, kk: (i, 0)),
        out_shape=jax.ShapeDtypeStruct((l, n), F32),
        scratch_shapes=[pltpu.VMEM((tm, n), F32)],
        compiler_params=_cparams(("parallel", "arbitrary")),
        name="dft_seq",
    )(wtab, uu)


def _seq_table(l):
    f = math.gcd(l, 128)

    def thin(n, stride):
        p = lax.broadcasted_iota(jnp.int32, (l, n), 0)
        q = lax.broadcasted_iota(jnp.int32, (l, n), 1) * stride
        ang = ((p * q) % l).astype(F32) * (2.0 * math.pi / l)
        return jnp.cos(ang), jnp.sin(ang)

    c1, s1 = (t[:, :, None] for t in thin(l // f, f))
    c2, s2 = (t[:, None, :] for t in thin(f, 1))
    return jnp.concatenate([(c1 * c2 - s1 * s2).reshape(l, l), (-(s1 * c2 + c1 * s2)).reshape(l, l)],
                           axis=1).astype(BF16)


def _channel_tables(l, d_f):
    cc = lax.broadcasted_iota(jnp.int32, (d_f, d_f), 0)
    qq = lax.broadcasted_iota(jnp.int32, (d_f, d_f), 1)
    a2 = (((cc % HEAD) * (qq % HEAD)) % HEAD).astype(F32) * (2.0 * math.pi / HEAD)
    same = (cc // HEAD) == (qq // HEAD)
    scale = 1.0 / math.sqrt(l * HEAD)
    return jnp.stack([jnp.where(same, jnp.cos(a2), 0.0), jnp.where(same, jnp.sin(a2), 0.0)]) * scale


FFT_SUB = 8


def _fft_stage1_kernel(*refs):
    u_refs, (ct_ref, t2_ref, tw_ref, o_ref) = refs[:-4], refs[-4:]
    i = pl.program_id(1)
    l2 = t2_ref.shape[0] // 2
    for j in range(FFT_SUB):
        u = jnp.concatenate([r[pl.ds(i * FFT_SUB + j, l2, stride=FFT_L1), :] for r in u_refs], axis=1)
        z = jnp.concatenate([_dot(u, ct_ref[0]), -_dot(u, ct_ref[1])], axis=0)
        bm = _dot(t2_ref[...], z)
        br, bi = bm[0:l2], bm[l2:]
        tw = tw_ref[j]
        cs, sn = tw[:, 0:1], tw[:, 1:2]
        comp = (br * cs + bi * sn, bi * cs - br * sn)
        for c in range(2):
            for pb in range(l2 // FFT_SUB):
                o_ref[pb, c, j * FFT_SUB:(j + 1) * FFT_SUB, :] = comp[c][pb * FFT_SUB:(pb + 1) * FFT_SUB]


def _fft_stage2_kernel(t1_ref, b_ref, o_ref):
    x = jnp.concatenate([b_ref[0], b_ref[1]], axis=0)
    y = _dot(t1_ref[...], x)
    for p1 in range(o_ref.shape[0]):
        o_ref[p1] = y[p1 * FFT_SUB:(p1 + 1) * FFT_SUB]


def _fft_tables(l):
    l1, l2 = FFT_L1, l // FFT_L1

    def cs(n, m, period):
        p = lax.broadcasted_iota(jnp.int32, (n, m), 0)
        q = lax.broadcasted_iota(jnp.int32, (n, m), 1)
        ang = ((p * q) % period).astype(F32) * (2.0 * math.pi / period)
        return jnp.cos(ang), jnp.sin(ang)

    c2, s2 = cs(l2, l2, l2)
    t2 = jnp.concatenate([jnp.concatenate([c2, s2], axis=1), jnp.concatenate([-s2, c2], axis=1)], axis=0)
    tw = jnp.stack(cs(l1, l2, l), axis=-1)
    c1, s1 = cs(l1, l1, l1)
    t1 = jnp.kron(jnp.concatenate([c1, s1], axis=1), jnp.eye(FFT_SUB, dtype=F32))
    return t2.astype(BF16), tw, t1.astype(BF16)


def _fourier_fft(px, ctab, tabs, ft_blk, d_f):
    b, l, n = px.shape
    l1, l2 = FFT_L1, l // FFT_L1
    npb = l2 // FFT_SUB
    nslab = d_f // LANES
    t2, tw, t1 = tabs
    stage1 = pl.pallas_call(
        _fft_stage1_kernel,
        grid=(b, l1 // FFT_SUB),
        in_specs=[pl.BlockSpec((None, l, LANES), lambda bi, i, k=k: (bi, 0, ft_blk * nslab + k))
                  for k in range(nslab)] + [
                  pl.BlockSpec(ctab.shape, lambda bi, i: (0, 0, 0)),
                  pl.BlockSpec(t2.shape, lambda bi, i: (0, 0)),
                  pl.BlockSpec((FFT_SUB, l2, 2), lambda bi, i: (i, 0, 0))],
        out_specs=pl.BlockSpec((None, npb, 2, FFT_SUB * FFT_SUB, d_f), lambda bi, i: (bi, 0, 0, i, 0)),
        out_shape=jax.ShapeDtypeStruct((b, npb, 2, l1 * FFT_SUB, d_f), F32),
        compiler_params=_cparams(("parallel", "arbitrary")),
        name="fft_stage1",
    )(*([px] * nslab), ctab, t2, tw)
    y = pl.pallas_call(
        _fft_stage2_kernel,
        grid=(b, npb),
        in_specs=[pl.BlockSpec(t1.shape, lambda bi, j: (0, 0)),
                  pl.BlockSpec((None, None, 2, l1 * FFT_SUB, d_f), lambda bi, j: (bi, j, 0, 0, 0))],
        out_specs=pl.BlockSpec((None, l1, None, FFT_SUB, d_f), lambda bi, j: (bi, 0, j, 0, 0)),
        out_shape=jax.ShapeDtypeStruct((b, l1, npb, FFT_SUB, d_f), F32),
        compiler_params=_cparams(("parallel", "parallel")),
        name="fft_stage2",
    )(t1, stage1)
    return y.reshape(b, l, d_f)


def _fourier_dense(px, wtab, ctab, ft_blk, d_f):
    b, l, _ = px.shape
    y = _dft_seq(wtab, _dft_channels(px, ctab, ft_blk, d_f))
    return y.reshape(l, b, d_f).transpose(1, 0, 2)


def _mix_out_kernel(period, yf_ref, yb_ref, z_ref, gate_ref, cg_ref, cx_ref, cb_ref, ft_ref, x_ref,
                    ga_ref, lnw_ref, lnb_ref, cw_ref, w_ref, o_ref, mix_ref):
    d_rwkv = yf_ref.shape[1]
    d_conv = cg_ref.shape[1]
    tm = yf_ref.shape[0]
    r = lax.broadcasted_iota(jnp.int32, (GW, GW), 0)
    c = lax.broadcasted_iota(jnp.int32, (GW, GW), 1)
    ones_bd = jnp.where((r // HEAD) == (c // HEAD), 1.0, 0.0).astype(BF16)
    inv_n = 1.0 / HEAD
    for s in range(d_rwkv // GW):
        sl = slice(s * GW, (s + 1) * GW)
        y = yf_ref[:, sl] + yb_ref[:, sl]
        mu = _dot_lhs_f32(y, ones_bd) * inv_n
        dlt = y - mu
        var = _dot_lhs_f32(dlt * dlt, ones_bd) * inv_n
        yn = dlt * lax.rsqrt(var + GN_EPS) * lnw_ref[:, sl] + lnb_ref[:, sl] + z_ref[:, sl]
        mix_ref[:, sl] = (yn * gate_ref[:, sl]).astype(BF16)
    u = cg_ref[...].astype(F32) * cx_ref[...].astype(F32)
    rowid = lax.broadcasted_iota(jnp.int32, u.shape, 0) % period
    up = jnp.where(rowid == 0, 0.0, pltpu.roll(u, 1, 0))
    dn = jnp.where(rowid == period - 1, 0.0, pltpu.roll(u, tm - 1, 0))
    cw = cw_ref[...]
    conv = cb_ref[...].astype(F32) * (up * cw[0:1] + u * cw[1:2] + dn * cw[2:3])
    mix_ref[:, d_rwkv:d_rwkv + d_conv] = conv.astype(BF16)
    mix_ref[:, d_rwkv + d_conv:] = ft_ref[...].astype(BF16)
    o_ref[...] = x_ref[...] + ga_ref[...] * jnp.dot(mix_ref[...], w_ref[...], preferred_element_type=F32)


def _mix_out(yf, yb, z, gate, px, ft, x, ga, lw, period, conv_blk):
    b, l, d = x.shape
    d_rwkv = yf.shape[2]
    d_conv = lw["conv_w"].shape[1]
    w_out, layer = lw["w_out"]
    d_f = ft.shape[2]
    tm = min(256, l)
    assert tm % period == 0
    yspec = pl.BlockSpec((None, tm, d_rwkv), lambda bi, i: (bi, i, 0))
    cspec = lambda off: pl.BlockSpec((None, tm, d_conv), lambda bi, i: (bi, i, conv_blk + off))
    return pl.pallas_call(
        functools.partial(_mix_out_kernel, period),
        grid=(b, l // tm),
        in_specs=[yspec, yspec, yspec, yspec, cspec(0), cspec(1), cspec(2),
                  pl.BlockSpec((None, tm, d_f), lambda bi, i: (bi, i, 0)),
                  pl.BlockSpec((None, tm, d), lambda bi, i: (bi, i, 0)),
                  pl.BlockSpec((None, 1, d), lambda bi, i: (bi, 0, 0)),
                  pl.BlockSpec((1, d_rwkv), lambda bi, i: (0, 0)),
                  pl.BlockSpec((1, d_rwkv), lambda bi, i: (0, 0)),
                  pl.BlockSpec((3, d_conv), lambda bi, i: (0, 0)),
                  pl.BlockSpec((None,) + w_out.shape[1:], lambda bi, i: (layer, 0, 0))],
        out_specs=pl.BlockSpec((None, tm, d), lambda bi, i: (bi, i, 0)),
        out_shape=jax.ShapeDtypeStruct((b, l, d), F32),
        scratch_shapes=[pltpu.VMEM((tm, w_out.shape[1]), BF16)],
        compiler_params=_cparams(("parallel", "parallel")),
        name="mix_out",
    )(yf, yb, z, gate, px, px, px, ft, x, ga, lw["ln_w"], lw["ln_b"], lw["conv_w"], w_out)


def _ffn_kernel(final, x_ref, g_ref, sh_ref, sc_ref, ga_ref, wg_ref, wu_ref, wd_ref, gf_ref, o_ref,
                h_ref, acc_ref):
    j = pl.program_id(2)

    @pl.when(j == 0)
    def _():
        h_ref[...] = _rms_mod(x_ref[...], g_ref[...], sh_ref[...], sc_ref[...]).astype(BF16)
        acc_ref[...] = jnp.zeros_like(acc_ref)

    h = h_ref[...]
    a = jnp.dot(h, wg_ref[...], preferred_element_type=F32)
    u = jnp.dot(h, wu_ref[...], preferred_element_type=F32)
    t = (a * jax.nn.sigmoid(a)) * u
    acc_ref[...] += jnp.dot(t.astype(BF16), wd_ref[...], preferred_element_type=F32)

    @pl.when(j == pl.num_programs(2) - 1)
    def _():
        xn = x_ref[...] + ga_ref[...] * acc_ref[...]
        if final:
            ms = jnp.mean(xn * xn, axis=-1, keepdims=True)
            xn = xn * lax.rsqrt(ms + RMS_EPS) * gf_ref[...]
        o_ref[...] = xn


def _ffn(x, g, shift, scale, ga, wg, wu, wd, g_final, final):
    b, l, d = x.shape
    tf = COL_TILE
    (wg, layer), (wu, _), (wd, _) = wg, wu, wd
    ff = wg.shape[2]
    tm = min(512, l)
    vec = pl.BlockSpec((None, 1, d), lambda bi, i, j: (bi, 0, 0))
    gspec = pl.BlockSpec((1, d), lambda bi, i, j: (0, 0))
    return pl.pallas_call(
        functools.partial(_ffn_kernel, final),
        grid=(b, l // tm, ff // tf),
        in_specs=[pl.BlockSpec((None, tm, d), lambda bi, i, j: (bi, i, 0)),
                  gspec, vec, vec, vec,
                  pl.BlockSpec((None, d, tf), lambda bi, i, j: (layer, 0, j)),
                  pl.BlockSpec((None, d, tf), lambda bi, i, j: (layer, 0, j)),
                  pl.BlockSpec((None, tf, d), lambda bi, i, j: (layer, j, 0)),
                  gspec],
        out_specs=pl.BlockSpec((None, tm, d), lambda bi, i, j: (bi, i, 0)),
        out_shape=jax.ShapeDtypeStruct((b, l, d), F32),
        scratch_shapes=[pltpu.VMEM((tm, d), BF16), pltpu.VMEM((tm, d), F32)],
        compiler_params=_cparams(("parallel", "parallel", "arbitrary")),
        name="ffn",
    )(x, g, shift, scale, ga, wg, wu, wd, g_final)


def _pad_lora(m, dims, stop):
    d_rwkv, _, _, dl, il, _ = dims
    ad0 = 3 * d_rwkv + dl
    gd0 = ad0 + il
    padl = jnp.zeros(m.shape[:-1] + (LORA_PAD - dl,), m.dtype)
    padi = jnp.zeros(m.shape[:-1] + (LORA_PAD - il,), m.dtype)
    return jnp.concatenate([m[..., :ad0], padl, m[..., ad0:gd0], padi, m[..., gd0:stop]], axis=-1)


def _layer_weights(i, big, rw_shift, dec_w0, dec_up, iclr_a0, iclr_up, k_k, k_a, r_k, ln_w, ln_b, g_up,
                   conv_w, dims):
    d_rwkv, d_conv, d_f, dl, il, gl = dims
    ng = d_rwkv // GW
    rest0 = 3 * d_rwkv + dl + il + gl
    pad_rows = lambda m, n: jnp.concatenate([m, jnp.zeros(m.shape[:-2] + (n - m.shape[-2], m.shape[-1]), m.dtype)], -2)
    zero = jnp.zeros((d_rwkv,), F32)
    pvec = jnp.stack([k_k[i], k_a[i], r_k[i].reshape(-1), dec_w0[i, 0], dec_w0[i, 1],
                      iclr_a0[i, 0], iclr_a0[i, 1], zero])
    return dict(
        ng=ng,
        w_in=(big["w_in"], i),
        mix=_pad_lora(rw_shift[i], dims, rest0),
        pvec=pvec,
        dec_up=pad_rows(dec_up[i], LORA_PAD).astype(BF16),
        iclr_up=pad_rows(iclr_up[i], LORA_PAD).astype(BF16),
        g_up=g_up[i].astype(BF16),
        ln_w=ln_w[i][None, :], ln_b=ln_b[i][None, :],
        conv_w=conv_w[i],
        w_out=(big["w_out"], i),
        w_gate=(big["w_gate"], i), w_up=(big["w_up"], i), w_down=(big["w_down"], i),
    )


def kernel(x, c, ctx, c_ctx, w_mod, b_mod, norm_mix, w_in, rw_shift, dec_w0, dec_up, iclr_a0, iclr_up,
           k_k, k_a, r_k, ln_w, ln_b, g_up, conv_w, w_out, norm_ffn, w_gate, w_up, w_down, norm_final):
    b, l, d = x.shape
    lc = ctx.shape[1]
    depth = w_mod.shape[0]
    d_rwkv = k_k.shape[1]
    d_conv = conv_w.shape[2]
    dl, il, gl = dec_up.shape[2], iclr_up.shape[2], g_up.shape[1]
    d_f = w_out.shape[1] - d_rwkv - d_conv
    assert dl <= LORA_PAD and il <= LORA_PAD and gl == 2 * LORA_PAD and d_rwkv % GW == 0
    assert l % GRID_W == 0 and l % CHUNK == 0 and lc % CHUNK == 0
    nsub_x = math.gcd(l // CHUNK, WKV_SUB)
    nsub_c = math.gcd(lc // CHUNK, WKV_SUB)
    ng = d_rwkv // GW
    n_rw = 3 * d_rwkv + 4 * LORA_PAD
    conv_blk = n_rw // d_conv
    assert conv_blk * d_conv == n_rw and d_f == COL_TILE

    cvec = jnp.concatenate([c, c_ctx[None, :], jnp.zeros((8 - b - 1, d), F32)], axis=0)
    mods = _adaln(cvec, w_mod, b_mod[:, None, :])

    def mod(i, j, ctx_rows):
        m = mods[i, :, j * d:(j + 1) * d]
        if ctx_rows:
            return jnp.broadcast_to(m[b:b + 1], (b, d))[:, None, :]
        return m[:b][:, None, :]

    fft_x = l % (FFT_L1 * 128) == 0
    wtab_c, ctab_c = _seq_table(lc), _channel_tables(lc, d_f)
    ctab_x = _channel_tables(l, d_f)
    if fft_x:
        ftab_x = _fft_tables(l)
    else:
        wtab_x = _seq_table(l)
    dims = (d_rwkv, d_conv, d_f, dl, il, gl)
    s_zero = jnp.zeros((b, 2, ng * N_PAIR, PW, PW), F32)
    gfin = norm_final[None, :]
    big = dict(w_in=_pad_lora(w_in.astype(BF16), dims, w_in.shape[2]), w_out=w_out.astype(BF16),
               w_gate=w_gate.astype(BF16), w_up=w_up.astype(BF16), w_down=w_down.astype(BF16))
    xc = ctx
    for i in range(depth):
        lw = _layer_weights(i, big, rw_shift, dec_w0, dec_up, iclr_a0, iclr_up, k_k, k_a, r_k, ln_w, ln_b,
                            g_up, conv_w, dims)
        gmix = norm_mix[i][None, :]
        gffn = norm_ffn[i][None, :]
        last = i == depth - 1
        pc, pcf = _in_proj(xc, gmix, mod(i, 0, True), mod(i, 1, True), lw["w_in"], lw["mix"])
        px, pxf = _in_proj(x, gmix, mod(i, 0, False), mod(i, 1, False), lw["w_in"], lw["mix"])
        yfc, ybc, zc, gc, s_ctx = _wkv(pc, lw, s_zero, nsub_c)
        yfx, ybx, zx, gx, _ = _wkv(px, lw, s_ctx, nsub_x)
        if fft_x:
            ftx = _fourier_fft(pxf, ctab_x.astype(BF16), ftab_x, 0, d_f)
        else:
            ftx = _fourier_dense(pxf, wtab_x, ctab_x, 0, d_f)
        x = _mix_out(yfx, ybx, zx, gx, px, ftx, x, mod(i, 2, False), lw, GRID_W, conv_blk)
        x = _ffn(x, gffn, mod(i, 3, False), mod(i, 4, False), mod(i, 5, False),
                 lw["w_gate"], lw["w_up"], lw["w_down"], gfin, last)
        if not last:
            ftc = _fourier_dense(pcf, wtab_c, ctab_c, 0, d_f)
            xc = _mix_out(yfc, ybc, zc, gc, pc, ftc, xc, mod(i, 2, True), lw, lc, conv_blk)
            xc = _ffn(xc, gffn, mod(i, 3, True), mod(i, 4, True), mod(i, 5, True),
                      lw["w_gate"], lw["w_up"], lw["w_down"], gfin, False)
    return x
```

```python
import functools
import math

import jax
import jax.numpy as jnp
from jax import lax
from jax.experimental import pallas as pl
from jax.experimental.pallas import tpu as pltpu

F32 = jnp.float32
BF16 = jnp.bfloat16

HEAD = 64
GROUP_HEADS = 4
GW = GROUP_HEADS * HEAD
PAIR_HEADS = 2
PW = PAIR_HEADS * HEAD
N_PAIR = GW // PW
CHUNK = 64
WKV_SUB = 8
HALO = 16
LANES = 128
COL_TILE = 512
STEP_TILES = 2
LORA_PAD = 128
GRID_W = 64
FFT_L1 = 64
FFT_SUB = 8
ROWS_IN_PROJ = 1024
ROWS_FFN = 512
ROWS_MIX_OUT = 256
ROWS_DFT = 512
ROWS_DFT_SEQ = 1024
DEPTH_DFT_SEQ = 2048
COLS_ADALN = 1024
RMS_EPS = 1e-6
GN_EPS = 64e-5
KK_EPS = 1e-12
VMEM_LIMIT = 56 * 1024 * 1024


def _cparams(sem):
    return pltpu.CompilerParams(dimension_semantics=sem, vmem_limit_bytes=VMEM_LIMIT)


def _dot(a, b, nt=False):
    dn = (((1,), (1,)), ((), ())) if nt else (((1,), (0,)), ((), ()))
    return lax.dot_general(a.astype(BF16), b.astype(BF16), dn, preferred_element_type=F32)


def _split2(a):
    hi = a.astype(BF16)
    return hi, (a - hi.astype(F32)).astype(BF16)


def _dot_lhs_f32(a, b_exact):
    hi, lo = _split2(a)
    return _dot(hi, b_exact) + _dot(lo, b_exact)


def _dot_rhs_f32(a_exact, b):
    hi, lo = _split2(b)
    return _dot(a_exact, hi) + _dot(a_exact, lo)


def _rms_mod(x, g, shift, scale):
    ms = jnp.mean(x * x, axis=-1, keepdims=True)
    return (x * lax.rsqrt(ms + RMS_EPS) * g) * (1.0 + scale) + shift


def _adaln_kernel(c_ref, w_ref, b_ref, o_ref):
    c = c_ref[...]
    s = c * jax.nn.sigmoid(c)
    o_ref[...] = _dot(s, w_ref[...]) + b_ref[...]


def _adaln(cvec, w_mod, b_mod):
    nl, d, n = w_mod.shape
    tn = COLS_ADALN
    return pl.pallas_call(
        _adaln_kernel,
        grid=(nl, n // tn),
        in_specs=[pl.BlockSpec((8, d), lambda l, j: (0, 0)),
                  pl.BlockSpec((None, d, tn), lambda l, j: (l, 0, j)),
                  pl.BlockSpec((None, 1, tn), lambda l, j: (l, 0, j))],
        out_specs=pl.BlockSpec((None, 8, tn), lambda l, j: (l, 0, j)),
        out_shape=jax.ShapeDtypeStruct((nl, 8, n), F32),
        compiler_params=_cparams(("parallel", "parallel")),
        name="adaln",
    )(cvec, w_mod, b_mod)


def _in_proj_kernel(n_mix_steps, x_ref, xp_ref, xn_ref, g_ref, sh_ref, sc_ref, w_ref, mix_ref, o_ref, of_ref,
                    h_ref):
    i = pl.program_id(1)
    j = pl.program_id(2)
    last = pl.num_programs(2) - 1
    tm = x_ref.shape[0]
    tn = of_ref.shape[1]

    @pl.when(j == 0)
    def _():
        g, sh, sc = g_ref[...], sh_ref[...], sc_ref[...]
        prev = jnp.where(i > 0, _rms_mod(xp_ref[...], g, sh, sc), 0.0)
        nxt = jnp.where(i < pl.num_programs(1) - 1, _rms_mod(xn_ref[...], g, sh, sc), 0.0)
        h_ref[0:HALO, :] = prev.astype(BF16)
        h_ref[HALO:HALO + tm, :] = _rms_mod(x_ref[...], g, sh, sc).astype(BF16)
        h_ref[HALO + tm:, :] = nxt.astype(BF16)

    @pl.when(j < n_mix_steps)
    def _():
        for t in range(STEP_TILES):
            sl = slice(t * tn, (t + 1) * tn)
            p = jnp.dot(h_ref[...], w_ref[:, sl], preferred_element_type=F32)
            mix = mix_ref[:, sl]
            up = pltpu.roll(p, 1, 0)[HALO:HALO + tm]
            dn = pltpu.roll(p, tm + 2 * HALO - 1, 0)[HALO:HALO + tm]
            o_ref[:, sl] = (up * mix[0:1, :] + p[HALO:HALO + tm] * mix[1:2, :] + dn * mix[2:3, :]).astype(o_ref.dtype)

    @pl.when((j >= n_mix_steps) & (j < last))
    def _():
        o_ref[...] = jnp.dot(h_ref[HALO:HALO + tm, :], w_ref[...],
                             preferred_element_type=F32).astype(o_ref.dtype)

    @pl.when(j == last)
    def _():
        of_ref[...] = jnp.dot(h_ref[HALO:HALO + tm, :], w_ref[:, 0:tn], preferred_element_type=F32)


def _in_proj(x, g, shift, scale, w, mix):
    b, l, d = x.shape
    tn = COL_TILE
    ts = STEP_TILES * tn
    w, layer = w
    steps = w.shape[2] // ts
    n_mix_steps = mix.shape[1] // ts
    assert steps * ts == w.shape[2] and n_mix_steps * ts == mix.shape[1] and n_mix_steps < steps
    tm = min(ROWS_IN_PROJ, l)
    hb = tm // HALO
    nhb = l // HALO
    vec = pl.BlockSpec((None, 1, d), lambda bi, i, j: (bi, 0, 0))
    return pl.pallas_call(
        functools.partial(_in_proj_kernel, n_mix_steps),
        grid=(b, l // tm, steps),
        in_specs=[pl.BlockSpec((None, tm, d), lambda bi, i, j: (bi, i, 0)),
                  pl.BlockSpec((None, HALO, d), lambda bi, i, j: (bi, jnp.maximum(i * hb - 1, 0), 0)),
                  pl.BlockSpec((None, HALO, d), lambda bi, i, j: (bi, jnp.minimum((i + 1) * hb, nhb - 1), 0)),
                  pl.BlockSpec((1, d), lambda bi, i, j: (0, 0)),
                  vec, vec,
                  pl.BlockSpec((None, d, ts), lambda bi, i, j: (layer, 0, j)),
                  pl.BlockSpec((3, ts), lambda bi, i, j: (0, jnp.minimum(j, n_mix_steps - 1)))],
        out_specs=[pl.BlockSpec((None, tm, ts), lambda bi, i, j: (bi, i, jnp.minimum(j, steps - 2))),
                   pl.BlockSpec((None, tm, tn), lambda bi, i, j: (bi, i, 0))],
        out_shape=[jax.ShapeDtypeStruct((b, l, (steps - 1) * ts), BF16), jax.ShapeDtypeStruct((b, l, tn), F32)],
        scratch_shapes=[pltpu.VMEM((tm + 2 * HALO, d), BF16)],
        compiler_params=_cparams(("parallel", "parallel", "arbitrary")),
        name="in_proj",
    )(x, x, x, g, shift, scale, w, mix)


def _block_masks(width):
    r = lax.broadcasted_iota(jnp.int32, (width, width), 0)
    c = lax.broadcasted_iota(jnp.int32, (width, width), 1)
    return (r // HEAD) == (c // HEAD), r % HEAD, c % HEAD, r == c


def _rs(x, bd):
    return jnp.where(bd, jnp.concatenate([x] * PAIR_HEADS, axis=0), 0.0)


def _ls(x_rs):
    out = x_rs[0:CHUNK]
    for h in range(1, PAIR_HEADS):
        out = out + x_rs[h * CHUNK:(h + 1) * CHUNK]
    return out


def _each(f, *lists):
    return [f(*a) for a in zip(*lists)]


def _chunk_affine(chunks, masks):
    bd, tt, ss, eye = masks
    row = lax.broadcasted_iota(jnp.int32, (CHUNK, CHUNK), 0)
    col = lax.broadcasted_iota(jnp.int32, (CHUNK, CHUNK), 1)
    tri = {rev: jnp.where((row <= col) if rev else (row >= col), 1.0, 0.0).astype(BF16) for rev in (False, True)}
    strict = {False: bd & (ss < tt), True: bd & (ss > tt)}
    r2 = lax.broadcasted_iota(jnp.int32, (PW, 2 * PW), 0)
    c2 = lax.broadcasted_iota(jnp.int32, (PW, 2 * PW), 1)
    bd2 = (r2 // HEAD) == ((c2 % PW) // HEAD)
    incl2 = {False: bd2 & (c2 % HEAD <= r2 % HEAD), True: bd2 & (c2 % HEAD >= r2 % HEAD)}

    crev = [ch[6] for ch in chunks]
    logw = [ch[5] for ch in chunks]
    cum = _each(lambda rv, lw: _dot_rhs_f32(tri[rv], lw), crev, logw)
    total = _each(lambda rv, cm: cm[0:1] if rv else cm[CHUNK - 1:CHUNK], crev, cum)
    p_inv = _each(lambda cm: jnp.exp(-cm), cum)
    p_end = _each(lambda tot, cm: jnp.exp(tot - cm), total, cum)
    bvec = [ch[3] * ch[4] for ch in chunks]
    wide = dict(
        a=_each(lambda ch, cm, lw: -ch[3] * jnp.exp(cm - lw), chunks, cum, logw),
        r=_each(lambda ch, cm: ch[0] * jnp.exp(cm), chunks, cum),
        b=_each(lambda b, p: b * p, bvec, p_inv),
        k=_each(lambda ch, p: ch[1] * p, chunks, p_inv),
        v=[ch[2] for ch in chunks],
        be=_each(lambda b, p: b * p, bvec, p_end),
        ke=_each(lambda ch, p: ch[1] * p, chunks, p_end),
        pc=_each(jnp.exp, total),
    )

    def pairs(name):
        return [x[:, p * PW:(p + 1) * PW] for x in wide[name] for p in range(N_PAIR)]

    rev = [rv for rv in crev for _ in range(N_PAIR)]
    r_t = pairs("r")
    a_rs = _each(lambda x: _rs(x, bd).astype(BF16), pairs("a"))
    r_rs = _each(lambda x: _rs(x, bd).astype(BF16), r_t)
    b_rs = _each(lambda x: jnp.concatenate([x.astype(BF16)] * PAIR_HEADS, axis=0), pairs("b"))
    k_rs = _each(lambda x: jnp.concatenate([x.astype(BF16)] * PAIR_HEADS, axis=0), pairs("k"))
    v_rs = _each(lambda x: _rs(x, bd).astype(BF16), pairs("v"))
    aa = _each(lambda a, r, b, k: _dot(jnp.concatenate([a, r], axis=0), jnp.concatenate([b, k], axis=0), nt=True),
               a_rs, r_rs, b_rs, k_rs)
    a_ab = _each(lambda rv, x: jnp.where(strict[rv], x[0:PW, 0:PW], 0.0), rev, aa)
    a_ak = _each(lambda rv, x: jnp.where(strict[rv], x[0:PW, PW:], 0.0), rev, aa)
    a_r = _each(lambda rv, x: jnp.where(incl2[rv], x[PW:, :], 0.0).astype(BF16), rev, aa)
    t_inv = _each(lambda a: jnp.where(eye, 1.0, a), a_ab)
    pw = _each(lambda a: _dot(a, a), a_ab)
    for _ in range(int(math.log2(CHUNK)) - 2):
        res = _each(lambda p, t: _dot(p, jnp.concatenate([p, t], axis=1)), pw, t_inv)
        pw = [x[:, 0:PW] for x in res]
        t_inv = _each(lambda t, x: t + x[:, PW:], t_inv, res)
    t_inv = _each(lambda p, t: (t + _dot(p, t)).astype(BF16), pw, t_inv)
    akv = _each(_dot, a_ak, v_rs)
    tw = _each(lambda t, a, x: _dot(t, jnp.concatenate([a, x.astype(BF16)], axis=1)).astype(BF16),
               t_inv, a_rs, akv)
    bk_t = _each(lambda b, k: jnp.concatenate([_rs(b, bd), _rs(k, bd)], axis=0).T.astype(BF16),
                 pairs("be"), pairs("ke"))
    fin = _each(lambda ar, bk, w, v: _dot(jnp.concatenate([ar, bk], axis=0),
                                          jnp.concatenate([w, jnp.concatenate([jnp.zeros_like(v), v], axis=1)],
                                                          axis=0)),
                a_r, bk_t, tw, v_rs)
    rh = _each(lambda r, f: r + _ls(f[0:PW, 0:PW]), r_t, fin)
    y0 = _each(lambda f: _ls(f[0:PW, PW:]), fin)
    m = _each(lambda pc, f: jnp.where(eye, pc, 0.0) + f[PW:, 0:PW], pairs("pc"), fin)
    nn = [f[PW:, PW:] for f in fin]
    out = list(zip(y0, rh, m, nn))
    return [out[i * N_PAIR:(i + 1) * N_PAIR] for i in range(len(chunks))]


def _head_sum(x, ones_bd):
    return _dot_lhs_f32(x, ones_bd)


def _wkv_kernel(nsub, r_f, k_f, v_f, lo_f, r_b, k_b, v_b, lo_b, pvec, dec_up, iclr_up, g_up, s0,
                yf_ref, yb_ref, z_ref, gate_ref, sfin_ref,
                zst):
    c = pl.program_id(2)
    nsteps = pl.num_programs(2)

    @pl.when(c == 0)
    def _():
        zst[...] = s0[...]

    masks = _block_masks(PW)
    ones_bd = jnp.where(_block_masks(GW)[0], 1.0, 0.0).astype(BF16)
    pv = pvec[...]
    k_k, k_a, r_k = pv[0:1], pv[1:2], pv[2:3]
    w0 = (pv[3:4], pv[4:5])
    a0 = (pv[5:6], pv[6:7])
    sig_scale = math.exp(-0.5)

    def streams(r_ref, k_ref, v_ref, lo_ref, d):
        r, k, v, lo = (ref[...].astype(F32) for ref in (r_ref, k_ref, v_ref, lo_ref))
        wd = jnp.tanh(lo[:, 0:LORA_PAD])
        ad = lo[:, LORA_PAD:2 * LORA_PAD]
        gd = lo[:, 2 * LORA_PAD:]
        kk = k * k_k
        kkn = kk * lax.rsqrt(jnp.maximum(_head_sum(kk * kk, ones_bd), KK_EPS * KK_EPS))
        logw = -sig_scale * jax.nn.sigmoid(w0[d] + _dot(wd, dec_up[d]))
        aic = jax.nn.sigmoid(a0[d] + _dot(ad, iclr_up[d]))
        kd = k * (1.0 + (aic - 1.0) * k_a)
        return r, k, v, kkn, logw, aic, kd, ad, gd

    r, k, v, kkn, logw, aic, kd, ad, gd = streams(r_f, k_f, v_f, lo_f, 0)
    aic_o = jax.nn.sigmoid(a0[1] + _dot(ad, iclr_up[1]))
    kd_o = k * (1.0 + (aic_o - 1.0) * k_a)
    z_ref[...] = _head_sum(r * (kd + kd_o) * r_k, ones_bd) * v
    gate_ref[...] = _dot(jax.nn.sigmoid(gd), g_up[...])
    chains = [tuple(t[j * CHUNK:(j + 1) * CHUNK] for t in (r, kd, v, kkn, aic, logw)) + (False,)
              for j in range(nsub)]
    r, k, v, kkn, logw, aic, kd, ad, gd = streams(r_b, k_b, v_b, lo_b, 1)
    chains += [tuple(t[j * CHUNK:(j + 1) * CHUNK] for t in (r, kd, v, kkn, aic, logw)) + (True,)
               for j in range(nsub)]
    affine = _chunk_affine(chains, masks)
    fwd, bwd = affine[:nsub], affine[nsub:]

    for d, (steps, y_ref) in enumerate(((list(range(nsub)), yf_ref), (list(reversed(range(nsub))), yb_ref))):
        affine_d = fwd if d == 0 else bwd
        z = [zst[d, p] for p in range(N_PAIR)]
        for j in steps:
            mz = [_dot(jnp.concatenate([affine_d[j][p][2], affine_d[j][p][1]], axis=0), z[p])
                  for p in range(N_PAIR)]
            y_ref[j * CHUNK:(j + 1) * CHUNK, :] = jnp.concatenate(
                [affine_d[j][p][0] + mz[p][PW:] for p in range(N_PAIR)], axis=1)
            z = [mz[p][0:PW] + affine_d[j][p][3] for p in range(N_PAIR)]
        for p in range(N_PAIR):
            zst[d, p] = z[p]

    @pl.when(c == nsteps - 1)
    def _():
        sfin_ref[...] = zst[...]


def _wkv(px, lw, s0, nsub):
    b, l, _ = px.shape
    ng = lw["ng"]
    d_rwkv = ng * GW
    t = nsub * CHUNK
    nsteps = l // t
    lo_blk = 3 * d_rwkv // (4 * LORA_PAD)
    lo_w = 4 * LORA_PAD

    def main_specs(blk):
        return [pl.BlockSpec((None, t, GW), lambda bi, g, c, s=s: (bi, blk(c), s * ng + g)) for s in range(3)] + [
            pl.BlockSpec((None, t, lo_w), lambda bi, g, c: (bi, blk(c), lo_blk))]

    fblk = lambda c: c
    bblk = lambda c: nsteps - 1 - c
    in_specs = (main_specs(fblk) + main_specs(bblk) + [
        pl.BlockSpec((8, GW), lambda bi, g, c: (0, g)),
        pl.BlockSpec((2, LORA_PAD, GW), lambda bi, g, c: (0, 0, g)),
        pl.BlockSpec((2, LORA_PAD, GW), lambda bi, g, c: (0, 0, g)),
        pl.BlockSpec((2 * LORA_PAD, GW), lambda bi, g, c: (0, g)),
        pl.BlockSpec((None, 2, N_PAIR, PW, PW), lambda bi, g, c: (bi, 0, g, 0, 0)),
    ])
    yspec = lambda blk: pl.BlockSpec((None, t, GW), lambda bi, g, c: (bi, blk(c), g))
    out_specs = [yspec(fblk), yspec(bblk), yspec(fblk), yspec(fblk),
                 pl.BlockSpec((None, 2, N_PAIR, PW, PW), lambda bi, g, c: (bi, 0, g, 0, 0))]
    ysh = jax.ShapeDtypeStruct((b, l, d_rwkv), F32)
    return pl.pallas_call(
        functools.partial(_wkv_kernel, nsub),
        grid=(b, ng, nsteps),
        in_specs=in_specs,
        out_specs=out_specs,
        out_shape=[ysh, ysh, ysh, ysh, jax.ShapeDtypeStruct(s0.shape, F32)],
        scratch_shapes=[pltpu.VMEM((2, N_PAIR, PW, PW), F32)],
        compiler_params=_cparams(("parallel", "parallel", "arbitrary")),
        name="wkv",
    )(*([px] * 8), lw["pvec"], lw["dec_up"], lw["iclr_up"], lw["g_up"], s0)


def _dft_ch_kernel(u_ref, t_ref, o_ref):
    u = u_ref[...]
    hi = u.astype(BF16)
    lo = (u - hi.astype(F32)).astype(BF16)
    tab = t_ref[...]
    o_ref[...] = (_dot(hi, tab) + _dot(lo, tab)).astype(BF16)


def _dft_channels(px, tabs, ft_blk, d_f):
    b, l, _ = px.shape
    tm = min(ROWS_DFT, l)
    nt = l // tm
    return pl.pallas_call(
        _dft_ch_kernel,
        grid=(b, 2, nt),
        in_specs=[pl.BlockSpec((None, tm, d_f), lambda bi, s, i: (bi, i, ft_blk)),
                  pl.BlockSpec((None, d_f, d_f), lambda bi, s, i: (s, 0, 0))],
        out_specs=pl.BlockSpec((tm, d_f), lambda bi, s, i: (s * nt + i, bi)),
        out_shape=jax.ShapeDtypeStruct((2 * l, b * d_f), BF16),
        compiler_params=_cparams(("parallel", "parallel", "parallel")),
        name="dft_channels",
    )(px, tabs)


def _dft_seq_kernel(w_ref, u_ref, o_ref, acc_ref):
    kk = pl.program_id(1)

    @pl.when(kk == 0)
    def _():
        acc_ref[...] = jnp.zeros_like(acc_ref)

    acc_ref[...] += jnp.dot(w_ref[...], u_ref[...], preferred_element_type=F32)

    @pl.when(kk == pl.num_programs(1) - 1)
    def _():
        o_ref[...] = acc_ref[...]


def _dft_seq(wtab, uu):
    l, k2 = wtab.shape
    n = uu.shape[1]
    tm = min(ROWS_DFT_SEQ, l)
    tk = min(DEPTH_DFT_SEQ, k2)
    return pl.pallas_call(
        _dft_seq_kernel,
        grid=(l // tm, k2 // tk),
        in_specs=[pl.BlockSpec((tm, tk), lambda i, kk: (i, kk)),
                  pl.BlockSpec((tk, n), lambda i, kk: (kk, 0))],
        out_specs=pl.BlockSpec((tm, n), lambda i, kk: (i, 0)),
        out_shape=jax.ShapeDtypeStruct((l, n), F32),
        scratch_shapes=[pltpu.VMEM((tm, n), F32)],
        compiler_params=_cparams(("parallel", "arbitrary")),
        name="dft_seq",
    )(wtab, uu)


def _seq_table(l):
    f = math.gcd(l, 128)

    def thin(n, stride):
        p = lax.broadcasted_iota(jnp.int32, (l, n), 0)
        q = lax.broadcasted_iota(jnp.int32, (l, n), 1) * stride
        ang = ((p * q) % l).astype(F32) * (2.0 * math.pi / l)
        return jnp.cos(ang), jnp.sin(ang)

    c1, s1 = (t[:, :, None] for t in thin(l // f, f))
    c2, s2 = (t[:, None, :] for t in thin(f, 1))
    return jnp.concatenate([(c1 * c2 - s1 * s2).reshape(l, l), (-(s1 * c2 + c1 * s2)).reshape(l, l)],
                           axis=1).astype(BF16)


def _channel_tables(l, d_f):
    cc = lax.broadcasted_iota(jnp.int32, (d_f, d_f), 0)
    qq = lax.broadcasted_iota(jnp.int32, (d_f, d_f), 1)
    a2 = (((cc % HEAD) * (qq % HEAD)) % HEAD).astype(F32) * (2.0 * math.pi / HEAD)
    same = (cc // HEAD) == (qq // HEAD)
    scale = 1.0 / math.sqrt(l * HEAD)
    return jnp.stack([jnp.where(same, jnp.cos(a2), 0.0), jnp.where(same, jnp.sin(a2), 0.0)]) * scale


def _fft_stage1_kernel(*refs):
    u_refs, (ct_ref, t2_ref, tw_ref, o_ref) = refs[:-4], refs[-4:]
    i = pl.program_id(1)
    l2 = t2_ref.shape[0] // 2
    for j in range(FFT_SUB):
        u = jnp.concatenate([r[pl.ds(i * FFT_SUB + j, l2, stride=FFT_L1), :] for r in u_refs], axis=1)
        z = jnp.concatenate([_dot(u, ct_ref[0]), -_dot(u, ct_ref[1])], axis=0)
        bm = _dot(t2_ref[...], z)
        br, bi = bm[0:l2], bm[l2:]
        tw = tw_ref[j]
        cs, sn = tw[:, 0:1], tw[:, 1:2]
        comp = (br * cs + bi * sn, bi * cs - br * sn)
        for c in range(2):
            for pb in range(l2 // FFT_SUB):
                o_ref[pb, c, j * FFT_SUB:(j + 1) * FFT_SUB, :] = comp[c][pb * FFT_SUB:(pb + 1) * FFT_SUB]


def _fft_stage2_kernel(t1_ref, b_ref, o_ref):
    x = jnp.concatenate([b_ref[0], b_ref[1]], axis=0)
    y = _dot(t1_ref[...], x)
    for p1 in range(o_ref.shape[0]):
        o_ref[p1] = y[p1 * FFT_SUB:(p1 + 1) * FFT_SUB]


def _fft_tables(l):
    l1, l2 = FFT_L1, l // FFT_L1

    def cs(n, m, period):
        p = lax.broadcasted_iota(jnp.int32, (n, m), 0)
        q = lax.broadcasted_iota(jnp.int32, (n, m), 1)
        ang = ((p * q) % period).astype(F32) * (2.0 * math.pi / period)
        return jnp.cos(ang), jnp.sin(ang)

    c2, s2 = cs(l2, l2, l2)
    t2 = jnp.concatenate([jnp.concatenate([c2, s2], axis=1), jnp.concatenate([-s2, c2], axis=1)], axis=0)
    tw = jnp.stack(cs(l1, l2, l), axis=-1)
    c1, s1 = cs(l1, l1, l1)
    t1 = jnp.kron(jnp.concatenate([c1, s1], axis=1), jnp.eye(FFT_SUB, dtype=F32))
    return t2.astype(BF16), tw, t1.astype(BF16)


def _fourier_fft(px, ctab, tabs, ft_blk, d_f):
    b, l, n = px.shape
    l1, l2 = FFT_L1, l // FFT_L1
    npb = l2 // FFT_SUB
    nslab = d_f // LANES
    t2, tw, t1 = tabs
    stage1 = pl.pallas_call(
        _fft_stage1_kernel,
        grid=(b, l1 // FFT_SUB),
        in_specs=[pl.BlockSpec((None, l, LANES), lambda bi, i, k=k: (bi, 0, ft_blk * nslab + k))
                  for k in range(nslab)] + [
                  pl.BlockSpec(ctab.shape, lambda bi, i: (0, 0, 0)),
                  pl.BlockSpec(t2.shape, lambda bi, i: (0, 0)),
                  pl.BlockSpec((FFT_SUB, l2, 2), lambda bi, i: (i, 0, 0))],
        out_specs=pl.BlockSpec((None, npb, 2, FFT_SUB * FFT_SUB, d_f), lambda bi, i: (bi, 0, 0, i, 0)),
        out_shape=jax.ShapeDtypeStruct((b, npb, 2, l1 * FFT_SUB, d_f), F32),
        compiler_params=_cparams(("parallel", "arbitrary")),
        name="fft_stage1",
    )(*([px] * nslab), ctab, t2, tw)
    y = pl.pallas_call(
        _fft_stage2_kernel,
        grid=(b, npb),
        in_specs=[pl.BlockSpec(t1.shape, lambda bi, j: (0, 0)),
                  pl.BlockSpec((None, None, 2, l1 * FFT_SUB, d_f), lambda bi, j: (bi, j, 0, 0, 0))],
        out_specs=pl.BlockSpec((None, l1, None, FFT_SUB, d_f), lambda bi, j: (bi, 0, j, 0, 0)),
        out_shape=jax.ShapeDtypeStruct((b, l1, npb, FFT_SUB, d_f), F32),
        compiler_params=_cparams(("parallel", "parallel")),
        name="fft_stage2",
    )(t1, stage1)
    return y.reshape(b, l, d_f)


def _fourier_dense(px, wtab, ctab, ft_blk, d_f):
    b, l, _ = px.shape
    y = _dft_seq(wtab, _dft_channels(px, ctab, ft_blk, d_f))
    return y.reshape(l, b, d_f).transpose(1, 0, 2)


def _mix_out_kernel(period, yf_ref, yb_ref, z_ref, gate_ref, cg_ref, cx_ref, cb_ref, ft_ref, x_ref,
                    ga_ref, lnw_ref, lnb_ref, cw_ref, w_ref, o_ref, mix_ref):
    d_rwkv = yf_ref.shape[1]
    d_conv = cg_ref.shape[1]
    tm = yf_ref.shape[0]
    r = lax.broadcasted_iota(jnp.int32, (GW, GW), 0)
    c = lax.broadcasted_iota(jnp.int32, (GW, GW), 1)
    ones_bd = jnp.where((r // HEAD) == (c // HEAD), 1.0, 0.0).astype(BF16)
    inv_n = 1.0 / HEAD
    for s in range(d_rwkv // GW):
        sl = slice(s * GW, (s + 1) * GW)
        y = yf_ref[:, sl] + yb_ref[:, sl]
        mu = _dot_lhs_f32(y, ones_bd) * inv_n
        dlt = y - mu
        var = _dot_lhs_f32(dlt * dlt, ones_bd) * inv_n
        yn = dlt * lax.rsqrt(var + GN_EPS) * lnw_ref[:, sl] + lnb_ref[:, sl] + z_ref[:, sl]
        mix_ref[:, sl] = (yn * gate_ref[:, sl]).astype(BF16)
    u = cg_ref[...].astype(F32) * cx_ref[...].astype(F32)
    rowid = lax.broadcasted_iota(jnp.int32, u.shape, 0) % period
    up = jnp.where(rowid == 0, 0.0, pltpu.roll(u, 1, 0))
    dn = jnp.where(rowid == period - 1, 0.0, pltpu.roll(u, tm - 1, 0))
    cw = cw_ref[...]
    conv = cb_ref[...].astype(F32) * (up * cw[0:1] + u * cw[1:2] + dn * cw[2:3])
    mix_ref[:, d_rwkv:d_rwkv + d_conv] = conv.astype(BF16)
    mix_ref[:, d_rwkv + d_conv:] = ft_ref[...].astype(BF16)
    o_ref[...] = x_ref[...] + ga_ref[...] * jnp.dot(mix_ref[...], w_ref[...], preferred_element_type=F32)


def _mix_out(yf, yb, z, gate, px, ft, x, ga, lw, period, conv_blk):
    b, l, d = x.shape
    d_rwkv = yf.shape[2]
    d_conv = lw["conv_w"].shape[1]
    w_out, layer = lw["w_out"]
    d_f = ft.shape[2]
    tm = min(ROWS_MIX_OUT, l)
    assert tm % period == 0
    yspec = pl.BlockSpec((None, tm, d_rwkv), lambda bi, i: (bi, i, 0))
    cspec = lambda off: pl.BlockSpec((None, tm, d_conv), lambda bi, i: (bi, i, conv_blk + off))
    return pl.pallas_call(
        functools.partial(_mix_out_kernel, period),
        grid=(b, l // tm),
        in_specs=[yspec, yspec, yspec, yspec, cspec(0), cspec(1), cspec(2),
                  pl.BlockSpec((None, tm, d_f), lambda bi, i: (bi, i, 0)),
                  pl.BlockSpec((None, tm, d), lambda bi, i: (bi, i, 0)),
                  pl.BlockSpec((None, 1, d), lambda bi, i: (bi, 0, 0)),
                  pl.BlockSpec((1, d_rwkv), lambda bi, i: (0, 0)),
                  pl.BlockSpec((1, d_rwkv), lambda bi, i: (0, 0)),
                  pl.BlockSpec((3, d_conv), lambda bi, i: (0, 0)),
                  pl.BlockSpec((None,) + w_out.shape[1:], lambda bi, i: (layer, 0, 0))],
        out_specs=pl.BlockSpec((None, tm, d), lambda bi, i: (bi, i, 0)),
        out_shape=jax.ShapeDtypeStruct((b, l, d), F32),
        scratch_shapes=[pltpu.VMEM((tm, w_out.shape[1]), BF16)],
        compiler_params=_cparams(("parallel", "parallel")),
        name="mix_out",
    )(yf, yb, z, gate, px, px, px, ft, x, ga, lw["ln_w"], lw["ln_b"], lw["conv_w"], w_out)


def _ffn_kernel(final, x_ref, g_ref, sh_ref, sc_ref, ga_ref, wg_ref, wu_ref, wd_ref, gf_ref, o_ref,
                h_ref, acc_ref):
    j = pl.program_id(2)

    @pl.when(j == 0)
    def _():
        h_ref[...] = _rms_mod(x_ref[...], g_ref[...], sh_ref[...], sc_ref[...]).astype(BF16)
        acc_ref[...] = jnp.zeros_like(acc_ref)

    h = h_ref[...]
    a = jnp.dot(h, wg_ref[...], preferred_element_type=F32)
    u = jnp.dot(h, wu_ref[...], preferred_element_type=F32)
    t = (a * jax.nn.sigmoid(a)) * u
    acc_ref[...] += jnp.dot(t.astype(BF16), wd_ref[...], preferred_element_type=F32)

    @pl.when(j == pl.num_programs(2) - 1)
    def _():
        xn = x_ref[...] + ga_ref[...] * acc_ref[...]
        if final:
            ms = jnp.mean(xn * xn, axis=-1, keepdims=True)
            xn = xn * lax.rsqrt(ms + RMS_EPS) * gf_ref[...]
        o_ref[...] = xn


def _ffn(x, g, shift, scale, ga, wg, wu, wd, g_final, final):
    b, l, d = x.shape
    tf = COL_TILE
    (wg, layer), (wu, _), (wd, _) = wg, wu, wd
    ff = wg.shape[2]
    tm = min(ROWS_FFN, l)
    vec = pl.BlockSpec((None, 1, d), lambda bi, i, j: (bi, 0, 0))
    gspec = pl.BlockSpec((1, d), lambda bi, i, j: (0, 0))
    return pl.pallas_call(
        functools.partial(_ffn_kernel, final),
        grid=(b, l // tm, ff // tf),
        in_specs=[pl.BlockSpec((None, tm, d), lambda bi, i, j: (bi, i, 0)),
                  gspec, vec, vec, vec,
                  pl.BlockSpec((None, d, tf), lambda bi, i, j: (layer, 0, j)),
                  pl.BlockSpec((None, d, tf), lambda bi, i, j: (layer, 0, j)),
                  pl.BlockSpec((None, tf, d), lambda bi, i, j: (layer, j, 0)),
                  gspec],
        out_specs=pl.BlockSpec((None, tm, d), lambda bi, i, j: (bi, i, 0)),
        out_shape=jax.ShapeDtypeStruct((b, l, d), F32),
        scratch_shapes=[pltpu.VMEM((tm, d), BF16), pltpu.VMEM((tm, d), F32)],
        compiler_params=_cparams(("parallel", "parallel", "arbitrary")),
        name="ffn",
    )(x, g, shift, scale, ga, wg, wu, wd, g_final)


def _pad_lora(m, dims, stop):
    d_rwkv, _, _, dl, il, _ = dims
    ad0 = 3 * d_rwkv + dl
    gd0 = ad0 + il
    padl = jnp.zeros(m.shape[:-1] + (LORA_PAD - dl,), m.dtype)
    padi = jnp.zeros(m.shape[:-1] + (LORA_PAD - il,), m.dtype)
    return jnp.concatenate([m[..., :ad0], padl, m[..., ad0:gd0], padi, m[..., gd0:stop]], axis=-1)


def _layer_weights(i, big, rw_shift, dec_w0, dec_up, iclr_a0, iclr_up, k_k, k_a, r_k, ln_w, ln_b, g_up,
                   conv_w, dims):
    d_rwkv, d_conv, d_f, dl, il, gl = dims
    ng = d_rwkv // GW
    rest0 = 3 * d_rwkv + dl + il + gl
    mix = _pad_lora(rw_shift[i], dims, rest0)
    ts = STEP_TILES * COL_TILE
    ident = jnp.broadcast_to(jnp.array([[0.0], [1.0], [0.0]], F32), (3, -mix.shape[1] % ts))
    mix = jnp.concatenate([mix, ident], axis=1)
    pad_rows = lambda m, n: jnp.concatenate([m, jnp.zeros(m.shape[:-2] + (n - m.shape[-2], m.shape[-1]), m.dtype)], -2)
    zero = jnp.zeros((d_rwkv,), F32)
    pvec = jnp.stack([k_k[i], k_a[i], r_k[i].reshape(-1), dec_w0[i, 0], dec_w0[i, 1],
                      iclr_a0[i, 0], iclr_a0[i, 1], zero])
    return dict(
        ng=ng,
        w_in=(big["w_in"], i),
        mix=mix,
        pvec=pvec,
        dec_up=pad_rows(dec_up[i], LORA_PAD).astype(BF16),
        iclr_up=pad_rows(iclr_up[i], LORA_PAD).astype(BF16),
        g_up=g_up[i].astype(BF16),
        ln_w=ln_w[i][None, :], ln_b=ln_b[i][None, :],
        conv_w=conv_w[i],
        w_out=(big["w_out"], i),
        w_gate=(big["w_gate"], i), w_up=(big["w_up"], i), w_down=(big["w_down"], i),
    )


def kernel(x, c, ctx, c_ctx, w_mod, b_mod, norm_mix, w_in, rw_shift, dec_w0, dec_up, iclr_a0, iclr_up,
           k_k, k_a, r_k, ln_w, ln_b, g_up, conv_w, w_out, norm_ffn, w_gate, w_up, w_down, norm_final):
    b, l, d = x.shape
    lc = ctx.shape[1]
    depth = w_mod.shape[0]
    d_rwkv = k_k.shape[1]
    d_conv = conv_w.shape[2]
    dl, il, gl = dec_up.shape[2], iclr_up.shape[2], g_up.shape[1]
    d_f = w_out.shape[1] - d_rwkv - d_conv
    assert dl <= LORA_PAD and il <= LORA_PAD and gl == 2 * LORA_PAD and d_rwkv % GW == 0
    assert l % GRID_W == 0 and l % CHUNK == 0 and lc % CHUNK == 0
    nsub_x = math.gcd(l // CHUNK, WKV_SUB)
    nsub_c = math.gcd(lc // CHUNK, WKV_SUB)
    ng = d_rwkv // GW
    n_rw = 3 * d_rwkv + 4 * LORA_PAD
    conv_blk = n_rw // d_conv
    assert conv_blk * d_conv == n_rw and d_f == COL_TILE

    cvec = jnp.concatenate([c, c_ctx[None, :], jnp.zeros((8 - b - 1, d), F32)], axis=0)
    mods = _adaln(cvec, w_mod, b_mod[:, None, :])

    def mod(i, j, ctx_rows):
        m = mods[i, :, j * d:(j + 1) * d]
        if ctx_rows:
            return jnp.broadcast_to(m[b:b + 1], (b, d))[:, None, :]
        return m[:b][:, None, :]

    fft_x = l % (FFT_L1 * 128) == 0
    wtab_c, ctab_c = _seq_table(lc), _channel_tables(lc, d_f)
    ctab_x = _channel_tables(l, d_f)
    if fft_x:
        ftab_x = _fft_tables(l)
    else:
        wtab_x = _seq_table(l)
    dims = (d_rwkv, d_conv, d_f, dl, il, gl)
    s_zero = jnp.zeros((b, 2, ng * N_PAIR, PW, PW), F32)
    gfin = norm_final[None, :]
    w_in_p = _pad_lora(w_in.astype(BF16), dims, w_in.shape[2])
    assert (w_in_p.shape[2] - COL_TILE) % (STEP_TILES * COL_TILE) == 0
    w_in_p = jnp.concatenate([w_in_p, jnp.zeros(w_in_p.shape[:2] + ((STEP_TILES - 1) * COL_TILE,), BF16)], axis=2)
    big = dict(w_in=w_in_p, w_out=w_out.astype(BF16),
               w_gate=w_gate.astype(BF16), w_up=w_up.astype(BF16), w_down=w_down.astype(BF16))
    xc = ctx
    for i in range(depth):
        lw = _layer_weights(i, big, rw_shift, dec_w0, dec_up, iclr_a0, iclr_up, k_k, k_a, r_k, ln_w, ln_b,
                            g_up, conv_w, dims)
        gmix = norm_mix[i][None, :]
        gffn = norm_ffn[i][None, :]
        last = i == depth - 1
        pc, pcf = _in_proj(xc, gmix, mod(i, 0, True), mod(i, 1, True), lw["w_in"], lw["mix"])
        px, pxf = _in_proj(x, gmix, mod(i, 0, False), mod(i, 1, False), lw["w_in"], lw["mix"])
        yfc, ybc, zc, gc, s_ctx = _wkv(pc, lw, s_zero, nsub_c)
        yfx, ybx, zx, gx, _ = _wkv(px, lw, s_ctx, nsub_x)
        if fft_x:
            ftx = _fourier_fft(pxf, ctab_x.astype(BF16), ftab_x, 0, d_f)
        else:
            ftx = _fourier_dense(pxf, wtab_x, ctab_x, 0, d_f)
        x = _mix_out(yfx, ybx, zx, gx, px, ftx, x, mod(i, 2, False), lw, GRID_W, conv_blk)
        x = _ffn(x, gffn, mod(i, 3, False), mod(i, 4, False), mod(i, 5, False),
                 lw["w_gate"], lw["w_up"], lw["w_down"], gfin, last)
        if not last:
            ftc = _fourier_dense(pcf, wtab_c, ctab_c, 0, d_f)
            xc = _mix_out(yfc, ybc, zc, gc, pc, ftc, xc, mod(i, 2, True), lw, lc, conv_blk)
            xc = _ffn(xc, gffn, mod(i, 3, True), mod(i, 4, True), mod(i, 5, True),
                      lw["w_gate"], lw["w_up"], lw["w_down"], gfin, False)
    return x
```

```python
import functools
import math

import jax
import jax.numpy as jnp
from jax import lax
from jax.experimental import pallas as pl
from jax.experimental.pallas import tpu as pltpu

F32 = jnp.float32
BF16 = jnp.bfloat16

HEAD = 64
GROUP_HEADS = 4
GW = GROUP_HEADS * HEAD
PAIR_HEADS = 2
PW = PAIR_HEADS * HEAD
N_PAIR = GW // PW
CHUNK = 64
WKV_SUB = 16
HALO = 16
LANES = 128
COL_TILE = 512
STEP_TILES = 2
LORA_PAD = 128
GRID_W = 64
FFT_L1 = 64
FFT_SUB = 8
ROWS_IN_PROJ = 1024
ROWS_FFN = 512
ROWS_MIX_OUT = 256
ROWS_DFT = 512
ROWS_DFT_SEQ = 1024
DEPTH_DFT_SEQ = 2048
COLS_ADALN = 1024
RMS_EPS = 1e-6
GN_EPS = 64e-5
KK_EPS = 1e-12
VMEM_LIMIT = 56 * 1024 * 1024


def _cparams(sem):
    return pltpu.CompilerParams(dimension_semantics=sem, vmem_limit_bytes=VMEM_LIMIT)


def _dot(a, b, nt=False):
    dn = (((1,), (1,)), ((), ())) if nt else (((1,), (0,)), ((), ()))
    return lax.dot_general(a.astype(BF16), b.astype(BF16), dn, preferred_element_type=F32)


def _split2(a):
    hi = a.astype(BF16)
    return hi, (a - hi.astype(F32)).astype(BF16)


def _dot_lhs_f32(a, b_exact):
    hi, lo = _split2(a)
    return _dot(hi, b_exact) + _dot(lo, b_exact)


def _dot_rhs_f32(a_exact, b):
    hi, lo = _split2(b)
    return _dot(a_exact, hi) + _dot(a_exact, lo)


def _rms_mod(x, g, shift, scale):
    ms = jnp.mean(x * x, axis=-1, keepdims=True)
    return (x * lax.rsqrt(ms + RMS_EPS) * g) * (1.0 + scale) + shift


def _adaln_kernel(c_ref, w_ref, b_ref, o_ref):
    c = c_ref[...]
    s = c * jax.nn.sigmoid(c)
    o_ref[...] = _dot(s, w_ref[...]) + b_ref[...]


def _adaln(cvec, w_mod, b_mod):
    nl, d, n = w_mod.shape
    tn = COLS_ADALN
    return pl.pallas_call(
        _adaln_kernel,
        grid=(nl, n // tn),
        in_specs=[pl.BlockSpec((8, d), lambda l, j: (0, 0)),
                  pl.BlockSpec((None, d, tn), lambda l, j: (l, 0, j)),
                  pl.BlockSpec((None, 1, tn), lambda l, j: (l, 0, j))],
        out_specs=pl.BlockSpec((None, 8, tn), lambda l, j: (l, 0, j)),
        out_shape=jax.ShapeDtypeStruct((nl, 8, n), F32),
        compiler_params=_cparams(("parallel", "parallel")),
        name="adaln",
    )(cvec, w_mod, b_mod)


def _in_proj_kernel(n_mix_steps, x_ref, xp_ref, xn_ref, g_ref, sh_ref, sc_ref, w_ref, mix_ref, o_ref, of_ref,
                    h_ref):
    i = pl.program_id(1)
    j = pl.program_id(2)
    last = pl.num_programs(2) - 1
    tm = x_ref.shape[0]
    tn = of_ref.shape[1]

    @pl.when(j == 0)
    def _():
        g, sh, sc = g_ref[...], sh_ref[...], sc_ref[...]
        prev = jnp.where(i > 0, _rms_mod(xp_ref[...], g, sh, sc), 0.0)
        nxt = jnp.where(i < pl.num_programs(1) - 1, _rms_mod(xn_ref[...], g, sh, sc), 0.0)
        h_ref[0:HALO, :] = prev.astype(BF16)
        h_ref[HALO:HALO + tm, :] = _rms_mod(x_ref[...], g, sh, sc).astype(BF16)
        h_ref[HALO + tm:, :] = nxt.astype(BF16)

    @pl.when(j < n_mix_steps)
    def _():
        for t in range(STEP_TILES):
            sl = slice(t * tn, (t + 1) * tn)
            p = jnp.dot(h_ref[...], w_ref[:, sl], preferred_element_type=F32)
            mix = mix_ref[:, sl]
            up = pltpu.roll(p, 1, 0)[HALO:HALO + tm]
            dn = pltpu.roll(p, tm + 2 * HALO - 1, 0)[HALO:HALO + tm]
            o_ref[:, sl] = (up * mix[0:1, :] + p[HALO:HALO + tm] * mix[1:2, :] + dn * mix[2:3, :]).astype(o_ref.dtype)

    @pl.when((j >= n_mix_steps) & (j < last))
    def _():
        o_ref[...] = jnp.dot(h_ref[HALO:HALO + tm, :], w_ref[...],
                             preferred_element_type=F32).astype(o_ref.dtype)

    @pl.when(j == last)
    def _():
        of_ref[...] = jnp.dot(h_ref[HALO:HALO + tm, :], w_ref[:, 0:tn], preferred_element_type=F32)


def _in_proj(x, g, shift, scale, w, mix):
    b, l, d = x.shape
    tn = COL_TILE
    ts = STEP_TILES * tn
    w, layer = w
    steps = w.shape[2] // ts
    n_mix_steps = mix.shape[1] // ts
    assert steps * ts == w.shape[2] and n_mix_steps * ts == mix.shape[1] and n_mix_steps < steps
    tm = min(ROWS_IN_PROJ, l)
    hb = tm // HALO
    nhb = l // HALO
    vec = pl.BlockSpec((None, 1, d), lambda bi, i, j: (bi, 0, 0))
    return pl.pallas_call(
        functools.partial(_in_proj_kernel, n_mix_steps),
        grid=(b, l // tm, steps),
        in_specs=[pl.BlockSpec((None, tm, d), lambda bi, i, j: (bi, i, 0)),
                  pl.BlockSpec((None, HALO, d), lambda bi, i, j: (bi, jnp.maximum(i * hb - 1, 0), 0)),
                  pl.BlockSpec((None, HALO, d), lambda bi, i, j: (bi, jnp.minimum((i + 1) * hb, nhb - 1), 0)),
                  pl.BlockSpec((1, d), lambda bi, i, j: (0, 0)),
                  vec, vec,
                  pl.BlockSpec((None, d, ts), lambda bi, i, j: (layer, 0, j)),
                  pl.BlockSpec((3, ts), lambda bi, i, j: (0, jnp.minimum(j, n_mix_steps - 1)))],
        out_specs=[pl.BlockSpec((None, tm, ts), lambda bi, i, j: (bi, i, jnp.minimum(j, steps - 2))),
                   pl.BlockSpec((None, tm, tn), lambda bi, i, j: (bi, i, 0))],
        out_shape=[jax.ShapeDtypeStruct((b, l, (steps - 1) * ts), BF16), jax.ShapeDtypeStruct((b, l, tn), F32)],
        scratch_shapes=[pltpu.VMEM((tm + 2 * HALO, d), BF16)],
        compiler_params=_cparams(("parallel", "parallel", "arbitrary")),
        name="in_proj",
    )(x, x, x, g, shift, scale, w, mix)


def _block_masks(width):
    r = lax.broadcasted_iota(jnp.int32, (width, width), 0)
    c = lax.broadcasted_iota(jnp.int32, (width, width), 1)
    return (r // HEAD) == (c // HEAD), r % HEAD, c % HEAD, r == c


def _rs(x, bd):
    return jnp.where(bd, jnp.concatenate([x] * PAIR_HEADS, axis=0), 0.0)


def _ls(x_rs):
    out = x_rs[0:CHUNK]
    for h in range(1, PAIR_HEADS):
        out = out + x_rs[h * CHUNK:(h + 1) * CHUNK]
    return out


def _each(f, *lists):
    return [f(*a) for a in zip(*lists)]


def _chunk_affine(chunks, masks):
    bd, tt, ss, eye = masks
    row = lax.broadcasted_iota(jnp.int32, (CHUNK, CHUNK), 0)
    col = lax.broadcasted_iota(jnp.int32, (CHUNK, CHUNK), 1)
    tri = {rev: jnp.where((row <= col) if rev else (row >= col), 1.0, 0.0).astype(BF16) for rev in (False, True)}
    strict = {False: bd & (ss < tt), True: bd & (ss > tt)}
    r2 = lax.broadcasted_iota(jnp.int32, (PW, 2 * PW), 0)
    c2 = lax.broadcasted_iota(jnp.int32, (PW, 2 * PW), 1)
    bd2 = (r2 // HEAD) == ((c2 % PW) // HEAD)
    incl2 = {False: bd2 & (c2 % HEAD <= r2 % HEAD), True: bd2 & (c2 % HEAD >= r2 % HEAD)}

    crev = [ch[6] for ch in chunks]
    logw = [ch[5] for ch in chunks]
    cum = _each(lambda rv, lw: _dot_rhs_f32(tri[rv], lw), crev, logw)
    total = _each(lambda rv, cm: cm[0:1] if rv else cm[CHUNK - 1:CHUNK], crev, cum)
    p_inv = _each(lambda cm: jnp.exp(-cm), cum)
    p_end = _each(lambda tot, cm: jnp.exp(tot - cm), total, cum)
    bvec = [ch[3] * ch[4] for ch in chunks]
    wide = dict(
        a=_each(lambda ch, cm, lw: -ch[3] * jnp.exp(cm - lw), chunks, cum, logw),
        r=_each(lambda ch, cm: ch[0] * jnp.exp(cm), chunks, cum),
        b=_each(lambda b, p: b * p, bvec, p_inv),
        k=_each(lambda ch, p: ch[1] * p, chunks, p_inv),
        v=[ch[2] for ch in chunks],
        be=_each(lambda b, p: b * p, bvec, p_end),
        ke=_each(lambda ch, p: ch[1] * p, chunks, p_end),
        pc=_each(jnp.exp, total),
    )

    def pairs(name):
        return [x[:, p * PW:(p + 1) * PW] for x in wide[name] for p in range(N_PAIR)]

    rev = [rv for rv in crev for _ in range(N_PAIR)]
    r_t = pairs("r")
    a_rs = _each(lambda x: _rs(x, bd).astype(BF16), pairs("a"))
    r_rs = _each(lambda x: _rs(x, bd).astype(BF16), r_t)
    b_rs = _each(lambda x: jnp.concatenate([x.astype(BF16)] * PAIR_HEADS, axis=0), pairs("b"))
    k_rs = _each(lambda x: jnp.concatenate([x.astype(BF16)] * PAIR_HEADS, axis=0), pairs("k"))
    v_rs = _each(lambda x: _rs(x, bd).astype(BF16), pairs("v"))
    aa = _each(lambda a, r, b, k: _dot(jnp.concatenate([a, r], axis=0), jnp.concatenate([b, k], axis=0), nt=True),
               a_rs, r_rs, b_rs, k_rs)
    a_ab = _each(lambda rv, x: jnp.where(strict[rv], x[0:PW, 0:PW], 0.0), rev, aa)
    a_ak = _each(lambda rv, x: jnp.where(strict[rv], x[0:PW, PW:], 0.0), rev, aa)
    a_r = _each(lambda rv, x: jnp.where(incl2[rv], x[PW:, :], 0.0).astype(BF16), rev, aa)
    t_inv = _each(lambda a: jnp.where(eye, 1.0, a), a_ab)
    pw = _each(lambda a: _dot(a, a), a_ab)
    for _ in range(int(math.log2(CHUNK)) - 2):
        res = _each(lambda p, t: _dot(p, jnp.concatenate([p, t], axis=1)), pw, t_inv)
        pw = [x[:, 0:PW] for x in res]
        t_inv = _each(lambda t, x: t + x[:, PW:], t_inv, res)
    t_inv = _each(lambda p, t: (t + _dot(p, t)).astype(BF16), pw, t_inv)
    akv = _each(_dot, a_ak, v_rs)
    tw = _each(lambda t, a, x: _dot(t, jnp.concatenate([a, x.astype(BF16)], axis=1)).astype(BF16),
               t_inv, a_rs, akv)
    bk_t = _each(lambda b, k: jnp.concatenate([_rs(b, bd), _rs(k, bd)], axis=0).T.astype(BF16),
                 pairs("be"), pairs("ke"))
    fin = _each(lambda ar, bk, w, v: _dot(jnp.concatenate([ar, bk], axis=0),
                                          jnp.concatenate([w, jnp.concatenate([jnp.zeros_like(v), v], axis=1)],
                                                          axis=0)),
                a_r, bk_t, tw, v_rs)
    rh = _each(lambda r, f: r + _ls(f[0:PW, 0:PW]), r_t, fin)
    y0 = _each(lambda f: _ls(f[0:PW, PW:]), fin)
    m = _each(lambda pc, f: jnp.where(eye, pc, 0.0) + f[PW:, 0:PW], pairs("pc"), fin)
    nn = [f[PW:, PW:] for f in fin]
    out = list(zip(y0, rh, m, nn))
    return [out[i * N_PAIR:(i + 1) * N_PAIR] for i in range(len(chunks))]


def _head_sum(x, ones_bd):
    return _dot_lhs_f32(x, ones_bd)


def _wkv_kernel(nsub, r_f, k_f, v_f, lo_f, r_b, k_b, v_b, lo_b, pvec, dec_up, iclr_up, g_up, s0,
                yf_ref, yb_ref, z_ref, gate_ref, sfin_ref,
                zst):
    c = pl.program_id(2)
    nsteps = pl.num_programs(2)

    @pl.when(c == 0)
    def _():
        zst[...] = s0[...]

    masks = _block_masks(PW)
    ones_bd = jnp.where(_block_masks(GW)[0], 1.0, 0.0).astype(BF16)
    pv = pvec[...]
    k_k, k_a, r_k = pv[0:1], pv[1:2], pv[2:3]
    w0 = (pv[3:4], pv[4:5])
    a0 = (pv[5:6], pv[6:7])
    sig_scale = math.exp(-0.5)

    def streams(r_ref, k_ref, v_ref, lo_ref, d):
        r, k, v, lo = (ref[...].astype(F32) for ref in (r_ref, k_ref, v_ref, lo_ref))
        wd = jnp.tanh(lo[:, 0:LORA_PAD])
        ad = lo[:, LORA_PAD:2 * LORA_PAD]
        gd = lo[:, 2 * LORA_PAD:]
        kk = k * k_k
        kkn = kk * lax.rsqrt(jnp.maximum(_head_sum(kk * kk, ones_bd), KK_EPS * KK_EPS))
        logw = -sig_scale * jax.nn.sigmoid(w0[d] + _dot(wd, dec_up[d]))
        aic = jax.nn.sigmoid(a0[d] + _dot(ad, iclr_up[d]))
        kd = k * (1.0 + (aic - 1.0) * k_a)
        return r, k, v, kkn, logw, aic, kd, ad, gd

    r, k, v, kkn, logw, aic, kd, ad, gd = streams(r_f, k_f, v_f, lo_f, 0)
    aic_o = jax.nn.sigmoid(a0[1] + _dot(ad, iclr_up[1]))
    kd_o = k * (1.0 + (aic_o - 1.0) * k_a)
    z_ref[...] = _head_sum(r * (kd + kd_o) * r_k, ones_bd) * v
    gate_ref[...] = _dot(jax.nn.sigmoid(gd), g_up[...])
    chains = [tuple(t[j * CHUNK:(j + 1) * CHUNK] for t in (r, kd, v, kkn, aic, logw)) + (False,)
              for j in range(nsub)]
    r, k, v, kkn, logw, aic, kd, ad, gd = streams(r_b, k_b, v_b, lo_b, 1)
    chains += [tuple(t[j * CHUNK:(j + 1) * CHUNK] for t in (r, kd, v, kkn, aic, logw)) + (True,)
               for j in range(nsub)]
    affine = _chunk_affine(chains, masks)
    fwd, bwd = affine[:nsub], affine[nsub:]

    for d, (steps, y_ref) in enumerate(((list(range(nsub)), yf_ref), (list(reversed(range(nsub))), yb_ref))):
        affine_d = fwd if d == 0 else bwd
        z = [zst[d, p] for p in range(N_PAIR)]
        for j in steps:
            mz = [_dot(jnp.concatenate([affine_d[j][p][2], affine_d[j][p][1]], axis=0), z[p])
                  for p in range(N_PAIR)]
            y_ref[j * CHUNK:(j + 1) * CHUNK, :] = jnp.concatenate(
                [affine_d[j][p][0] + mz[p][PW:] for p in range(N_PAIR)], axis=1)
            z = [mz[p][0:PW] + affine_d[j][p][3] for p in range(N_PAIR)]
        for p in range(N_PAIR):
            zst[d, p] = z[p]

    @pl.when(c == nsteps - 1)
    def _():
        sfin_ref[...] = zst[...]


def _wkv(px, lw, s0, nsub):
    b, l, _ = px.shape
    ng = lw["ng"]
    d_rwkv = ng * GW
    t = nsub * CHUNK
    nsteps = l // t
    lo_blk = 3 * d_rwkv // (4 * LORA_PAD)
    lo_w = 4 * LORA_PAD

    def main_specs(blk):
        return [pl.BlockSpec((None, t, GW), lambda bi, g, c, s=s: (bi, blk(c), s * ng + g)) for s in range(3)] + [
            pl.BlockSpec((None, t, lo_w), lambda bi, g, c: (bi, blk(c), lo_blk))]

    fblk = lambda c: c
    bblk = lambda c: nsteps - 1 - c
    in_specs = (main_specs(fblk) + main_specs(bblk) + [
        pl.BlockSpec((8, GW), lambda bi, g, c: (0, g)),
        pl.BlockSpec((2, LORA_PAD, GW), lambda bi, g, c: (0, 0, g)),
        pl.BlockSpec((2, LORA_PAD, GW), lambda bi, g, c: (0, 0, g)),
        pl.BlockSpec((2 * LORA_PAD, GW), lambda bi, g, c: (0, g)),
        pl.BlockSpec((None, 2, N_PAIR, PW, PW), lambda bi, g, c: (bi, 0, g, 0, 0)),
    ])
    yspec = lambda blk: pl.BlockSpec((None, t, GW), lambda bi, g, c: (bi, blk(c), g))
    out_specs = [yspec(fblk), yspec(bblk), yspec(fblk), yspec(fblk),
                 pl.BlockSpec((None, 2, N_PAIR, PW, PW), lambda bi, g, c: (bi, 0, g, 0, 0))]
    ysh = jax.ShapeDtypeStruct((b, l, d_rwkv), F32)
    return pl.pallas_call(
        functools.partial(_wkv_kernel, nsub),
        grid=(b, ng, nsteps),
        in_specs=in_specs,
        out_specs=out_specs,
        out_shape=[ysh, ysh, ysh, ysh, jax.ShapeDtypeStruct(s0.shape, F32)],
        scratch_shapes=[pltpu.VMEM((2, N_PAIR, PW, PW), F32)],
        compiler_params=_cparams(("parallel", "parallel", "arbitrary")),
        name="wkv",
    )(*([px] * 8), lw["pvec"], lw["dec_up"], lw["iclr_up"], lw["g_up"], s0)


def _dft_ch_kernel(u_ref, t_ref, o_ref):
    u = u_ref[...]
    hi = u.astype(BF16)
    lo = (u - hi.astype(F32)).astype(BF16)
    tab = t_ref[...]
    o_ref[...] = (_dot(hi, tab) + _dot(lo, tab)).astype(BF16)


def _dft_channels(px, tabs, ft_blk, d_f):
    b, l, _ = px.shape
    tm = min(ROWS_DFT, l)
    nt = l // tm
    return pl.pallas_call(
        _dft_ch_kernel,
        grid=(b, 2, nt),
        in_specs=[pl.BlockSpec((None, tm, d_f), lambda bi, s, i: (bi, i, ft_blk)),
                  pl.BlockSpec((None, d_f, d_f), lambda bi, s, i: (s, 0, 0))],
        out_specs=pl.BlockSpec((tm, d_f), lambda bi, s, i: (s * nt + i, bi)),
        out_shape=jax.ShapeDtypeStruct((2 * l, b * d_f), BF16),
        compiler_params=_cparams(("parallel", "parallel", "parallel")),
        name="dft_channels",
    )(px, tabs)


def _dft_seq_kernel(w_ref, u_ref, o_ref, acc_ref):
    kk = pl.program_id(1)

    @pl.when(kk == 0)
    def _():
        acc_ref[...] = jnp.zeros_like(acc_ref)

    acc_ref[...] += jnp.dot(w_ref[...], u_ref[...], preferred_element_type=F32)

    @pl.when(kk == pl.num_programs(1) - 1)
    def _():
        o_ref[...] = acc_ref[...]


def _dft_seq(wtab, uu):
    l, k2 = wtab.shape
    n = uu.shape[1]
    tm = min(ROWS_DFT_SEQ, l)
    tk = min(DEPTH_DFT_SEQ, k2)
    return pl.pallas_call(
        _dft_seq_kernel,
        grid=(l // tm, k2 // tk),
        in_specs=[pl.BlockSpec((tm, tk), lambda i, kk: (i, kk)),
                  pl.BlockSpec((tk, n), lambda i, kk: (kk, 0))],
        out_specs=pl.BlockSpec((tm, n), lambda i, kk: (i, 0)),
        out_shape=jax.ShapeDtypeStruct((l, n), F32),
        scratch_shapes=[pltpu.VMEM((tm, n), F32)],
        compiler_params=_cparams(("parallel", "arbitrary")),
        name="dft_seq",
    )(wtab, uu)


def _seq_table(l):
    f = math.gcd(l, 128)

    def thin(n, stride):
        p = lax.broadcasted_iota(jnp.int32, (l, n), 0)
        q = lax.broadcasted_iota(jnp.int32, (l, n), 1) * stride
        ang = ((p * q) % l).astype(F32) * (2.0 * math.pi / l)
        return jnp.cos(ang), jnp.sin(ang)

    c1, s1 = (t[:, :, None] for t in thin(l // f, f))
    c2, s2 = (t[:, None, :] for t in thin(f, 1))
    return jnp.concatenate([(c1 * c2 - s1 * s2).reshape(l, l), (-(s1 * c2 + c1 * s2)).reshape(l, l)],
                           axis=1).astype(BF16)


def _channel_tables(l, d_f):
    cc = lax.broadcasted_iota(jnp.int32, (d_f, d_f), 0)
    qq = lax.broadcasted_iota(jnp.int32, (d_f, d_f), 1)
    a2 = (((cc % HEAD) * (qq % HEAD)) % HEAD).astype(F32) * (2.0 * math.pi / HEAD)
    same = (cc // HEAD) == (qq // HEAD)
    scale = 1.0 / math.sqrt(l * HEAD)
    return jnp.stack([jnp.where(same, jnp.cos(a2), 0.0), jnp.where(same, jnp.sin(a2), 0.0)]) * scale


def _fft_stage1_kernel(*refs):
    u_refs, (ct_ref, t2_ref, tw_ref, o_ref) = refs[:-4], refs[-4:]
    i = pl.program_id(1)
    l2 = t2_ref.shape[0] // 2
    for j in range(FFT_SUB):
        u = jnp.concatenate([r[pl.ds(i * FFT_SUB + j, l2, stride=FFT_L1), :] for r in u_refs], axis=1)
        z = jnp.concatenate([_dot(u, ct_ref[0]), -_dot(u, ct_ref[1])], axis=0)
        bm = _dot(t2_ref[...], z)
        br, bi = bm[0:l2], bm[l2:]
        tw = tw_ref[j]
        cs, sn = tw[:, 0:1], tw[:, 1:2]
        comp = (br * cs + bi * sn, bi * cs - br * sn)
        for c in range(2):
            for pb in range(l2 // FFT_SUB):
                o_ref[pb, c, j * FFT_SUB:(j + 1) * FFT_SUB, :] = comp[c][pb * FFT_SUB:(pb + 1) * FFT_SUB]


def _fft_stage2_kernel(t1_ref, b_ref, o_ref):
    x = jnp.concatenate([b_ref[0], b_ref[1]], axis=0)
    y = _dot(t1_ref[...], x)
    for p1 in range(o_ref.shape[0]):
        o_ref[p1] = y[p1 * FFT_SUB:(p1 + 1) * FFT_SUB]


def _fft_tables(l):
    l1, l2 = FFT_L1, l // FFT_L1

    def cs(n, m, period):
        p = lax.broadcasted_iota(jnp.int32, (n, m), 0)
        q = lax.broadcasted_iota(jnp.int32, (n, m), 1)
        ang = ((p * q) % period).astype(F32) * (2.0 * math.pi / period)
        return jnp.cos(ang), jnp.sin(ang)

    c2, s2 = cs(l2, l2, l2)
    t2 = jnp.concatenate([jnp.concatenate([c2, s2], axis=1), jnp.concatenate([-s2, c2], axis=1)], axis=0)
    tw = jnp.stack(cs(l1, l2, l), axis=-1)
    c1, s1 = cs(l1, l1, l1)
    t1 = jnp.kron(jnp.concatenate([c1, s1], axis=1), jnp.eye(FFT_SUB, dtype=F32))
    return t2.astype(BF16), tw, t1.astype(BF16)


def _fourier_fft(px, ctab, tabs, ft_blk, d_f):
    b, l, n = px.shape
    l1, l2 = FFT_L1, l // FFT_L1
    npb = l2 // FFT_SUB
    nslab = d_f // LANES
    t2, tw, t1 = tabs
    stage1 = pl.pallas_call(
        _fft_stage1_kernel,
        grid=(b, l1 // FFT_SUB),
        in_specs=[pl.BlockSpec((None, l, LANES), lambda bi, i, k=k: (bi, 0, ft_blk * nslab + k))
                  for k in range(nslab)] + [
                  pl.BlockSpec(ctab.shape, lambda bi, i: (0, 0, 0)),
                  pl.BlockSpec(t2.shape, lambda bi, i: (0, 0)),
                  pl.BlockSpec((FFT_SUB, l2, 2), lambda bi, i: (i, 0, 0))],
        out_specs=pl.BlockSpec((None, npb, 2, FFT_SUB * FFT_SUB, d_f), lambda bi, i: (bi, 0, 0, i, 0)),
        out_shape=jax.ShapeDtypeStruct((b, npb, 2, l1 * FFT_SUB, d_f), F32),
        compiler_params=_cparams(("parallel", "arbitrary")),
        name="fft_stage1",
    )(*([px] * nslab), ctab, t2, tw)
    y = pl.pallas_call(
        _fft_stage2_kernel,
        grid=(b, npb),
        in_specs=[pl.BlockSpec(t1.shape, lambda bi, j: (0, 0)),
                  pl.BlockSpec((None, None, 2, l1 * FFT_SUB, d_f), lambda bi, j: (bi, j, 0, 0, 0))],
        out_specs=pl.BlockSpec((None, l1, None, FFT_SUB, d_f), lambda bi, j: (bi, 0, j, 0, 0)),
        out_shape=jax.ShapeDtypeStruct((b, l1, npb, FFT_SUB, d_f), F32),
        compiler_params=_cparams(("parallel", "parallel")),
        name="fft_stage2",
    )(t1, stage1)
    return y.reshape(b, l, d_f)


def _fourier_dense(px, wtab, ctab, ft_blk, d_f):
    b, l, _ = px.shape
    y = _dft_seq(wtab, _dft_channels(px, ctab, ft_blk, d_f))
    return y.reshape(l, b, d_f).transpose(1, 0, 2)


def _mix_out_kernel(period, yf_ref, yb_ref, z_ref, gate_ref, cg_ref, cx_ref, cb_ref, ft_ref, x_ref,
                    ga_ref, lnw_ref, lnb_ref, cw_ref, w_ref, o_ref, mix_ref):
    d_rwkv = yf_ref.shape[1]
    d_conv = cg_ref.shape[1]
    tm = yf_ref.shape[0]
    r = lax.broadcasted_iota(jnp.int32, (GW, GW), 0)
    c = lax.broadcasted_iota(jnp.int32, (GW, GW), 1)
    ones_bd = jnp.where((r // HEAD) == (c // HEAD), 1.0, 0.0).astype(BF16)
    inv_n = 1.0 / HEAD
    for s in range(d_rwkv // GW):
        sl = slice(s * GW, (s + 1) * GW)
        y = yf_ref[:, sl] + yb_ref[:, sl]
        mu = _dot_lhs_f32(y, ones_bd) * inv_n
        dlt = y - mu
        var = _dot_lhs_f32(dlt * dlt, ones_bd) * inv_n
        yn = dlt * lax.rsqrt(var + GN_EPS) * lnw_ref[:, sl] + lnb_ref[:, sl] + z_ref[:, sl]
        mix_ref[:, sl] = (yn * gate_ref[:, sl]).astype(BF16)
    u = cg_ref[...].astype(F32) * cx_ref[...].astype(F32)
    rowid = lax.broadcasted_iota(jnp.int32, u.shape, 0) % period
    up = jnp.where(rowid == 0, 0.0, pltpu.roll(u, 1, 0))
    dn = jnp.where(rowid == period - 1, 0.0, pltpu.roll(u, tm - 1, 0))
    cw = cw_ref[...]
    conv = cb_ref[...].astype(F32) * (up * cw[0:1] + u * cw[1:2] + dn * cw[2:3])
    mix_ref[:, d_rwkv:d_rwkv + d_conv] = conv.astype(BF16)
    mix_ref[:, d_rwkv + d_conv:] = ft_ref[...].astype(BF16)
    o_ref[...] = x_ref[...] + ga_ref[...] * jnp.dot(mix_ref[...], w_ref[...], preferred_element_type=F32)


def _mix_out(yf, yb, z, gate, px, ft, x, ga, lw, period, conv_blk):
    b, l, d = x.shape
    d_rwkv = yf.shape[2]
    d_conv = lw["conv_w"].shape[1]
    w_out, layer = lw["w_out"]
    d_f = ft.shape[2]
    tm = min(ROWS_MIX_OUT, l)
    assert tm % period == 0
    yspec = pl.BlockSpec((None, tm, d_rwkv), lambda bi, i: (bi, i, 0))
    cspec = lambda off: pl.BlockSpec((None, tm, d_conv), lambda bi, i: (bi, i, conv_blk + off))
    return pl.pallas_call(
        functools.partial(_mix_out_kernel, period),
        grid=(b, l // tm),
        in_specs=[yspec, yspec, yspec, yspec, cspec(0), cspec(1), cspec(2),
                  pl.BlockSpec((None, tm, d_f), lambda bi, i: (bi, i, 0)),
                  pl.BlockSpec((None, tm, d), lambda bi, i: (bi, i, 0)),
                  pl.BlockSpec((None, 1, d), lambda bi, i: (bi, 0, 0)),
                  pl.BlockSpec((1, d_rwkv), lambda bi, i: (0, 0)),
                  pl.BlockSpec((1, d_rwkv), lambda bi, i: (0, 0)),
                  pl.BlockSpec((3, d_conv), lambda bi, i: (0, 0)),
                  pl.BlockSpec((None,) + w_out.shape[1:], lambda bi, i: (layer, 0, 0))],
        out_specs=pl.BlockSpec((None, tm, d), lambda bi, i: (bi, i, 0)),
        out_shape=jax.ShapeDtypeStruct((b, l, d), F32),
        scratch_shapes=[pltpu.VMEM((tm, w_out.shape[1]), BF16)],
        compiler_params=_cparams(("parallel", "parallel")),
        name="mix_out",
    )(yf, yb, z, gate, px, px, px, ft, x, ga, lw["ln_w"], lw["ln_b"], lw["conv_w"], w_out)


def _ffn_kernel(final, x_ref, g_ref, sh_ref, sc_ref, ga_ref, wg_ref, wu_ref, wd_ref, gf_ref, o_ref,
                h_ref, acc_ref):
    j = pl.program_id(2)

    @pl.when(j == 0)
    def _():
        h_ref[...] = _rms_mod(x_ref[...], g_ref[...], sh_ref[...], sc_ref[...]).astype(BF16)
        acc_ref[...] = jnp.zeros_like(acc_ref)

    h = h_ref[...]
    a = jnp.dot(h, wg_ref[...], preferred_element_type=F32)
    u = jnp.dot(h, wu_ref[...], preferred_element_type=F32)
    t = (a * jax.nn.sigmoid(a)) * u
    acc_ref[...] += jnp.dot(t.astype(BF16), wd_ref[...], preferred_element_type=F32)

    @pl.when(j == pl.num_programs(2) - 1)
    def _():
        xn = x_ref[...] + ga_ref[...] * acc_ref[...]
        if final:
            ms = jnp.mean(xn * xn, axis=-1, keepdims=True)
            xn = xn * lax.rsqrt(ms + RMS_EPS) * gf_ref[...]
        o_ref[...] = xn


def _ffn(x, g, shift, scale, ga, wg, wu, wd, g_final, final):
    b, l, d = x.shape
    tf = COL_TILE
    (wg, layer), (wu, _), (wd, _) = wg, wu, wd
    ff = wg.shape[2]
    tm = min(ROWS_FFN, l)
    vec = pl.BlockSpec((None, 1, d), lambda bi, i, j: (bi, 0, 0))
    gspec = pl.BlockSpec((1, d), lambda bi, i, j: (0, 0))
    return pl.pallas_call(
        functools.partial(_ffn_kernel, final),
        grid=(b, l // tm, ff // tf),
        in_specs=[pl.BlockSpec((None, tm, d), lambda bi, i, j: (bi, i, 0)),
                  gspec, vec, vec, vec,
                  pl.BlockSpec((None, d, tf), lambda bi, i, j: (layer, 0, j)),
                  pl.BlockSpec((None, d, tf), lambda bi, i, j: (layer, 0, j)),
                  pl.BlockSpec((None, tf, d), lambda bi, i, j: (layer, j, 0)),
                  gspec],
        out_specs=pl.BlockSpec((None, tm, d), lambda bi, i, j: (bi, i, 0)),
        out_shape=jax.ShapeDtypeStruct((b, l, d), F32),
        scratch_shapes=[pltpu.VMEM((tm, d), BF16), pltpu.VMEM((tm, d), F32)],
        compiler_params=_cparams(("parallel", "parallel", "arbitrary")),
        name="ffn",
    )(x, g, shift, scale, ga, wg, wu, wd, g_final)


def _pad_lora(m, dims, stop):
    d_rwkv, _, _, dl, il, _ = dims
    ad0 = 3 * d_rwkv + dl
    gd0 = ad0 + il
    padl = jnp.zeros(m.shape[:-1] + (LORA_PAD - dl,), m.dtype)
    padi = jnp.zeros(m.shape[:-1] + (LORA_PAD - il,), m.dtype)
    return jnp.concatenate([m[..., :ad0], padl, m[..., ad0:gd0], padi, m[..., gd0:stop]], axis=-1)


def _layer_weights(i, big, rw_shift, dec_w0, dec_up, iclr_a0, iclr_up, k_k, k_a, r_k, ln_w, ln_b, g_up,
                   conv_w, dims):
    d_rwkv, d_conv, d_f, dl, il, gl = dims
    ng = d_rwkv // GW
    rest0 = 3 * d_rwkv + dl + il + gl
    mix = _pad_lora(rw_shift[i], dims, rest0)
    ts = STEP_TILES * COL_TILE
    ident = jnp.broadcast_to(jnp.array([[0.0], [1.0], [0.0]], F32), (3, -mix.shape[1] % ts))
    mix = jnp.concatenate([mix, ident], axis=1)
    pad_rows = lambda m, n: jnp.concatenate([m, jnp.zeros(m.shape[:-2] + (n - m.shape[-2], m.shape[-1]), m.dtype)], -2)
    zero = jnp.zeros((d_rwkv,), F32)
    pvec = jnp.stack([k_k[i], k_a[i], r_k[i].reshape(-1), dec_w0[i, 0], dec_w0[i, 1],
                      iclr_a0[i, 0], iclr_a0[i, 1], zero])
    return dict(
        ng=ng,
        w_in=(big["w_in"], i),
        mix=mix,
        pvec=pvec,
        dec_up=pad_rows(dec_up[i], LORA_PAD).astype(BF16),
        iclr_up=pad_rows(iclr_up[i], LORA_PAD).astype(BF16),
        g_up=g_up[i].astype(BF16),
        ln_w=ln_w[i][None, :], ln_b=ln_b[i][None, :],
        conv_w=conv_w[i],
        w_out=(big["w_out"], i),
        w_gate=(big["w_gate"], i), w_up=(big["w_up"], i), w_down=(big["w_down"], i),
    )


def kernel(x, c, ctx, c_ctx, w_mod, b_mod, norm_mix, w_in, rw_shift, dec_w0, dec_up, iclr_a0, iclr_up,
           k_k, k_a, r_k, ln_w, ln_b, g_up, conv_w, w_out, norm_ffn, w_gate, w_up, w_down, norm_final):
    b, l, d = x.shape
    lc = ctx.shape[1]
    depth = w_mod.shape[0]
    d_rwkv = k_k.shape[1]
    d_conv = conv_w.shape[2]
    dl, il, gl = dec_up.shape[2], iclr_up.shape[2], g_up.shape[1]
    d_f = w_out.shape[1] - d_rwkv - d_conv
    assert dl <= LORA_PAD and il <= LORA_PAD and gl == 2 * LORA_PAD and d_rwkv % GW == 0
    assert l % GRID_W == 0 and l % CHUNK == 0 and lc % CHUNK == 0
    nsub_x = math.gcd(l // CHUNK, WKV_SUB)
    nsub_c = math.gcd(lc // CHUNK, WKV_SUB)
    ng = d_rwkv // GW
    n_rw = 3 * d_rwkv + 4 * LORA_PAD
    conv_blk = n_rw // d_conv
    assert conv_blk * d_conv == n_rw and d_f == COL_TILE

    cvec = jnp.concatenate([c, c_ctx[None, :], jnp.zeros((8 - b - 1, d), F32)], axis=0)
    mods = _adaln(cvec, w_mod, b_mod[:, None, :])

    def mod(i, j, ctx_rows):
        m = mods[i, :, j * d:(j + 1) * d]
        if ctx_rows:
            return jnp.broadcast_to(m[b:b + 1], (b, d))[:, None, :]
        return m[:b][:, None, :]

    fft_x = l % (FFT_L1 * 128) == 0
    wtab_c, ctab_c = _seq_table(lc), _channel_tables(lc, d_f)
    ctab_x = _channel_tables(l, d_f)
    if fft_x:
        ftab_x = _fft_tables(l)
    else:
        wtab_x = _seq_table(l)
    dims = (d_rwkv, d_conv, d_f, dl, il, gl)
    s_zero = jnp.zeros((b, 2, ng * N_PAIR, PW, PW), F32)
    gfin = norm_final[None, :]
    w_in_p = _pad_lora(w_in.astype(BF16), dims, w_in.shape[2])
    assert (w_in_p.shape[2] - COL_TILE) % (STEP_TILES * COL_TILE) == 0
    w_in_p = jnp.concatenate([w_in_p, jnp.zeros(w_in_p.shape[:2] + ((STEP_TILES - 1) * COL_TILE,), BF16)], axis=2)
    big = dict(w_in=w_in_p, w_out=w_out.astype(BF16),
               w_gate=w_gate.astype(BF16), w_up=w_up.astype(BF16), w_down=w_down.astype(BF16))
    xc = ctx
    for i in range(depth):
        lw = _layer_weights(i, big, rw_shift, dec_w0, dec_up, iclr_a0, iclr_up, k_k, k_a, r_k, ln_w, ln_b,
                            g_up, conv_w, dims)
        gmix = norm_mix[i][None, :]
        gffn = norm_ffn[i][None, :]
        last = i == depth - 1
        pc, pcf = _in_proj(xc, gmix, mod(i, 0, True), mod(i, 1, True), lw["w_in"], lw["mix"])
        px, pxf = _in_proj(x, gmix, mod(i, 0, False), mod(i, 1, False), lw["w_in"], lw["mix"])
        yfc, ybc, zc, gc, s_ctx = _wkv(pc, lw, s_zero, nsub_c)
        yfx, ybx, zx, gx, _ = _wkv(px, lw, s_ctx, nsub_x)
        if fft_x:
            ftx = _fourier_fft(pxf, ctab_x.astype(BF16), ftab_x, 0, d_f)
        else:
            ftx = _fourier_dense(pxf, wtab_x, ctab_x, 0, d_f)
        x = _mix_out(yfx, ybx, zx, gx, px, ftx, x, mod(i, 2, False), lw, GRID_W, conv_blk)
        x = _ffn(x, gffn, mod(i, 3, False), mod(i, 4, False), mod(i, 5, False),
                 lw["w_gate"], lw["w_up"], lw["w_down"], gfin, last)
        if not last:
            ftc = _fourier_dense(pcf, wtab_c, ctab_c, 0, d_f)
            xc = _mix_out(yfc, ybc, zc, gc, pc, ftc, xc, mod(i, 2, True), lw, lc, conv_blk)
            xc = _ffn(xc, gffn, mod(i, 3, True), mod(i, 4, True), mod(i, 5, True),
                      lw["w_gate"], lw["w_up"], lw["w_down"], gfin, False)
    return x
```

```python
import functools
import math

import jax
import jax.numpy as jnp
from jax import lax
from jax.experimental import pallas as pl
from jax.experimental.pallas import tpu as pltpu

F32 = jnp.float32
BF16 = jnp.bfloat16

HEAD = 64
GROUP_HEADS = 4
GW = GROUP_HEADS * HEAD
PAIR_HEADS = 2
PW = PAIR_HEADS * HEAD
N_PAIR = GW // PW
CHUNK = 64
WKV_SUB = 16
HALO = 16
LANES = 128
COL_TILE = 512
STEP_TILES = 2
LORA_PAD = 128
GRID_W = 64
FFT_L1 = 64
FFT_SUB = 8
ROWS_IN_PROJ = 1024
ROWS_FFN = 512
ROWS_MIX_OUT = 256
ROWS_DFT = 512
ROWS_DFT_SEQ = 1024
DEPTH_DFT_SEQ = 2048
COLS_ADALN = 1024
RMS_EPS = 1e-6
GN_EPS = 64e-5
KK_EPS = 1e-12
VMEM_LIMIT = 56 * 1024 * 1024


def _cparams(sem):
    return pltpu.CompilerParams(dimension_semantics=sem, vmem_limit_bytes=VMEM_LIMIT)


def _dot(a, b, nt=False):
    dn = (((1,), (1,)), ((), ())) if nt else (((1,), (0,)), ((), ()))
    return lax.dot_general(a.astype(BF16), b.astype(BF16), dn, preferred_element_type=F32)


def _split2(a):
    hi = a.astype(BF16)
    return hi, (a - hi.astype(F32)).astype(BF16)


def _dot_lhs_f32(a, b_exact):
    hi, lo = _split2(a)
    return _dot(hi, b_exact) + _dot(lo, b_exact)


def _dot_rhs_f32(a_exact, b):
    hi, lo = _split2(b)
    return _dot(a_exact, hi) + _dot(a_exact, lo)


def _rms_mod(x, g, shift, scale):
    ms = jnp.mean(x * x, axis=-1, keepdims=True)
    return (x * lax.rsqrt(ms + RMS_EPS) * g) * (1.0 + scale) + shift


def _adaln_kernel(c_ref, w_ref, b_ref, o_ref):
    c = c_ref[...]
    s = c * jax.nn.sigmoid(c)
    o_ref[...] = _dot(s, w_ref[...]) + b_ref[...]


def _adaln(cvec, w_mod, b_mod):
    nl, d, n = w_mod.shape
    tn = COLS_ADALN
    return pl.pallas_call(
        _adaln_kernel,
        grid=(nl, n // tn),
        in_specs=[pl.BlockSpec((8, d), lambda l, j: (0, 0)),
                  pl.BlockSpec((None, d, tn), lambda l, j: (l, 0, j)),
                  pl.BlockSpec((None, 1, tn), lambda l, j: (l, 0, j))],
        out_specs=pl.BlockSpec((None, 8, tn), lambda l, j: (l, 0, j)),
        out_shape=jax.ShapeDtypeStruct((nl, 8, n), F32),
        compiler_params=_cparams(("parallel", "parallel")),
        name="adaln",
    )(cvec, w_mod, b_mod)


def _in_proj_kernel(n_mix_steps, x_ref, xp_ref, xn_ref, g_ref, sh_ref, sc_ref, w_ref, mix_ref, o_ref, of_ref,
                    h_ref):
    i = pl.program_id(1)
    j = pl.program_id(2)
    last = pl.num_programs(2) - 1
    tm = x_ref.shape[0]
    tn = of_ref.shape[1]

    @pl.when(j == 0)
    def _():
        g, sh, sc = g_ref[...], sh_ref[...], sc_ref[...]
        prev = jnp.where(i > 0, _rms_mod(xp_ref[...], g, sh, sc), 0.0)
        nxt = jnp.where(i < pl.num_programs(1) - 1, _rms_mod(xn_ref[...], g, sh, sc), 0.0)
        h_ref[0:HALO, :] = prev.astype(BF16)
        h_ref[HALO:HALO + tm, :] = _rms_mod(x_ref[...], g, sh, sc).astype(BF16)
        h_ref[HALO + tm:, :] = nxt.astype(BF16)

    @pl.when(j < n_mix_steps)
    def _():
        for t in range(STEP_TILES):
            sl = slice(t * tn, (t + 1) * tn)
            p = jnp.dot(h_ref[...], w_ref[:, sl], preferred_element_type=F32)
            mix = mix_ref[:, sl]
            up = pltpu.roll(p, 1, 0)[HALO:HALO + tm]
            dn = pltpu.roll(p, tm + 2 * HALO - 1, 0)[HALO:HALO + tm]
            o_ref[:, sl] = (up * mix[0:1, :] + p[HALO:HALO + tm] * mix[1:2, :] + dn * mix[2:3, :]).astype(o_ref.dtype)

    @pl.when((j >= n_mix_steps) & (j < last))
    def _():
        o_ref[...] = jnp.dot(h_ref[HALO:HALO + tm, :], w_ref[...],
                             preferred_element_type=F32).astype(o_ref.dtype)

    @pl.when(j == last)
    def _():
        of_ref[...] = jnp.dot(h_ref[HALO:HALO + tm, :], w_ref[:, 0:tn], preferred_element_type=F32)


def _in_proj(x, g, shift, scale, w, mix):
    b, l, d = x.shape
    tn = COL_TILE
    ts = STEP_TILES * tn
    w, layer = w
    steps = w.shape[2] // ts
    n_mix_steps = mix.shape[1] // ts
    assert steps * ts == w.shape[2] and n_mix_steps * ts == mix.shape[1] and n_mix_steps < steps
    tm = min(ROWS_IN_PROJ, l)
    hb = tm // HALO
    nhb = l // HALO
    vec = pl.BlockSpec((None, 1, d), lambda bi, i, j: (bi, 0, 0))
    return pl.pallas_call(
        functools.partial(_in_proj_kernel, n_mix_steps),
        grid=(b, l // tm, steps),
        in_specs=[pl.BlockSpec((None, tm, d), lambda bi, i, j: (bi, i, 0)),
                  pl.BlockSpec((None, HALO, d), lambda bi, i, j: (bi, jnp.maximum(i * hb - 1, 0), 0)),
                  pl.BlockSpec((None, HALO, d), lambda bi, i, j: (bi, jnp.minimum((i + 1) * hb, nhb - 1), 0)),
                  pl.BlockSpec((1, d), lambda bi, i, j: (0, 0)),
                  vec, vec,
                  pl.BlockSpec((None, d, ts), lambda bi, i, j: (layer, 0, j)),
                  pl.BlockSpec((3, ts), lambda bi, i, j: (0, jnp.minimum(j, n_mix_steps - 1)))],
        out_specs=[pl.BlockSpec((None, tm, ts), lambda bi, i, j: (bi, i, jnp.minimum(j, steps - 2))),
                   pl.BlockSpec((None, tm, tn), lambda bi, i, j: (bi, i, 0))],
        out_shape=[jax.ShapeDtypeStruct((b, l, (steps - 1) * ts), BF16), jax.ShapeDtypeStruct((b, l, tn), F32)],
        scratch_shapes=[pltpu.VMEM((tm + 2 * HALO, d), BF16)],
        compiler_params=_cparams(("parallel", "parallel", "arbitrary")),
        name="in_proj",
    )(x, x, x, g, shift, scale, w, mix)


def _block_masks(width):
    r = lax.broadcasted_iota(jnp.int32, (width, width), 0)
    c = lax.broadcasted_iota(jnp.int32, (width, width), 1)
    return (r // HEAD) == (c // HEAD), r % HEAD, c % HEAD, r == c


def _rs(x, bd):
    return jnp.where(bd, jnp.concatenate([x] * PAIR_HEADS, axis=0), 0.0)


def _ls(x_rs):
    out = x_rs[0:CHUNK]
    for h in range(1, PAIR_HEADS):
        out = out + x_rs[h * CHUNK:(h + 1) * CHUNK]
    return out


def _each(f, *lists):
    return [f(*a) for a in zip(*lists)]


def _chunk_affine(chunks, masks):
    bd, tt, ss, eye = masks
    row = lax.broadcasted_iota(jnp.int32, (CHUNK, CHUNK), 0)
    col = lax.broadcasted_iota(jnp.int32, (CHUNK, CHUNK), 1)
    tri = {rev: jnp.where((row <= col) if rev else (row >= col), 1.0, 0.0).astype(BF16) for rev in (False, True)}
    strict = {False: bd & (ss < tt), True: bd & (ss > tt)}
    r2 = lax.broadcasted_iota(jnp.int32, (PW, 2 * PW), 0)
    c2 = lax.broadcasted_iota(jnp.int32, (PW, 2 * PW), 1)
    bd2 = (r2 // HEAD) == ((c2 % PW) // HEAD)
    incl2 = {False: bd2 & (c2 % HEAD <= r2 % HEAD), True: bd2 & (c2 % HEAD >= r2 % HEAD)}

    crev = [ch[6] for ch in chunks]
    logw = [ch[5] for ch in chunks]
    cum = _each(lambda rv, lw: _dot_rhs_f32(tri[rv], lw), crev, logw)
    total = _each(lambda rv, cm: cm[0:1] if rv else cm[CHUNK - 1:CHUNK], crev, cum)
    p_inv = _each(lambda cm: jnp.exp(-cm), cum)
    p_end = _each(lambda tot, cm: jnp.exp(tot - cm), total, cum)
    bvec = [ch[3] * ch[4] for ch in chunks]
    wide = dict(
        a=_each(lambda ch, cm, lw: -ch[3] * jnp.exp(cm - lw), chunks, cum, logw),
        r=_each(lambda ch, cm: ch[0] * jnp.exp(cm), chunks, cum),
        b=_each(lambda b, p: b * p, bvec, p_inv),
        k=_each(lambda ch, p: ch[1] * p, chunks, p_inv),
        v=[ch[2] for ch in chunks],
        be=_each(lambda b, p: b * p, bvec, p_end),
        ke=_each(lambda ch, p: ch[1] * p, chunks, p_end),
        pc=_each(jnp.exp, total),
    )

    def pairs(name):
        return [x[:, p * PW:(p + 1) * PW] for x in wide[name] for p in range(N_PAIR)]

    rev = [rv for rv in crev for _ in range(N_PAIR)]
    r_t = pairs("r")
    a_rs = _each(lambda x: _rs(x, bd).astype(BF16), pairs("a"))
    r_rs = _each(lambda x: _rs(x, bd).astype(BF16), r_t)
    b_rs = _each(lambda x: jnp.concatenate([x.astype(BF16)] * PAIR_HEADS, axis=0), pairs("b"))
    k_rs = _each(lambda x: jnp.concatenate([x.astype(BF16)] * PAIR_HEADS, axis=0), pairs("k"))
    v_rs = _each(lambda x: _rs(x, bd).astype(BF16), pairs("v"))
    aa = _each(lambda a, r, b, k: _dot(jnp.concatenate([a, r], axis=0), jnp.concatenate([b, k], axis=0), nt=True),
               a_rs, r_rs, b_rs, k_rs)
    a_ab = _each(lambda rv, x: jnp.where(strict[rv], x[0:PW, 0:PW], 0.0), rev, aa)
    a_ak = _each(lambda rv, x: jnp.where(strict[rv], x[0:PW, PW:], 0.0), rev, aa)
    a_r = _each(lambda rv, x: jnp.where(incl2[rv], x[PW:, :], 0.0).astype(BF16), rev, aa)
    t_inv = _each(lambda a: jnp.where(eye, 1.0, a), a_ab)
    pw = _each(lambda a: _dot(a, a), a_ab)
    for _ in range(int(math.log2(CHUNK)) - 2):
        res = _each(lambda p, t: _dot(p, jnp.concatenate([p, t], axis=1)), pw, t_inv)
        pw = [x[:, 0:PW] for x in res]
        t_inv = _each(lambda t, x: t + x[:, PW:], t_inv, res)
    t_inv = _each(lambda p, t: (t + _dot(p, t)).astype(BF16), pw, t_inv)
    akv = _each(_dot, a_ak, v_rs)
    tw = _each(lambda t, a, x: _dot(t, jnp.concatenate([a, x.astype(BF16)], axis=1)).astype(BF16),
               t_inv, a_rs, akv)
    bk_t = _each(lambda b, k: jnp.concatenate([_rs(b, bd), _rs(k, bd)], axis=0).T.astype(BF16),
                 pairs("be"), pairs("ke"))
    fin = _each(lambda ar, bk, w, v: _dot(jnp.concatenate([ar, bk], axis=0),
                                          jnp.concatenate([w, jnp.concatenate([jnp.zeros_like(v), v], axis=1)],
                                                          axis=0)),
                a_r, bk_t, tw, v_rs)
    rh = _each(lambda r, f: r + _ls(f[0:PW, 0:PW]), r_t, fin)
    y0 = _each(lambda f: _ls(f[0:PW, PW:]), fin)
    m = _each(lambda pc, f: jnp.where(eye, pc, 0.0) + f[PW:, 0:PW], pairs("pc"), fin)
    nn = [f[PW:, PW:] for f in fin]
    out = list(zip(y0, rh, m, nn))
    return [out[i * N_PAIR:(i + 1) * N_PAIR] for i in range(len(chunks))]


def _head_sum(x, ones_bd):
    return _dot_lhs_f32(x, ones_bd)


def _wkv_kernel(nsub, r_f, k_f, v_f, lo_f, r_b, k_b, v_b, lo_b, pvec, dec_up, iclr_up, g_up, s0,
                yf_ref, yb_ref, z_ref, gate_ref, sfin_ref,
                zst):
    c = pl.program_id(2)
    nsteps = pl.num_programs(2)

    @pl.when(c == 0)
    def _():
        zst[...] = s0[...]

    masks = _block_masks(PW)
    ones_bd = jnp.where(_block_masks(GW)[0], 1.0, 0.0).astype(BF16)
    pv = pvec[...]
    k_k, k_a, r_k = pv[0:1], pv[1:2], pv[2:3]
    w0 = (pv[3:4], pv[4:5])
    a0 = (pv[5:6], pv[6:7])
    sig_scale = math.exp(-0.5)

    def streams(r_ref, k_ref, v_ref, lo_ref, d):
        r, k, v, lo = (ref[...].astype(F32) for ref in (r_ref, k_ref, v_ref, lo_ref))
        wd = jnp.tanh(lo[:, 0:LORA_PAD])
        ad = lo[:, LORA_PAD:2 * LORA_PAD]
        gd = lo[:, 2 * LORA_PAD:]
        kk = k * k_k
        kkn = kk * lax.rsqrt(jnp.maximum(_head_sum(kk * kk, ones_bd), KK_EPS * KK_EPS))
        logw = -sig_scale * jax.nn.sigmoid(w0[d] + _dot(wd, dec_up[d]))
        aic = jax.nn.sigmoid(a0[d] + _dot(ad, iclr_up[d]))
        kd = k * (1.0 + (aic - 1.0) * k_a)
        return r, k, v, kkn, logw, aic, kd, ad, gd

    r, k, v, kkn, logw, aic, kd, ad, gd = streams(r_f, k_f, v_f, lo_f, 0)
    aic_o = jax.nn.sigmoid(a0[1] + _dot(ad, iclr_up[1]))
    kd_o = k * (1.0 + (aic_o - 1.0) * k_a)
    z_ref[...] = _head_sum(r * (kd + kd_o) * r_k, ones_bd) * v
    gate_ref[...] = _dot(jax.nn.sigmoid(gd), g_up[...])
    chains = [tuple(t[j * CHUNK:(j + 1) * CHUNK] for t in (r, kd, v, kkn, aic, logw)) + (False,)
              for j in range(nsub)]
    r, k, v, kkn, logw, aic, kd, ad, gd = streams(r_b, k_b, v_b, lo_b, 1)
    chains += [tuple(t[j * CHUNK:(j + 1) * CHUNK] for t in (r, kd, v, kkn, aic, logw)) + (True,)
               for j in range(nsub)]
    affine = _chunk_affine(chains, masks)
    fwd, bwd = affine[:nsub], affine[nsub:]

    for d, (steps, y_ref) in enumerate(((list(range(nsub)), yf_ref), (list(reversed(range(nsub))), yb_ref))):
        affine_d = fwd if d == 0 else bwd
        z = [zst[d, p] for p in range(N_PAIR)]
        for j in steps:
            mz = [_dot(jnp.concatenate([affine_d[j][p][2], affine_d[j][p][1]], axis=0), z[p])
                  for p in range(N_PAIR)]
            y_ref[j * CHUNK:(j + 1) * CHUNK, :] = jnp.concatenate(
                [affine_d[j][p][0] + mz[p][PW:] for p in range(N_PAIR)], axis=1)
            z = [mz[p][0:PW] + affine_d[j][p][3] for p in range(N_PAIR)]
        for p in range(N_PAIR):
            zst[d, p] = z[p]

    @pl.when(c == nsteps - 1)
    def _():
        sfin_ref[...] = zst[...]


def _wkv(px, lw, s0, nsub):
    b, l, _ = px.shape
    ng = lw["ng"]
    d_rwkv = ng * GW
    t = nsub * CHUNK
    nsteps = l // t
    lo_blk = 3 * d_rwkv // (4 * LORA_PAD)
    lo_w = 4 * LORA_PAD

    def main_specs(blk):
        return [pl.BlockSpec((None, t, GW), lambda bi, g, c, s=s: (bi, blk(c), s * ng + g)) for s in range(3)] + [
            pl.BlockSpec((None, t, lo_w), lambda bi, g, c: (bi, blk(c), lo_blk))]

    fblk = lambda c: c
    bblk = lambda c: nsteps - 1 - c
    in_specs = (main_specs(fblk) + main_specs(bblk) + [
        pl.BlockSpec((8, GW), lambda bi, g, c: (0, g)),
        pl.BlockSpec((2, LORA_PAD, GW), lambda bi, g, c: (0, 0, g)),
        pl.BlockSpec((2, LORA_PAD, GW), lambda bi, g, c: (0, 0, g)),
        pl.BlockSpec((2 * LORA_PAD, GW), lambda bi, g, c: (0, g)),
        pl.BlockSpec((None, 2, N_PAIR, PW, PW), lambda bi, g, c: (bi, 0, g, 0, 0)),
    ])
    yspec = lambda blk: pl.BlockSpec((None, t, GW), lambda bi, g, c: (bi, blk(c), g))
    out_specs = [yspec(fblk), yspec(bblk), yspec(fblk), yspec(fblk),
                 pl.BlockSpec((None, 2, N_PAIR, PW, PW), lambda bi, g, c: (bi, 0, g, 0, 0))]
    ysh = jax.ShapeDtypeStruct((b, l, d_rwkv), F32)
    return pl.pallas_call(
        functools.partial(_wkv_kernel, nsub),
        grid=(b, ng, nsteps),
        in_specs=in_specs,
        out_specs=out_specs,
        out_shape=[ysh, ysh, ysh, ysh, jax.ShapeDtypeStruct(s0.shape, F32)],
        scratch_shapes=[pltpu.VMEM((2, N_PAIR, PW, PW), F32)],
        compiler_params=_cparams(("parallel", "parallel", "arbitrary")),
        name="wkv",
    )(*([px] * 8), lw["pvec"], lw["dec_up"], lw["iclr_up"], lw["g_up"], s0)


def _dft_ch_kernel(u_ref, t_ref, o_ref):
    u = u_ref[...]
    hi = u.astype(BF16)
    lo = (u - hi.astype(F32)).astype(BF16)
    tab = t_ref[...]
    o_ref[...] = (_dot(hi, tab) + _dot(lo, tab)).astype(BF16)


def _dft_channels(px, tabs, ft_blk, d_f):
    b, l, _ = px.shape
    tm = min(ROWS_DFT, l)
    nt = l // tm
    return pl.pallas_call(
        _dft_ch_kernel,
        grid=(b, 2, nt),
        in_specs=[pl.BlockSpec((None, tm, d_f), lambda bi, s, i: (bi, i, ft_blk)),
                  pl.BlockSpec((None, d_f, d_f), lambda bi, s, i: (s, 0, 0))],
        out_specs=pl.BlockSpec((tm, d_f), lambda bi, s, i: (s * nt + i, bi)),
        out_shape=jax.ShapeDtypeStruct((2 * l, b * d_f), BF16),
        compiler_params=_cparams(("parallel", "parallel", "parallel")),
        name="dft_channels",
    )(px, tabs)


def _dft_seq_kernel(w_ref, u_ref, o_ref, acc_ref):
    kk = pl.program_id(1)

    @pl.when(kk == 0)
    def _():
        acc_ref[...] = jnp.zeros_like(acc_ref)

    acc_ref[...] += jnp.dot(w_ref[...], u_ref[...], preferred_element_type=F32)

    @pl.when(kk == pl.num_programs(1) - 1)
    def _():
        o_ref[...] = acc_ref[...]


def _dft_seq(wtab, uu):
    l, k2 = wtab.shape
    n = uu.shape[1]
    tm = min(ROWS_DFT_SEQ, l)
    tk = min(DEPTH_DFT_SEQ, k2)
    return pl.pallas_call(
        _dft_seq_kernel,
        grid=(l // tm, k2 // tk),
        in_specs=[pl.BlockSpec((tm, tk), lambda i, kk: (i, kk)),
                  pl.BlockSpec((tk, n), lambda i, kk: (kk, 0))],
        out_specs=pl.BlockSpec((tm, n), lambda i, kk: (i, 0)),
        out_shape=jax.ShapeDtypeStruct((l, n), F32),
        scratch_shapes=[pltpu.VMEM((tm, n), F32)],
        compiler_params=_cparams(("parallel", "arbitrary")),
        name="dft_seq",
    )(wtab, uu)


def _seq_table(l):
    f = math.gcd(l, 128)

    def thin(n, stride):
        p = lax.broadcasted_iota(jnp.int32, (l, n), 0)
        q = lax.broadcasted_iota(jnp.int32, (l, n), 1) * stride
        ang = ((p * q) % l).astype(F32) * (2.0 * math.pi / l)
        return jnp.cos(ang), jnp.sin(ang)

    c1, s1 = (t[:, :, None] for t in thin(l // f, f))
    c2, s2 = (t[:, None, :] for t in thin(f, 1))
    return jnp.concatenate([(c1 * c2 - s1 * s2).reshape(l, l), (-(s1 * c2 + c1 * s2)).reshape(l, l)],
                           axis=1).astype(BF16)


def _channel_tables(l, d_f):
    cc = lax.broadcasted_iota(jnp.int32, (d_f, d_f), 0)
    qq = lax.broadcasted_iota(jnp.int32, (d_f, d_f), 1)
    a2 = (((cc % HEAD) * (qq % HEAD)) % HEAD).astype(F32) * (2.0 * math.pi / HEAD)
    same = (cc // HEAD) == (qq // HEAD)
    scale = 1.0 / math.sqrt(l * HEAD)
    return jnp.stack([jnp.where(same, jnp.cos(a2), 0.0), jnp.where(same, jnp.sin(a2), 0.0)]) * scale


def _fft_stage1_kernel(*refs):
    u_refs, (ct_ref, t2_ref, tw_ref, o_ref) = refs[:-4], refs[-4:]
    i = pl.program_id(1)
    l2 = t2_ref.shape[0] // 2
    for j in range(FFT_SUB):
        u = jnp.concatenate([r[pl.ds(i * FFT_SUB + j, l2, stride=FFT_L1), :] for r in u_refs], axis=1)
        z = jnp.concatenate([_dot(u, ct_ref[0]), -_dot(u, ct_ref[1])], axis=0)
        bm = _dot(t2_ref[...], z)
        br, bi = bm[0:l2], bm[l2:]
        tw = tw_ref[j]
        cs, sn = tw[:, 0:1], tw[:, 1:2]
        comp = (br * cs + bi * sn, bi * cs - br * sn)
        for c in range(2):
            for pb in range(l2 // FFT_SUB):
                o_ref[pb, c, j * FFT_SUB:(j + 1) * FFT_SUB, :] = comp[c][pb * FFT_SUB:(pb + 1) * FFT_SUB]


def _fft_stage2_kernel(t1_ref, b_ref, o_ref):
    x = jnp.concatenate([b_ref[0], b_ref[1]], axis=0)
    y = _dot(t1_ref[...], x)
    for p1 in range(o_ref.shape[0]):
        o_ref[p1] = y[p1 * FFT_SUB:(p1 + 1) * FFT_SUB]


def _fft_tables(l):
    l1, l2 = FFT_L1, l // FFT_L1

    def cs(n, m, period):
        p = lax.broadcasted_iota(jnp.int32, (n, m), 0)
        q = lax.broadcasted_iota(jnp.int32, (n, m), 1)
        ang = ((p * q) % period).astype(F32) * (2.0 * math.pi / period)
        return jnp.cos(ang), jnp.sin(ang)

    c2, s2 = cs(l2, l2, l2)
    t2 = jnp.concatenate([jnp.concatenate([c2, s2], axis=1), jnp.concatenate([-s2, c2], axis=1)], axis=0)
    tw = jnp.stack(cs(l1, l2, l), axis=-1)
    c1, s1 = cs(l1, l1, l1)
    t1 = jnp.kron(jnp.concatenate([c1, s1], axis=1), jnp.eye(FFT_SUB, dtype=F32))
    return t2.astype(BF16), tw, t1.astype(BF16)


def _fourier_fft(px, ctab, tabs, ft_blk, d_f):
    b, l, n = px.shape
    l1, l2 = FFT_L1, l // FFT_L1
    npb = l2 // FFT_SUB
    nslab = d_f // LANES
    t2, tw, t1 = tabs
    stage1 = pl.pallas_call(
        _fft_stage1_kernel,
        grid=(b, l1 // FFT_SUB),
        in_specs=[pl.BlockSpec((None, l, LANES), lambda bi, i, k=k: (bi, 0, ft_blk * nslab + k))
                  for k in range(nslab)] + [
                  pl.BlockSpec(ctab.shape, lambda bi, i: (0, 0, 0)),
                  pl.BlockSpec(t2.shape, lambda bi, i: (0, 0)),
                  pl.BlockSpec((FFT_SUB, l2, 2), lambda bi, i: (i, 0, 0))],
        out_specs=pl.BlockSpec((None, npb, 2, FFT_SUB * FFT_SUB, d_f), lambda bi, i: (bi, 0, 0, i, 0)),
        out_shape=jax.ShapeDtypeStruct((b, npb, 2, l1 * FFT_SUB, d_f), F32),
        compiler_params=_cparams(("parallel", "arbitrary")),
        name="fft_stage1",
    )(*([px] * nslab), ctab, t2, tw)
    y = pl.pallas_call(
        _fft_stage2_kernel,
        grid=(b, npb),
        in_specs=[pl.BlockSpec(t1.shape, lambda bi, j: (0, 0)),
                  pl.BlockSpec((None, None, 2, l1 * FFT_SUB, d_f), lambda bi, j: (bi, j, 0, 0, 0))],
        out_specs=pl.BlockSpec((None, l1, None, FFT_SUB, d_f), lambda bi, j: (bi, 0, j, 0, 0)),
        out_shape=jax.ShapeDtypeStruct((b, l1, npb, FFT_SUB, d_f), F32),
        compiler_params=_cparams(("parallel", "parallel")),
        name="fft_stage2",
    )(t1, stage1)
    return y.reshape(b, l, d_f)


def _fourier_dense(px, wtab, ctab, ft_blk, d_f):
    b, l, _ = px.shape
    y = _dft_seq(wtab, _dft_channels(px, ctab, ft_blk, d_f))
    return y.reshape(l, b, d_f).transpose(1, 0, 2)


def _mix_out_kernel(period, yf_ref, yb_ref, z_ref, gate_ref, cg_ref, cx_ref, cb_ref, ft_ref, x_ref,
                    ga_ref, lnw_ref, lnb_ref, cw_ref, w_ref, o_ref, mix_ref):
    d_rwkv = yf_ref.shape[1]
    d_conv = cg_ref.shape[1]
    tm = yf_ref.shape[0]
    r = lax.broadcasted_iota(jnp.int32, (GW, GW), 0)
    c = lax.broadcasted_iota(jnp.int32, (GW, GW), 1)
    ones_bd = jnp.where((r // HEAD) == (c // HEAD), 1.0, 0.0).astype(BF16)
    inv_n = 1.0 / HEAD
    for s in range(d_rwkv // GW):
        sl = slice(s * GW, (s + 1) * GW)
        y = yf_ref[:, sl] + yb_ref[:, sl]
        mu = _dot_lhs_f32(y, ones_bd) * inv_n
        dlt = y - mu
        var = _dot_lhs_f32(dlt * dlt, ones_bd) * inv_n
        yn = dlt * lax.rsqrt(var + GN_EPS) * lnw_ref[:, sl] + lnb_ref[:, sl] + z_ref[:, sl]
        mix_ref[:, sl] = (yn * gate_ref[:, sl]).astype(BF16)
    u = cg_ref[...].astype(F32) * cx_ref[...].astype(F32)
    rowid = lax.broadcasted_iota(jnp.int32, u.shape, 0) % period
    up = jnp.where(rowid == 0, 0.0, pltpu.roll(u, 1, 0))
    dn = jnp.where(rowid == period - 1, 0.0, pltpu.roll(u, tm - 1, 0))
    cw = cw_ref[...]
    conv = cb_ref[...].astype(F32) * (up * cw[0:1] + u * cw[1:2] + dn * cw[2:3])
    mix_ref[:, d_rwkv:d_rwkv + d_conv] = conv.astype(BF16)
    mix_ref[:, d_rwkv + d_conv:] = ft_ref[...].astype(BF16)
    o_ref[...] = x_ref[...] + ga_ref[...] * jnp.dot(mix_ref[...], w_ref[...], preferred_element_type=F32)


def _mix_out(yf, yb, z, gate, px, ft, x, ga, lw, period, conv_blk):
    b, l, d = x.shape
    d_rwkv = yf.shape[2]
    d_conv = lw["conv_w"].shape[1]
    w_out, layer = lw["w_out"]
    d_f = ft.shape[2]
    tm = min(ROWS_MIX_OUT, l)
    assert tm % period == 0
    yspec = pl.BlockSpec((None, tm, d_rwkv), lambda bi, i: (bi, i, 0))
    cspec = lambda off: pl.BlockSpec((None, tm, d_conv), lambda bi, i: (bi, i, conv_blk + off))
    return pl.pallas_call(
        functools.partial(_mix_out_kernel, period),
        grid=(b, l // tm),
        in_specs=[yspec, yspec, yspec, yspec, cspec(0), cspec(1), cspec(2),
                  pl.BlockSpec((None, tm, d_f), lambda bi, i: (bi, i, 0)),
                  pl.BlockSpec((None, tm, d), lambda bi, i: (bi, i, 0)),
                  pl.BlockSpec((None, 1, d), lambda bi, i: (bi, 0, 0)),
                  pl.BlockSpec((1, d_rwkv), lambda bi, i: (0, 0)),
                  pl.BlockSpec((1, d_rwkv), lambda bi, i: (0, 0)),
                  pl.BlockSpec((3, d_conv), lambda bi, i: (0, 0)),
                  pl.BlockSpec((None,) + w_out.shape[1:], lambda bi, i: (layer, 0, 0))],
        out_specs=pl.BlockSpec((None, tm, d), lambda bi, i: (bi, i, 0)),
        out_shape=jax.ShapeDtypeStruct((b, l, d), F32),
        scratch_shapes=[pltpu.VMEM((tm, w_out.shape[1]), BF16)],
        compiler_params=_cparams(("parallel", "parallel")),
        name="mix_out",
    )(yf, yb, z, gate, px, px, px, ft, x, ga, lw["ln_w"], lw["ln_b"], lw["conv_w"], w_out)


def _ffn_kernel(final, x_ref, g_ref, sh_ref, sc_ref, ga_ref, wg_ref, wu_ref, wd_ref, gf_ref, o_ref,
                h_ref, acc_ref):
    j = pl.program_id(2)

    @pl.when(j == 0)
    def _():
        h_ref[...] = _rms_mod(x_ref[...], g_ref[...], sh_ref[...], sc_ref[...]).astype(BF16)
        acc_ref[...] = jnp.zeros_like(acc_ref)

    h = h_ref[...]
    a = jnp.dot(h, wg_ref[...], preferred_element_type=F32)
    u = jnp.dot(h, wu_ref[...], preferred_element_type=F32)
    t = (a * jax.nn.sigmoid(a)) * u
    acc_ref[...] += jnp.dot(t.astype(BF16), wd_ref[...], preferred_element_type=F32)

    @pl.when(j == pl.num_programs(2) - 1)
    def _():
        xn = x_ref[...] + ga_ref[...] * acc_ref[...]
        if final:
            ms = jnp.mean(xn * xn, axis=-1, keepdims=True)
            xn = xn * lax.rsqrt(ms + RMS_EPS) * gf_ref[...]
        o_ref[...] = xn


def _ffn(x, g, shift, scale, ga, wg, wu, wd, g_final, final):
    b, l, d = x.shape
    tf = COL_TILE
    (wg, layer), (wu, _), (wd, _) = wg, wu, wd
    ff = wg.shape[2]
    tm = min(ROWS_FFN, l)
    vec = pl.BlockSpec((None, 1, d), lambda bi, i, j: (bi, 0, 0))
    gspec = pl.BlockSpec((1, d), lambda bi, i, j: (0, 0))
    return pl.pallas_call(
        functools.partial(_ffn_kernel, final),
        grid=(b, l // tm, ff // tf),
        in_specs=[pl.BlockSpec((None, tm, d), lambda bi, i, j: (bi, i, 0)),
                  gspec, vec, vec, vec,
                  pl.BlockSpec((None, d, tf), lambda bi, i, j: (layer, 0, j)),
                  pl.BlockSpec((None, d, tf), lambda bi, i, j: (layer, 0, j)),
                  pl.BlockSpec((None, tf, d), lambda bi, i, j: (layer, j, 0)),
                  gspec],
        out_specs=pl.BlockSpec((None, tm, d), lambda bi, i, j: (bi, i, 0)),
        out_shape=jax.ShapeDtypeStruct((b, l, d), F32),
        scratch_shapes=[pltpu.VMEM((tm, d), BF16), pltpu.VMEM((tm, d), F32)],
        compiler_params=_cparams(("parallel", "parallel", "arbitrary")),
        name="ffn",
    )(x, g, shift, scale, ga, wg, wu, wd, g_final)


def _pad_lora(m, dims, stop):
    d_rwkv, _, _, dl, il, _ = dims
    ad0 = 3 * d_rwkv + dl
    gd0 = ad0 + il
    padl = jnp.zeros(m.shape[:-1] + (LORA_PAD - dl,), m.dtype)
    padi = jnp.zeros(m.shape[:-1] + (LORA_PAD - il,), m.dtype)
    return jnp.concatenate([m[..., :ad0], padl, m[..., ad0:gd0], padi, m[..., gd0:stop]], axis=-1)


def _layer_weights(i, big, rw_shift, dec_w0, dec_up, iclr_a0, iclr_up, k_k, k_a, r_k, ln_w, ln_b, g_up,
                   conv_w, dims):
    d_rwkv, d_conv, d_f, dl, il, gl = dims
    ng = d_rwkv // GW
    rest0 = 3 * d_rwkv + dl + il + gl
    mix = _pad_lora(rw_shift[i], dims, rest0)
    ts = STEP_TILES * COL_TILE
    ident = jnp.broadcast_to(jnp.array([[0.0], [1.0], [0.0]], F32), (3, -mix.shape[1] % ts))
    mix = jnp.concatenate([mix, ident], axis=1)
    pad_rows = lambda m, n: jnp.concatenate([m, jnp.zeros(m.shape[:-2] + (n - m.shape[-2], m.shape[-1]), m.dtype)], -2)
    zero = jnp.zeros((d_rwkv,), F32)
    pvec = jnp.stack([k_k[i], k_a[i], r_k[i].reshape(-1), dec_w0[i, 0], dec_w0[i, 1],
                      iclr_a0[i, 0], iclr_a0[i, 1], zero])
    return dict(
        ng=ng,
        w_in=(big["w_in"], i),
        mix=mix,
        pvec=pvec,
        dec_up=pad_rows(dec_up[i], LORA_PAD).astype(BF16),
        iclr_up=pad_rows(iclr_up[i], LORA_PAD).astype(BF16),
        g_up=g_up[i].astype(BF16),
        ln_w=ln_w[i][None, :], ln_b=ln_b[i][None, :],
        conv_w=conv_w[i],
        w_out=(big["w_out"], i),
        w_gate=(big["w_gate"], i), w_up=(big["w_up"], i), w_down=(big["w_down"], i),
    )


def kernel(x, c, ctx, c_ctx, w_mod, b_mod, norm_mix, w_in, rw_shift, dec_w0, dec_up, iclr_a0, iclr_up,
           k_k, k_a, r_k, ln_w, ln_b, g_up, conv_w, w_out, norm_ffn, w_gate, w_up, w_down, norm_final):
    b, l, d = x.shape
    lc = ctx.shape[1]
    depth = w_mod.shape[0]
    d_rwkv = k_k.shape[1]
    d_conv = conv_w.shape[2]
    dl, il, gl = dec_up.shape[2], iclr_up.shape[2], g_up.shape[1]
    d_f = w_out.shape[1] - d_rwkv - d_conv
    assert dl <= LORA_PAD and il <= LORA_PAD and gl == 2 * LORA_PAD and d_rwkv % GW == 0
    assert l % GRID_W == 0 and l % CHUNK == 0 and lc % CHUNK == 0
    nsub_x = math.gcd(l // CHUNK, WKV_SUB)
    nsub_c = math.gcd(lc // CHUNK, WKV_SUB)
    ng = d_rwkv // GW
    n_rw = 3 * d_rwkv + 4 * LORA_PAD
    conv_blk = n_rw // d_conv
    assert conv_blk * d_conv == n_rw and d_f == COL_TILE

    cvec = jnp.concatenate([c, c_ctx[None, :], jnp.zeros((8 - b - 1, d), F32)], axis=0)
    mods = _adaln(cvec, w_mod, b_mod[:, None, :])

    def mod(i, j, ctx_rows):
        m = mods[i, :, j * d:(j + 1) * d]
        if ctx_rows:
            return jnp.broadcast_to(m[b:b + 1], (b, d))[:, None, :]
        return m[:b][:, None, :]

    fft_x = l % (FFT_L1 * 128) == 0
    wtab_c, ctab_c = _seq_table(lc), _channel_tables(lc, d_f)
    ctab_x = _channel_tables(l, d_f)
    if fft_x:
        ftab_x = _fft_tables(l)
    else:
        wtab_x = _seq_table(l)
    dims = (d_rwkv, d_conv, d_f, dl, il, gl)
    s_zero = jnp.zeros((b, 2, ng * N_PAIR, PW, PW), F32)
    gfin = norm_final[None, :]
    w_in_p = _pad_lora(w_in, dims, w_in.shape[2])
    assert (w_in_p.shape[2] - COL_TILE) % (STEP_TILES * COL_TILE) == 0
    w_in_p = jnp.concatenate([w_in_p, jnp.zeros(w_in_p.shape[:2] + ((STEP_TILES - 1) * COL_TILE,), F32)],
                             axis=2).astype(BF16)
    big = dict(w_in=w_in_p, w_out=w_out.astype(BF16),
               w_gate=w_gate.astype(BF16), w_up=w_up.astype(BF16), w_down=w_down.astype(BF16))
    xc = ctx
    for i in range(depth):
        lw = _layer_weights(i, big, rw_shift, dec_w0, dec_up, iclr_a0, iclr_up, k_k, k_a, r_k, ln_w, ln_b,
                            g_up, conv_w, dims)
        gmix = norm_mix[i][None, :]
        gffn = norm_ffn[i][None, :]
        last = i == depth - 1
        pc, pcf = _in_proj(xc, gmix, mod(i, 0, True), mod(i, 1, True), lw["w_in"], lw["mix"])
        px, pxf = _in_proj(x, gmix, mod(i, 0, False), mod(i, 1, False), lw["w_in"], lw["mix"])
        yfc, ybc, zc, gc, s_ctx = _wkv(pc, lw, s_zero, nsub_c)
        yfx, ybx, zx, gx, _ = _wkv(px, lw, s_ctx, nsub_x)
        if fft_x:
            ftx = _fourier_fft(pxf, ctab_x.astype(BF16), ftab_x, 0, d_f)
        else:
            ftx = _fourier_dense(pxf, wtab_x, ctab_x, 0, d_f)
        x = _mix_out(yfx, ybx, zx, gx, px, ftx, x, mod(i, 2, False), lw, GRID_W, conv_blk)
        x = _ffn(x, gffn, mod(i, 3, False), mod(i, 4, False), mod(i, 5, False),
                 lw["w_gate"], lw["w_up"], lw["w_down"], gfin, last)
        if not last:
            ftc = _fourier_dense(pcf, wtab_c, ctab_c, 0, d_f)
            xc = _mix_out(yfc, ybc, zc, gc, pc, ftc, xc, mod(i, 2, True), lw, lc, conv_blk)
            xc = _ffn(xc, gffn, mod(i, 3, True), mod(i, 4, True), mod(i, 5, True),
                      lw["w_gate"], lw["w_up"], lw["w_down"], gfin, False)
    return x
```
